```python
import math
import jax, jax.numpy as jnp
from jax import lax
import numpy as np

D_MODEL = 2048
BATCH = 16
SEQ = 256
DEPTH = 2
DEC_BATCH = 2
DEC_SEQ = 1024
PAST_LEN = 256

F32 = jnp.float32
GRID_W = 64
ROPE_THETA = 10000.0
NORM_EPS = 1e-6
QBLOCK = 128

HGRN_HEADS = 8
HGRN_DK = 128
HGRN_DV = 128
HGRN_W = HGRN_HEADS * HGRN_DK
HGRN_CHUNK = 16

DIFF_HEADS = 8
DIFF_DK = 64
DIFF_DV = 2 * DIFF_DK
DIFF_W = DIFF_HEADS * DIFF_DV

GQA_HEADS = 8
GQA_KV_HEADS = 2
GQA_DH = 128
GQA_REP = GQA_HEADS // GQA_KV_HEADS
GQA_W = GQA_HEADS * GQA_DH

N_BRANCH = 3
BRANCH_W = 1024

PEER_HEADS = 8
PEER_DK = 256
N_KEYS = 128
N_EXPERTS = N_KEYS * N_KEYS
PEER_TOPK = 16
PEER_BLOCK = 64

IN_SPLITS = (HGRN_W, HGRN_W, HGRN_W, HGRN_W, HGRN_W,
             DIFF_HEADS * 2 * DIFF_DK, DIFF_HEADS * 2 * DIFF_DK, DIFF_W,
             GQA_W, GQA_KV_HEADS * GQA_DH, GQA_KV_HEADS * GQA_DH,
             N_BRANCH * D_MODEL)
IN_WIDTH = 15872

kernel_name = 'hybrid_diffusion_hgrn2_diffattn_gqa_peer_step'


def rms_norm(x, gain):
    x32 = x.astype(F32)
    y = x32 * lax.rsqrt(jnp.mean(x32 * x32, axis=-1, keepdims=True) + NORM_EPS)
    return (y * gain.astype(F32)).astype(x.dtype)


def axial_rope(x):
    T, dim = x.shape[1], x.shape[-1]
    rows = T // GRID_W
    row = jnp.repeat(jnp.arange(rows, dtype=F32), GRID_W)
    col = jnp.tile(jnp.arange(GRID_W, dtype=F32), rows)
    nfreq = dim // 4
    inv_freq = ROPE_THETA ** (-jnp.arange(nfreq, dtype=F32) / nfreq)

    def rot(xh, pos):
        ang = (pos[:, None] * inv_freq)[:, None, :]
        cos, sin = jnp.cos(ang), jnp.sin(ang)
        x1, x2 = xh[..., :nfreq].astype(F32), xh[..., nfreq:].astype(F32)
        return jnp.concatenate([x1 * cos - x2 * sin, x2 * cos + x1 * sin], axis=-1)

    half = dim // 2
    return jnp.concatenate([rot(x[..., :half], row), rot(x[..., half:], col)], axis=-1).astype(x.dtype)


def sweep_query_blocks(block_fn, queries, keys_values):
    B, T = queries[0].shape[:2]
    nb = T // QBLOCK
    qb = tuple(jnp.moveaxis(q.reshape((B, nb, QBLOCK) + q.shape[2:]), 1, 0) for q in queries)
    out = lax.map(lambda blk: block_fn(*blk, *keys_values), qb)
    return jnp.moveaxis(out, 0, 1).reshape((B, T) + out.shape[3:])


def diff_attn_block(q1, q2, k1, k2, v, lam):
    scale = DIFF_DK ** -0.5
    s1 = jnp.einsum('bqhd,bshd->bhqs', q1, k1).astype(F32) * scale
    s2 = jnp.einsum('bqhd,bshd->bhqs', q2, k2).astype(F32) * scale
    p = jax.nn.softmax(s1, axis=-1) - lam * jax.nn.softmax(s2, axis=-1)
    return jnp.einsum('bhqs,bshv->bqhv', p.astype(v.dtype), v)


def gqa_block(q, k, v):
    s = jnp.einsum('bqgrd,bsgd->bgrqs', q, k).astype(F32) * (GQA_DH ** -0.5)
    p = jax.nn.softmax(s, axis=-1).astype(v.dtype)
    return jnp.einsum('bgrqs,bsgd->bqgrd', p, v)


def forget_gate(zf, lb):
    lb = lb.astype(F32)
    logf = jnp.logaddexp(jnp.log(lb), jnp.log1p(-lb) + jax.nn.log_sigmoid(zf.astype(F32)))
    return logf, -jnp.expm1(logf)


def hgrn_chunkwise(q, logf, k, v, s0):
    B, T, H, K = q.shape
    V = v.shape[-1]
    C = HGRN_CHUNK
    N = T // C

    def chunks(a):
        return a.astype(F32).reshape(B, N, C, H, a.shape[-1]).transpose(1, 0, 3, 2, 4)

    qc, fc, kc, vc = chunks(q), chunks(logf), chunks(k), chunks(v)
    b = jnp.cumsum(fc, axis=3)
    b_last = b[:, :, :, -1, :]
    tri = jnp.tril(jnp.ones((C, C), bool))[:, :, None]
    rel = b[:, :, :, :, None, :] - b[:, :, :, None, :, :]
    decay = jnp.exp(jnp.where(tri, rel, -jnp.inf))
    attn = jnp.einsum('nbhtk,nbhsk,nbhtsk->nbhts', qc, kc, decay)
    o_intra = jnp.einsum('nbhts,nbhsv->nbhtv', attn, vc)
    q_dec = qc * jnp.exp(b)
    k_dec = kc * jnp.exp(b_last[:, :, :, None, :] - b)

    def step(S, inp):
        qd, kd, vv, dl = inp
        o = jnp.einsum('bhtk,bhkv->bhtv', qd, S)
        S = jnp.exp(dl)[..., None] * S + jnp.einsum('bhsk,bhsv->bhkv', kd, vv)
        return S, o

    s_fin, o_inter = lax.scan(step, s0.astype(F32), (q_dec, k_dec, vc, b_last))
    o = (o_intra + o_inter).transpose(1, 0, 3, 2, 4).reshape(B, T, H, V)
    return o.astype(v.dtype), s_fin.astype(v.dtype)


def peer_ffn(h, wq, keys, u, v):
    B, T, D = h.shape
    P = PEER_BLOCK
    xt = h.reshape(B * T // P, P, D)

    def block(xb):
        q = (xb @ wq).reshape(P, PEER_HEADS, 2, PEER_DK // 2)
        s = jnp.einsum('phcd,ckd->phck', q, keys).astype(F32)
        sv, si = lax.top_k(s, PEER_TOPK)
        cand = sv[:, :, 0, :, None] + sv[:, :, 1, None, :]
        cv, ci = lax.top_k(cand.reshape(P, PEER_HEADS, PEER_TOPK * PEER_TOPK), PEER_TOPK)
        i1 = jnp.take_along_axis(si[:, :, 0, :], ci // PEER_TOPK, axis=-1)
        i2 = jnp.take_along_axis(si[:, :, 1, :], ci % PEER_TOPK, axis=-1)
        e = i1 * N_KEYS + i2
        g = jax.nn.softmax(cv, axis=-1)
        a = jax.nn.gelu(jnp.einsum('phkd,pd->phk', u[e], xb), approximate=False)
        w = (g * a.astype(F32)).astype(xb.dtype)
        return jnp.einsum('phk,phkd->pd', w, v[e])

    return lax.map(block, xt).reshape(B, T, D)


def token_mixer(h, p, lb, lam_init, ctx):
    B, T, _ = h.shape
    latent = ctx is not None
    z = h @ p['w_in']
    hq, hf_fwd, hf_bwd, hi, hg, dq, dk, dv, gq, gk, gv, gate_logits = jnp.split(
        z, np.cumsum(IN_SPLITS)[:-1].tolist(), axis=-1)

    q_h = hq.reshape(B, T, HGRN_HEADS, HGRN_DK)
    v_h = hi.reshape(B, T, HGRN_HEADS, HGRN_DV)
    s0 = ctx['state_hgrn'] if latent else jnp.zeros((B, 2, HGRN_HEADS, HGRN_DK, HGRN_DV), F32)
    o_sum = None
    finals = []
    for d, zf in enumerate((hf_fwd, hf_bwd)):
        logf, k_h = forget_gate(zf.reshape(B, T, HGRN_HEADS, HGRN_DK), lb[d].reshape(HGRN_HEADS, HGRN_DK))
        if d == 1:
            o_d, s_d = hgrn_chunkwise(q_h[:, ::-1], logf[:, ::-1], k_h[:, ::-1], v_h[:, ::-1], s0[:, d])
            o_d = o_d[:, ::-1]
        else:
            o_d, s_d = hgrn_chunkwise(q_h, logf, k_h, v_h, s0[:, d])
        o_sum = o_d if o_sum is None else o_sum + o_d
        finals.append(s_d)
    hgrn_out = (rms_norm(o_sum, p['hgrn_norm'])
                * jax.nn.silu(hg.reshape(B, T, HGRN_HEADS, HGRN_DV))).reshape(B, T, HGRN_W)

    dq = rms_norm(dq.reshape(B, T, DIFF_HEADS, 2, DIFF_DK), p['diff_qk_norm'][0])
    dk = rms_norm(dk.reshape(B, T, DIFF_HEADS, 2, DIFF_DK), p['diff_qk_norm'][1])
    dv = dv.reshape(B, T, DIFF_HEADS, DIFF_DV)
    q1, q2, k1, k2 = dq[..., 0, :], dq[..., 1, :], dk[..., 0, :], dk[..., 1, :]
    if latent:
        q1, q2, k1, k2 = axial_rope(q1), axial_rope(q2), axial_rope(k1), axial_rope(k2)
        ck = ctx['diff_k']
        k1 = jnp.concatenate([k1, ck[..., :DIFF_DK]], axis=1)
        k2 = jnp.concatenate([k2, ck[..., DIFF_DK:]], axis=1)
        dv_all = jnp.concatenate([dv, ctx['diff_v']], axis=1)
    else:
        dv_all = dv
        new_diff_k = jnp.concatenate([k1, k2], axis=-1)
        new_diff_v = dv
    lp = p['diff_lambda'].astype(F32)
    lam = jnp.exp(jnp.sum(lp[0] * lp[1])) - jnp.exp(jnp.sum(lp[2] * lp[3])) + lam_init
    o_d = sweep_query_blocks(lambda a, b_, k1_, k2_, v_: diff_attn_block(a, b_, k1_, k2_, v_, lam),
                             (q1, q2), (k1, k2, dv_all))
    diff_out = (rms_norm(o_d, p['diff_subln']) * (1.0 - lam_init)).reshape(B, T, DIFF_W)

    gq = rms_norm(gq.reshape(B, T, GQA_HEADS, GQA_DH), p['gqa_qk_norm'][0])
    gk = rms_norm(gk.reshape(B, T, GQA_KV_HEADS, GQA_DH), p['gqa_qk_norm'][1])
    gv = gv.reshape(B, T, GQA_KV_HEADS, GQA_DH)
    if latent:
        gq, gk = axial_rope(gq), axial_rope(gk)
        gk_all = jnp.concatenate([gk, ctx['gqa_k']], axis=1)
        gv_all = jnp.concatenate([gv, ctx['gqa_v']], axis=1)
    else:
        gk_all, gv_all = gk, gv
        new_gqa_k, new_gqa_v = gk, gv
    o_g = sweep_query_blocks(gqa_block, (gq.reshape(B, T, GQA_KV_HEADS, GQA_REP, GQA_DH),), (gk_all, gv_all))
    gqa_out = o_g.reshape(B, T, GQA_W)

    branches = jnp.stack([hgrn_out, diff_out, gqa_out], axis=2)
    proj = jnp.einsum('btci,cid->btcd', branches, p['w_branch'])
    gates = jax.nn.sigmoid(gate_logits.reshape(B, T, N_BRANCH, D_MODEL))
    out = jnp.sum(gates * proj, axis=2) @ p['w_out']
    if latent:
        return out, None
    return out, (new_diff_k, new_diff_v, new_gqa_k, new_gqa_v, jnp.stack(finals, axis=1))


def trunk_layer(x, cond, p, lb, lam_init, ctx):
    mod = jax.nn.silu(cond) @ p['mod_w'] + p['mod_b']
    sh1, sc1, g1, sh2, sc2, g2 = jnp.split(mod[:, None, :], 6, axis=-1)
    h = rms_norm(x, p['norm_mix']) * (1 + sc1) + sh1
    mix, ctx_out = token_mixer(h, p, lb, lam_init, ctx)
    x = x + g1 * mix
    h = rms_norm(x, p['norm_ffn']) * (1 + sc2) + sh2
    x = x + g2 * peer_ffn(h, p['peer_wq'], p['peer_keys'], p['peer_u'], p['peer_v'])
    return x, ctx_out


def setup_inputs(seed: int = 0) -> dict:
    key = jax.random.key(seed)
    ks = jax.random.split(key, 32)

    def nrm(k, shape, s):
        return jax.random.normal(k, shape, F32) * s

    D = D_MODEL
    return {
        'x_prompt': nrm(ks[0], (BATCH, SEQ, D), 1.0),
        'x_sample': nrm(ks[1], (DEC_BATCH, DEC_SEQ, D), 1.0),
        'c': nrm(ks[2], (DEC_BATCH, D), 1.0),
        'cache_diff_k': nrm(ks[3], (DEC_BATCH, DEPTH, PAST_LEN, DIFF_HEADS, 2 * DIFF_DK), 1.0),
        'cache_diff_v': nrm(ks[4], (DEC_BATCH, DEPTH, PAST_LEN, DIFF_HEADS, DIFF_DV), 1.0),
        'cache_gqa_k': nrm(ks[5], (DEC_BATCH, DEPTH, PAST_LEN, GQA_KV_HEADS, GQA_DH), 1.0),
        'cache_gqa_v': nrm(ks[6], (DEC_BATCH, DEPTH, PAST_LEN, GQA_KV_HEADS, GQA_DH), 1.0),
        'state_hgrn': nrm(ks[7], (DEC_BATCH, DEPTH, 2, HGRN_HEADS, HGRN_DK, HGRN_DV), 0.5),
        'c_ctx': nrm(ks[8], (D,), 1.0),
        'mod_w': nrm(ks[9], (DEPTH, D, 6 * D), 0.5 * D ** -0.5),
        'mod_b': nrm(ks[10], (DEPTH, 6 * D), 0.02),
        'norm_mix': 1.0 + nrm(ks[11], (DEPTH, D), 0.02),
        'norm_ffn': 1.0 + nrm(ks[12], (DEPTH, D), 0.02),
        'w_in': nrm(ks[13], (DEPTH, D, IN_WIDTH), D ** -0.5),
        'hgrn_lb': nrm(ks[14], (2, DEPTH, HGRN_W), 0.5),
        'hgrn_norm': 1.0 + nrm(ks[15], (DEPTH, HGRN_DV), 0.02),
        'diff_qk_norm': 1.0 + nrm(ks[16], (DEPTH, 2, DIFF_DK), 0.02),
        'diff_lambda': nrm(ks[17], (DEPTH, 4, DIFF_DK), 0.1),
        'diff_subln': 1.0 + nrm(ks[18], (DEPTH, DIFF_DV), 0.02),
        'gqa_qk_norm': 1.0 + nrm(ks[19], (DEPTH, 2, GQA_DH), 0.02),
        'w_branch': nrm(ks[20], (DEPTH, N_BRANCH, BRANCH_W, D), BRANCH_W ** -0.5),
        'w_out': nrm(ks[21], (DEPTH, D, D), D ** -0.5),
        'peer_wq': nrm(ks[22], (DEPTH, D, PEER_HEADS * PEER_DK), D ** -0.5),
        'peer_keys': nrm(ks[23], (DEPTH, 2, N_KEYS, PEER_DK // 2), (PEER_DK // 2) ** -0.5),
        'peer_u': nrm(ks[24], (DEPTH, N_EXPERTS, D), D ** -0.5),
        'peer_v': nrm(ks[25], (DEPTH, N_EXPERTS, D), 0.1),
    }


def reference(x_prompt, x_sample, c, cache_diff_k, cache_diff_v, cache_gqa_k, cache_gqa_v, state_hgrn,
              c_ctx, mod_w, mod_b, norm_mix, norm_ffn, w_in, hgrn_lb, hgrn_norm, diff_qk_norm,
              diff_lambda, diff_subln, gqa_qk_norm, w_branch, w_out, peer_wq, peer_keys, peer_u, peer_v):
    lb_all = jnp.cumsum(jax.nn.softmax(hgrn_lb.astype(F32), axis=1), axis=1)
    lb_all = lb_all - lb_all[:, :1]
    y_prompt = x_prompt
    y_sample = x_sample
    nk, nv, ngk, ngv, nst = [], [], [], [], []
    for l in range(DEPTH):
        p = {'mod_w': mod_w[l], 'mod_b': mod_b[l], 'norm_mix': norm_mix[l], 'norm_ffn': norm_ffn[l],
             'w_in': w_in[l], 'hgrn_norm': hgrn_norm[l], 'diff_qk_norm': diff_qk_norm[l],
             'diff_lambda': diff_lambda[l], 'diff_subln': diff_subln[l], 'gqa_qk_norm': gqa_qk_norm[l],
             'w_branch': w_branch[l], 'w_out': w_out[l], 'peer_wq': peer_wq[l], 'peer_keys': peer_keys[l],
             'peer_u': peer_u[l], 'peer_v': peer_v[l]}
        lam_init = 0.8 - 0.6 * math.exp(-0.3 * l)
        lb = lb_all[:, l]
        y_prompt, ctx_out = trunk_layer(y_prompt, c_ctx[None, :], p, lb, lam_init, None)
        nk.append(ctx_out[0]); nv.append(ctx_out[1]); ngk.append(ctx_out[2]); ngv.append(ctx_out[3]); nst.append(ctx_out[4])
        ctx_l = {'diff_k': cache_diff_k[:, l], 'diff_v': cache_diff_v[:, l], 'gqa_k': cache_gqa_k[:, l],
                 'gqa_v': cache_gqa_v[:, l], 'state_hgrn': state_hgrn[:, l]}
        y_sample, _ = trunk_layer(y_sample, c, p, lb, lam_init, ctx_l)
    new_cache_diff_k = jnp.stack(nk, axis=1)
    new_cache_diff_v = jnp.stack(nv, axis=1)
    new_cache_gqa_k = jnp.stack(ngk, axis=1)
    new_cache_gqa_v = jnp.stack(ngv, axis=1)
    new_state_hgrn = jnp.stack(nst, axis=1)
    return (y_prompt, y_sample, new_cache_diff_k, new_cache_diff_v, new_cache_gqa_k, new_cache_gqa_v, new_state_hgrn)
```

```python
import functools
import math

import numpy as np
import jax
import jax.numpy as jnp
from jax import lax
from jax.experimental import pallas as pl
from jax.experimental.pallas import tpu as pltpu

F32 = jnp.float32
BF16 = jnp.bfloat16

D_MODEL = 2048
BATCH = 16
SEQ = 256
DEPTH = 2
DEC_BATCH = 2
DEC_SEQ = 1024
PAST_LEN = 256
GRID_W = 64
ROPE_THETA = 10000.0
NORM_EPS = 1e-6

HGRN_HEADS = 8
HGRN_DK = 128
HGRN_W = 1024
DIFF_HEADS = 8
DIFF_DK = 64
GQA_HEADS = 8
GQA_KV_HEADS = 2
GQA_DH = 128
GQA_REP = 4
N_BRANCH = 3
BRANCH_W = 1024
PEER_HEADS = 8
N_KEYS = 128
N_EXPERTS = N_KEYS * N_KEYS
PEER_TOPK = 16
IN_WIDTH = 15872

TP = BATCH * SEQ
TS = DEC_BATCH * DEC_SEQ
TT = TP + TS

C_HQ, C_HF0, C_HF1, C_HI, C_HG = 0, 1024, 2048, 3072, 4096
C_DQ, C_DK, C_DV = 5120, 6144, 7168
C_GQ, C_GK, C_GV = 8192, 9216, 9472
C_GATE = 9728

LANES = 128
CH = 128
NCH = TT // CH
NPC = TP // CH
CH_PER_PROMPT = SEQ // CH
CH_PER_SAMPLE = DEC_SEQ // CH
N_LEVELS = 7
NEG_INF = float("-inf")
POS_INF = float("inf")
MIN_NORMAL = 1.1754944e-38

_NT = (((1,), (1,)), ((), ()))
_TN = (((0,), (0,)), ((), ()))


def _cp(sem, vmem_mb):
    return pltpu.CompilerParams(dimension_semantics=sem, vmem_limit_bytes=vmem_mb * 1024 * 1024)


def _dot(a, b):
    return jnp.dot(a, b, preferred_element_type=F32)


def _dot_nt(a, b):
    return lax.dot_general(a, b, _NT, preferred_element_type=F32)


def _dot_tn(a, b):
    return lax.dot_general(a, b, _TN, preferred_element_type=F32)


def _split_bf16(x):
    hi = x.astype(BF16)
    lo = (x - hi.astype(F32)).astype(BF16)
    return hi, lo


def _seg_matrix(seg):
    r = lax.broadcasted_iota(jnp.int32, (LANES, LANES), 0) // seg
    c = lax.broadcasted_iota(jnp.int32, (LANES, LANES), 1) // seg
    return (r == c).astype(BF16)


def _rmsnorm_seg(x, gain_row, seg_mat, seg):
    hi, lo = _split_bf16(x * x)
    ss = _dot(hi, seg_mat) + _dot(lo, seg_mat)
    return x * lax.rsqrt(ss * (1.0 / seg) + NORM_EPS) * gain_row


def _mod_row(i, tm):
    n_p = TP // tm
    per = DEC_SEQ // tm
    return jnp.where(i < n_p, 0, 1 + (i - n_p) // per)


def _mod_kernel(cond_ref, w_ref, b_ref, out_ref):
    a = cond_ref[...]
    a = a * jax.nn.sigmoid(a)
    out_ref[0] = _dot(a.astype(BF16), w_ref[0].astype(BF16)) + b_ref[0]


def _modulation(cond8, mod_w, mod_b):
    tn = 1024
    n6 = 6 * D_MODEL
    return pl.pallas_call(
        _mod_kernel,
        grid=(DEPTH, n6 // tn),
        in_specs=[
            pl.BlockSpec((8, D_MODEL), lambda l, j: (0, 0)),
            pl.BlockSpec((1, D_MODEL, tn), lambda l, j: (l, 0, j)),
            pl.BlockSpec((1, 1, tn), lambda l, j: (l, 0, j)),
        ],
        out_specs=pl.BlockSpec((1, 8, tn), lambda l, j: (l, 0, j)),
        out_shape=jax.ShapeDtypeStruct((DEPTH, 8, n6), F32),
        compiler_params=_cp(("parallel", "parallel"), 40),
        name="modulation",
    )(cond8, mod_w, mod_b.reshape(DEPTH, 1, n6))


def _norm_mm_kernel(x_ref, mod_ref, gain_ref, w_ref, *rest, sh_idx, sc_idx, emit_h):
    if emit_h:
        z_ref, h_ref, h_scr = rest
    else:
        z_ref, h_scr = rest

    @pl.when(pl.program_id(1) == 0)
    def _():
        x = x_ref[...]
        ms = jnp.mean(x * x, axis=-1, keepdims=True)
        y = x * lax.rsqrt(ms + NORM_EPS) * gain_ref[...]
        h = y * (1.0 + mod_ref[0, sc_idx:sc_idx + 1, :]) + mod_ref[0, sh_idx:sh_idx + 1, :]
        hb = h.astype(BF16)
        h_scr[...] = hb
        if emit_h:
            h_ref[...] = hb

    z_ref[...] = _dot(h_scr[...], w_ref[...]).astype(z_ref.dtype)


def _norm_matmul(x, mod3, gain, w, *, sh_idx, sc_idx, tm, tn, out_dtype, emit_h, name):
    n = w.shape[1]
    out_shape = [jax.ShapeDtypeStruct((TT, n), out_dtype)]
    out_specs = [pl.BlockSpec((tm, tn), lambda i, j: (i, j))]
    if emit_h:
        out_shape.append(jax.ShapeDtypeStruct((TT, D_MODEL), BF16))
        out_specs.append(pl.BlockSpec((tm, D_MODEL), lambda i, j: (i, 0)))
    return pl.pallas_call(
        functools.partial(_norm_mm_kernel, sh_idx=sh_idx, sc_idx=sc_idx, emit_h=emit_h),
        grid=(TT // tm, n // tn),
        in_specs=[
            pl.BlockSpec((tm, D_MODEL), lambda i, j: (i, 0)),
            pl.BlockSpec((1, 6, D_MODEL), lambda i, j: (_mod_row(i, tm), 0, 0)),
            pl.BlockSpec((1, D_MODEL), lambda i, j: (0, 0)),
            pl.BlockSpec((D_MODEL, tn), lambda i, j: (0, j)),
        ],
        out_specs=out_specs,
        out_shape=out_shape,
        scratch_shapes=[pltpu.VMEM((tm, D_MODEL), BF16)],
        compiler_params=_cp(("parallel", "arbitrary"), 48),
        name=name,
    )(x, mod3, gain, w)


def _level_table(rev):
    t = np.arange(CH)[:, None]
    s = np.arange(CH)[None, :]
    x = t ^ s
    lev = np.full((CH, CH), -1, np.int32)
    nz = x > 0
    lev[nz] = np.floor(np.log2(x[nz])).astype(np.int32)
    valid = (t < s) if rev else (t > s)
    lev = np.where(valid, lev, -1)
    lev[np.arange(CH), np.arange(CH)] = N_LEVELS
    return lev.astype(np.int32)


def _bmid(b, b3, m, rev):
    off = m if rev else m - 1
    if m >= 8:
        pieces = []
        for j in range(CH // (2 * m)):
            idx = j * 2 * m + off
            pieces.append(jnp.broadcast_to(b[idx:idx + 1, :], (2 * m, LANES)))
        return pieces[0] if len(pieces) == 1 else jnp.concatenate(pieces, axis=0)
    sub = lax.broadcasted_iota(jnp.int32, (CH // 8, 8, LANES), 1)
    out = None
    for j in range(8 // (2 * m)):
        idx = j * 2 * m + off
        piece = jnp.broadcast_to(b3[:, idx:idx + 1, :], (CH // 8, 8, LANES))
        out = piece if out is None else jnp.where(sub >= j * 2 * m, piece, out)
    return out.reshape(CH, LANES)


def _hgrn_kernel(q_ref, f_ref, v_ref, lb_ref, s0_ref, lev_ref, o_ref, sfin_ref, st_scr, *, rev):
    i = pl.program_id(0)
    c = (NCH - 1 - i) if rev else i
    is_prompt = c < NPC
    cp_first = (c % CH_PER_PROMPT) == 0
    cp_last = (c % CH_PER_PROMPT) == CH_PER_PROMPT - 1
    cs = jnp.maximum(c - NPC, 0)
    cs_first = (cs % CH_PER_SAMPLE) == 0
    cs_last = (cs % CH_PER_SAMPLE) == CH_PER_SAMPLE - 1
    if rev:
        start_p, end_p, start_s = cp_last, cp_first, cs_last
    else:
        start_p, end_p, start_s = cp_first, cp_last, cs_first

    @pl.when(jnp.logical_and(is_prompt, start_p))
    def _():
        st_scr[...] = jnp.zeros_like(st_scr)

    @pl.when(jnp.logical_and(jnp.logical_not(is_prompt), start_s))
    def _():
        for h in range(HGRN_HEADS):
            st_scr[h] = s0_ref[0, 0, 0, h].T

    row = lax.broadcasted_iota(jnp.int32, (CH, CH), 0)
    col = lax.broadcasted_iota(jnp.int32, (CH, CH), 1)
    tri = ((col >= row) if rev else (col <= row)).astype(BF16)
    lev = lev_ref[...]
    q_bit = 0 if rev else 1

    for h in range(HGRN_HEADS):
        sl = slice(h * HGRN_DK, (h + 1) * HGRN_DK)
        q = q_ref[:, sl]
        zf = f_ref[:, sl]
        v = v_ref[:, sl]
        lb = lb_ref[:, sl]
        f = lb + (1.0 - lb) * jax.nn.sigmoid(zf)
        kk = (1.0 - lb) * jax.nn.sigmoid(-zf)
        logf = jnp.log(jnp.maximum(f, MIN_NORMAL))
        hi, lo = _split_bf16(logf)
        b = _dot(tri, hi) + _dot(tri, lo)
        b3 = b.reshape(CH // 8, 8, LANES)
        vb = v.astype(BF16)

        a = jnp.where(lev == N_LEVELS, _dot_nt(q.astype(BF16), kk.astype(BF16)), 0.0)
        for lm in range(N_LEVELS):
            e = jnp.exp(-jnp.abs(b - _bmid(b, b3, 1 << lm, rev)))
            q_side = ((row >> lm) & 1) == q_bit
            x = (jnp.where(q_side, q, kk) * e).astype(BF16)
            a = jnp.where(lev == lm, _dot_nt(x, x), a)

        b_end = b[0:1, :] if rev else b[CH - 1:CH, :]
        qd = (q * jnp.exp(b)).astype(BF16)
        kd = (kk * jnp.exp(b_end - b)).astype(BF16)
        st = st_scr[h]
        o_ref[:, sl] = _dot(a.astype(BF16), vb) + _dot_nt(qd, st.astype(BF16))
        st_scr[h] = st * jnp.exp(b_end) + _dot_tn(vb, kd)

    @pl.when(jnp.logical_and(is_prompt, end_p))
    def _():
        for h in range(HGRN_HEADS):
            sfin_ref[0, 0, h] = st_scr[h].T


def _hgrn(z, lb_l, state_hgrn, layer, rev):
    d = 1 if rev else 0
    lev = jnp.asarray(_level_table(rev))

    def cidx(i):
        return (NCH - 1 - i) if rev else i

    def s0_map(i):
        b = jnp.clip((cidx(i) - NPC) // CH_PER_SAMPLE, 0, DEC_BATCH - 1)
        return (b, layer, d, 0, 0, 0)

    def sfin_map(i):
        return (jnp.minimum(cidx(i) // CH_PER_PROMPT, BATCH - 1), 0, 0, 0, 0)

    wblk = HGRN_W
    o, sfin = pl.pallas_call(
        functools.partial(_hgrn_kernel, rev=rev),
        grid=(NCH,),
        in_specs=[
            pl.BlockSpec((CH, wblk), lambda i: (cidx(i), C_HQ // wblk)),
            pl.BlockSpec((CH, wblk), lambda i: (cidx(i), (C_HF1 if rev else C_HF0) // wblk)),
            pl.BlockSpec((CH, wblk), lambda i: (cidx(i), C_HI // wblk)),
            pl.BlockSpec((1, wblk), lambda i: (0, 0)),
            pl.BlockSpec((1, 1, 1, HGRN_HEADS, HGRN_DK, HGRN_DK), s0_map),
            pl.BlockSpec((CH, CH), lambda i: (0, 0)),
        ],
        out_specs=[
            pl.BlockSpec((CH, wblk), lambda i: (cidx(i), 0)),
            pl.BlockSpec((1, 1, HGRN_HEADS, HGRN_DK, HGRN_DK), sfin_map),
        ],
        out_shape=[
            jax.ShapeDtypeStruct((TT, HGRN_W), F32),
            jax.ShapeDtypeStruct((BATCH, 1, HGRN_HEADS, HGRN_DK, HGRN_DK), F32),
        ],
        scratch_shapes=[pltpu.VMEM((HGRN_HEADS, HGRN_DK, HGRN_DK), F32)],
        compiler_params=_cp(("arbitrary",), 32),
        name="hgrn_bwd" if rev else "hgrn_fwd",
    )(z, z, z, lb_l[d:d + 1], state_hgrn, lev)
    return o, sfin


def _rope_tables(dim, copies):
    nfreq = dim // 4
    inv_freq = ROPE_THETA ** (-np.arange(nfreq, dtype=np.float64) / nfreq)
    t = np.arange(DEC_SEQ)
    pos_row = (t // GRID_W).astype(np.float64)
    pos_col = (t % GRID_W).astype(np.float64)
    lane = np.arange(dim)
    use_col = (lane // (dim // 2)) == 1
    fidx = lane % nfreq
    first = (lane % (dim // 2)) < nfreq
    pos = np.where(use_col[None, :], pos_col[:, None], pos_row[:, None])
    ang = pos * inv_freq[fidx][None, :]
    cos = np.cos(ang)
    sin = np.where(first[None, :], -np.sin(ang), np.sin(ang))
    cos = np.tile(cos, (1, copies)).astype(np.float32)
    sin = np.tile(sin, (1, copies)).astype(np.float32)
    return jnp.asarray(cos), jnp.asarray(sin)


def _rope(x, cos, sin, dim):
    nfreq = dim // 4
    lane = lax.broadcasted_iota(jnp.int32, x.shape, 1)
    first = (lane % (dim // 2)) < nfreq
    partner = jnp.where(first, pltpu.roll(x, LANES - nfreq, 1), pltpu.roll(x, nfreq, 1))
    return x * cos + partner * sin


def _diff_kernel(*refs, latent, lam_init, tq):
    if latent:
        (q_ref, k_ref, v_ref, nq_ref, nk_ref, sub_ref, lam_ref, cos_ref, sin_ref, ck_ref, cv_ref,
         out_ref, k_scr, v_scr) = refs
    else:
        (q_ref, k_ref, v_ref, nq_ref, nk_ref, sub_ref, lam_ref,
         out_ref, ckout_ref, cvout_ref, k_scr, v_scr) = refs
    n = q_ref.shape[0]
    seg64 = _seg_matrix(DIFF_DK)
    seg128 = _seg_matrix(LANES)
    qn = _rmsnorm_seg(q_ref[...], nq_ref[...], seg64, DIFF_DK)
    kn = _rmsnorm_seg(k_ref[...], nk_ref[...], seg64, DIFF_DK)
    v = v_ref[...]
    if latent:
        qn = _rope(qn, cos_ref[...], sin_ref[...], DIFF_DK)
        kn = _rope(kn, cos_ref[...], sin_ref[...], DIFF_DK)
        k_scr[n:, :] = ck_ref[0, 0].astype(BF16)
        v_scr[n:, :] = cv_ref[0, 0].astype(BF16)
    else:
        ckout_ref[...] = kn
        cvout_ref[...] = v
    k_scr[0:n, :] = kn.astype(BF16)
    v_scr[0:n, :] = v.astype(BF16)

    lp = lam_ref[...]
    lam = (jnp.exp(jnp.sum(lp[0:1] * lp[1:2], axis=-1, keepdims=True))
           - jnp.exp(jnp.sum(lp[2:3] * lp[3:4], axis=-1, keepdims=True)) + lam_init)

    lane = lax.broadcasted_iota(jnp.int32, (tq, LANES), 1)
    kall = k_scr[...]
    vall = v_scr[...]
    scale = DIFF_DK ** -0.5
    for blk in range(n // tq):
        qb = qn[blk * tq:(blk + 1) * tq, :] * scale
        q1 = jnp.where(lane < DIFF_DK, qb, 0.0).astype(BF16)
        q2 = jnp.where(lane >= DIFF_DK, qb, 0.0).astype(BF16)
        s1 = _dot_nt(q1, kall)
        s2 = _dot_nt(q2, kall)
        e1 = jnp.exp(s1 - jnp.max(s1, axis=-1, keepdims=True))
        e2 = jnp.exp(s2 - jnp.max(s2, axis=-1, keepdims=True))
        r1 = 1.0 / jnp.sum(e1, axis=-1, keepdims=True)
        r2 = lam / jnp.sum(e2, axis=-1, keepdims=True)
        p = e1 * r1 - e2 * r2
        o = _dot(p.astype(BF16), vall)
        on = _rmsnorm_seg(o, sub_ref[...], seg128, LANES) * (1.0 - lam_init)
        out_ref[blk * tq:(blk + 1) * tq, :] = on.astype(out_ref.dtype)


def _diff_attention(z, nq, nk, sub, lam_p, cache_k, cache_v, layer, lam_init, latent):
    hw = 2 * DIFF_DK
    if latent:
        n, nseq, row0, skv = DEC_SEQ, DEC_BATCH, TP // DEC_SEQ, DEC_SEQ + PAST_LEN
    else:
        n, nseq, row0, skv = SEQ, BATCH, 0, SEQ
    tq = 256
    zspec = lambda c0: pl.BlockSpec((n, hw), lambda s, h: (row0 + s, c0 // hw + h))
    vec = pl.BlockSpec((1, hw), lambda s, h: (0, 0))
    in_specs = [zspec(C_DQ), zspec(C_DK), zspec(C_DV), vec, vec, vec,
                pl.BlockSpec((4, DIFF_DK), lambda s, h: (0, 0))]
    args = [z, z, z, nq, nk, sub, lam_p]
    out_blk = pl.BlockSpec((n, hw), lambda s, h: (s, h))
    out_specs = [out_blk]
    out_shape = [jax.ShapeDtypeStruct((nseq * n, DIFF_HEADS * hw), BF16)]
    if latent:
        cos, sin = _rope_tables(DIFF_DK, 2)
        tab = pl.BlockSpec((DEC_SEQ, hw), lambda s, h: (0, 0))
        cspec = pl.BlockSpec((1, 1, PAST_LEN, hw), lambda s, h: (s, layer, 0, h))
        in_specs += [tab, tab, cspec, cspec]
        args += [cos, sin, cache_k, cache_v]
    else:
        out_specs += [out_blk, out_blk]
        out_shape += [jax.ShapeDtypeStruct((TP, DIFF_HEADS * hw), F32)] * 2
    return pl.pallas_call(
        functools.partial(_diff_kernel, latent=latent, lam_init=lam_init, tq=tq),
        grid=(nseq, DIFF_HEADS),
        in_specs=in_specs,
        out_specs=out_specs,
        out_shape=out_shape,
        scratch_shapes=[pltpu.VMEM((skv, hw), BF16), pltpu.VMEM((skv, hw), BF16)],
        compiler_params=_cp(("parallel", "parallel"), 48),
        name="diff_latent" if latent else "diff_context",
    )(*args)


def _gqa_kernel(*refs, latent, tq):
    if latent:
        (q_ref, k_ref, v_ref, nq_ref, nk_ref, cos_ref, sin_ref, ck_ref, cv_ref,
         out_ref, k_scr, v_scr) = refs
    else:
        (q_ref, k_ref, v_ref, nq_ref, nk_ref,
         out_ref, ckout_ref, cvout_ref, k_scr, v_scr) = refs
    n = q_ref.shape[0]
    seg128 = _seg_matrix(LANES)
    kn = _rmsnorm_seg(k_ref[...], nk_ref[...], seg128, GQA_DH)
    v = v_ref[...]
    if latent:
        kn = _rope(kn, cos_ref[...], sin_ref[...], GQA_DH)
        k_scr[n:, :] = ck_ref[0, 0].astype(BF16)
        v_scr[n:, :] = cv_ref[0, 0].astype(BF16)
    else:
        ckout_ref[...] = kn
        cvout_ref[...] = v
    k_scr[0:n, :] = kn.astype(BF16)
    v_scr[0:n, :] = v.astype(BF16)
    kall = k_scr[...]
    vall = v_scr[...]
    scale = GQA_DH ** -0.5
    for r in range(GQA_REP):
        sl = slice(r * GQA_DH, (r + 1) * GQA_DH)
        qn = _rmsnorm_seg(q_ref[:, sl], nq_ref[...], seg128, GQA_DH)
        if latent:
            qn = _rope(qn, cos_ref[...], sin_ref[...], GQA_DH)
        qb16 = qn.astype(BF16)
        for blk in range(n // tq):
            s = _dot_nt(qb16[blk * tq:(blk + 1) * tq, :], kall) * scale
            e = jnp.exp(s - jnp.max(s, axis=-1, keepdims=True))
            p = e * (1.0 / jnp.sum(e, axis=-1, keepdims=True))
            out_ref[blk * tq:(blk + 1) * tq, sl] = _dot(p.astype(BF16), vall).astype(out_ref.dtype)


def _gqa_attention(z, nq, nk, cache_k, cache_v, layer, latent):
    qw = GQA_REP * GQA_DH
    if latent:
        n, nseq, row0, skv = DEC_SEQ, DEC_BATCH, TP // DEC_SEQ, DEC_SEQ + PAST_LEN
    else:
        n, nseq, row0, skv = SEQ, BATCH, 0, SEQ
    tq = 256
    vec = pl.BlockSpec((1, GQA_DH), lambda s, g: (0, 0))
    in_specs = [
        pl.BlockSpec((n, qw), lambda s, g: (row0 + s, C_GQ // qw + g)),
        pl.BlockSpec((n, GQA_DH), lambda s, g: (row0 + s, C_GK // GQA_DH + g)),
        pl.BlockSpec((n, GQA_DH), lambda s, g: (row0 + s, C_GV // GQA_DH + g)),
        vec, vec]
    args = [z, z, z, nq, nk]
    out_specs = [pl.BlockSpec((n, qw), lambda s, g: (s, g))]
    out_shape = [jax.ShapeDtypeStruct((nseq * n, GQA_HEADS * GQA_DH), BF16)]
    if latent:
        cos, sin = _rope_tables(GQA_DH, 1)
        tab = pl.BlockSpec((DEC_SEQ, GQA_DH), lambda s, g: (0, 0))
        cspec = pl.BlockSpec((1, 1, PAST_LEN, GQA_DH), lambda s, g: (s, layer, 0, g))
        in_specs += [tab, tab, cspec, cspec]
        args += [cos, sin, cache_k, cache_v]
    else:
        cblk = pl.BlockSpec((n, GQA_DH), lambda s, g: (s, g))
        out_specs += [cblk, cblk]
        out_shape += [jax.ShapeDtypeStruct((TP, GQA_KV_HEADS * GQA_DH), F32)] * 2
    return pl.pallas_call(
        functools.partial(_gqa_kernel, latent=latent, tq=tq),
        grid=(nseq, GQA_KV_HEADS),
        in_specs=in_specs,
        out_specs=out_specs,
        out_shape=out_shape,
        scratch_shapes=[pltpu.VMEM((skv, GQA_DH), BF16), pltpu.VMEM((skv, GQA_DH), BF16)],
        compiler_params=_cp(("parallel", "parallel"), 48),
        name="gqa_latent" if latent else "gqa_context",
    )(*args)


def _merge_kernel(of_ref, ob_ref, hg_ref, hn_ref, dp_ref, ds_ref, gp_ref, gs_ref,
                  g0_ref, g1_ref, g2_ref, wb_ref, m_ref, br_scr, *, n_prompt_blocks):
    i = pl.program_id(0)

    @pl.when(pl.program_id(1) == 0)
    def _():
        seg128 = _seg_matrix(LANES)
        for h in range(HGRN_HEADS):
            sl = slice(h * HGRN_DK, (h + 1) * HGRN_DK)
            o = of_ref[:, sl] + ob_ref[:, sl]
            g = hg_ref[:, sl]
            y = _rmsnorm_seg(o, hn_ref[...], seg128, HGRN_DK) * (g * jax.nn.sigmoid(g))
            br_scr[0, :, sl] = y.astype(BF16)

        @pl.when(i < n_prompt_blocks)
        def _():
            br_scr[1] = dp_ref[...]
            br_scr[2] = gp_ref[...]

        @pl.when(i >= n_prompt_blocks)
        def _():
            br_scr[1] = ds_ref[...]
            br_scr[2] = gs_ref[...]

    acc = jax.nn.sigmoid(g0_ref[...]) * _dot(br_scr[0], wb_ref[0])
    acc += jax.nn.sigmoid(g1_ref[...]) * _dot(br_scr[1], wb_ref[1])
    acc += jax.nn.sigmoid(g2_ref[...]) * _dot(br_scr[2], wb_ref[2])
    m_ref[...] = acc.astype(m_ref.dtype)


def _merge(o_f, o_b, z, hgrn_norm_row, diff_p, diff_s, gqa_p, gqa_s, w_branch):
    tm, tn = 512, 512
    npb = TP // tm
    nsb = TS // tm
    row = lambda i, j: (i, 0)
    prow = lambda i, j: (jnp.minimum(i, npb - 1), 0)
    srow = lambda i, j: (jnp.clip(i - npb, 0, nsb - 1), 0)
    gate = lambda c: pl.BlockSpec((tm, tn), lambda i, j: (i, (C_GATE + c * D_MODEL) // tn + j))
    return pl.pallas_call(
        functools.partial(_merge_kernel, n_prompt_blocks=npb),
        grid=(TT // tm, D_MODEL // tn),
        in_specs=[
            pl.BlockSpec((tm, HGRN_W), row),
            pl.BlockSpec((tm, HGRN_W), row),
            pl.BlockSpec((tm, HGRN_W), lambda i, j: (i, C_HG // HGRN_W)),
            pl.BlockSpec((1, HGRN_DK), lambda i, j: (0, 0)),
            pl.BlockSpec((tm, BRANCH_W), prow),
            pl.BlockSpec((tm, BRANCH_W), srow),
            pl.BlockSpec((tm, BRANCH_W), prow),
            pl.BlockSpec((tm, BRANCH_W), srow),
            gate(0), gate(1), gate(2),
            pl.BlockSpec((N_BRANCH, BRANCH_W, tn), lambda i, j: (0, 0, j)),
        ],
        out_specs=pl.BlockSpec((tm, tn), lambda i, j: (i, j)),
        out_shape=jax.ShapeDtypeStruct((TT, D_MODEL), BF16),
        scratch_shapes=[pltpu.VMEM((N_BRANCH, tm, BRANCH_W), BF16)],
        compiler_params=_cp(("parallel", "arbitrary"), 48),
        name="branch_merge",
    )(o_f, o_b, z, hgrn_norm_row, diff_p, diff_s, gqa_p, gqa_s, z, z, z, w_branch)


def _out_proj_kernel(m_ref, w_ref, x_ref, mod_ref, y_ref, *, gate_idx):
    y_ref[...] = x_ref[...] + mod_ref[0, gate_idx:gate_idx + 1, :] * _dot(m_ref[...], w_ref[...])


def _out_proj(m, w_out, x, mod3, gate_idx):
    tm, tn = 512, 1024
    return pl.pallas_call(
        functools.partial(_out_proj_kernel, gate_idx=gate_idx),
        grid=(TT // tm, D_MODEL // tn),
        in_specs=[
            pl.BlockSpec((tm, D_MODEL), lambda i, j: (i, 0)),
            pl.BlockSpec((D_MODEL, tn), lambda i, j: (0, j)),
            pl.BlockSpec((tm, tn), lambda i, j: (i, j)),
            pl.BlockSpec((1, 6, tn), lambda i, j: (_mod_row(i, tm), 0, j)),
        ],
        out_specs=pl.BlockSpec((tm, tn), lambda i, j: (i, j)),
        out_shape=jax.ShapeDtypeStruct((TT, D_MODEL), F32),
        compiler_params=_cp(("parallel", "parallel"), 48),
        name="out_proj",
    )(m, w_out, x, mod3)


_STAIR = [PEER_TOPK // (r + 1) for r in range(8)]


def _top_values(s, k):
    vals = []
    for _ in range(k):
        m = jnp.max(s, axis=0, keepdims=True)
        vals.append(m)
        s = jnp.where(s >= m, NEG_INF, s)
    return vals


def _route_kernel(q_ref, keys_ref, r1_ref, r2_ref, *, tm):
    kb = keys_ref[...].astype(BF16)
    sub8 = lax.broadcasted_iota(jnp.int32, (8, LANES), 0)
    for g in range(tm // LANES):
        rows = slice(g * LANES, (g + 1) * LANES)
        cols = slice(g * LANES, (g + 1) * LANES)
        s1 = _dot_nt(kb[0], q_ref[rows, 0:N_KEYS])
        s2 = _dot_nt(kb[1], q_ref[rows, N_KEYS:2 * N_KEYS])
        v1 = _top_values(s1, PEER_TOPK)
        v2 = _top_values(s2, PEER_TOPK)
        sv2 = jnp.concatenate(v2, axis=0)
        sv1_hi = jnp.concatenate(v1[8:], axis=0)
        cand = [v1[0] + sv2, v1[1] + sv2[0:8]]
        for r in range(2, 8):
            cand.append(jnp.where(sub8 < _STAIR[r], v1[r] + sv2[0:8], NEG_INF))
        cand.append(sv1_hi + v2[0])
        cand = jnp.concatenate(cand, axis=0)
        tau = _top_values(cand, PEER_TOPK)[-1]
        cmax = v1[0] + v2[0]
        zsum = jnp.sum(jnp.where(cand >= tau, jnp.exp(cand - cmax), 0.0), axis=0, keepdims=True)
        rz = 1.0 / zsum
        th = jnp.full((N_KEYS, LANES), POS_INF, F32)
        for r in range(PEER_TOPK):
            th_r = jnp.min(jnp.where(v1[r] + sv2 >= tau, sv2, POS_INF), axis=0, keepdims=True)
            th = jnp.where(s1 == v1[r], th_r, th)
        r1_ref[0, 0, :, cols] = th
        r1_ref[0, 1, :, cols] = jnp.exp(s1 - v1[0]) * rz
        r2_ref[0, 0, :, cols] = s2
        r2_ref[0, 1, :, cols] = jnp.exp(s2 - v2[0])


def _peer_route(qp, keys_l):
    tm = 512
    blk = pl.BlockSpec((1, 2, N_KEYS, tm), lambda i, h: (h, 0, 0, i))
    shp = jax.ShapeDtypeStruct((PEER_HEADS, 2, N_KEYS, TT), F32)
    return pl.pallas_call(
        functools.partial(_route_kernel, tm=tm),
        grid=(TT // tm, PEER_HEADS),
        in_specs=[
            pl.BlockSpec((tm, 2 * N_KEYS), lambda i, h: (i, h)),
            pl.BlockSpec((2, N_KEYS, N_KEYS), lambda i, h: (0, 0, 0)),
        ],
        out_specs=[blk, blk],
        out_shape=[shp, shp],
        compiler_params=_cp(("parallel", "parallel"), 32),
        name="peer_route",
    )(qp, keys_l)


def _expert_kernel(h_ref, u_ref, v_ref, r1_ref, r2_ref, x_ref, mod_ref, y_ref,
                   at_scr, g_scr, acc_scr, *, tm, ce, gate_idx):
    j = pl.program_id(1)

    @pl.when(j == 0)
    def _():
        acc_scr[...] = jnp.zeros_like(acc_scr)

    at_scr[...] = _dot_nt(u_ref[...], h_ref[...])

    n_i1 = ce // N_KEYS
    for g in range(tm // LANES):
        cols = slice(g * LANES, (g + 1) * LANES)

        for il in range(n_i1):
            w = jnp.zeros((N_KEYS, LANES), F32)
            for h in range(PEER_HEADS):
                th = r1_ref[h, 0, il:il + 1, cols]
                e1 = r1_ref[h, 1, il:il + 1, cols]
                w += jnp.where(r2_ref[h, 0, :, cols] >= th, r2_ref[h, 1, :, cols], 0.0) * e1
            rows = slice(il * N_KEYS, (il + 1) * N_KEYS)
            a = at_scr[rows, cols]
            act = 0.5 * a * (1.0 + lax.erf(a * (2.0 ** -0.5)))
            g_scr[rows, cols] = (act * w).astype(BF16)

    acc_scr[...] += _dot_tn(g_scr[...], v_ref[...])

    @pl.when(j == pl.num_programs(1) - 1)
    def _():
        y_ref[...] = x_ref[...] + mod_ref[0, gate_idx:gate_idx + 1, :] * acc_scr[...]


def _peer_experts(h2, u, v, r1, r2, x, mod3, gate_idx):
    tm, ce = 512, 1024
    n_i1 = ce // N_KEYS
    return pl.pallas_call(
        functools.partial(_expert_kernel, tm=tm, ce=ce, gate_idx=gate_idx),
        grid=(TT // tm, N_EXPERTS // ce),
        in_specs=[
            pl.BlockSpec((tm, D_MODEL), lambda i, j: (i, 0)),
            pl.BlockSpec((ce, D_MODEL), lambda i, j: (j, 0)),
            pl.BlockSpec((ce, D_MODEL), lambda i, j: (j, 0)),
            pl.BlockSpec((PEER_HEADS, 2, n_i1, tm), lambda i, j: (0, 0, j, i)),
            pl.BlockSpec((PEER_HEADS, 2, N_KEYS, tm), lambda i, j: (0, 0, 0, i)),
            pl.BlockSpec((tm, D_MODEL), lambda i, j: (i, 0)),
            pl.BlockSpec((1, 6, D_MODEL), lambda i, j: (_mod_row(i, tm), 0, 0)),
        ],
        out_specs=pl.BlockSpec((tm, D_MODEL), lambda i, j: (i, 0)),
        out_shape=jax.ShapeDtypeStruct((TT, D_MODEL), F32),
        scratch_shapes=[
            pltpu.VMEM((ce, tm), F32),
            pltpu.VMEM((ce, tm), BF16),
            pltpu.VMEM((tm, D_MODEL), F32),
        ],
        compiler_params=_cp(("parallel", "arbitrary"), 56),
        name="peer_experts",
    )(h2, u, v, r1, r2, x, mod3)


def kernel(x_prompt, x_sample, c, cache_diff_k, cache_diff_v, cache_gqa_k, cache_gqa_v, state_hgrn, c_ctx,
           mod_w, mod_b, norm_mix, norm_ffn, w_in, hgrn_lb, hgrn_norm, diff_qk_norm, diff_lambda, diff_subln,
           gqa_qk_norm, w_branch, w_out, peer_wq, peer_keys, peer_u, peer_v):
    x = jnp.concatenate([x_prompt.reshape(TP, D_MODEL), x_sample.reshape(TS, D_MODEL)], axis=0)
    cond8 = jnp.concatenate([c_ctx[None, :], c, jnp.zeros((8 - 1 - DEC_BATCH, D_MODEL), F32)], axis=0)
    mods = _modulation(cond8, mod_w, mod_b)

    lb_all = jnp.cumsum(jax.nn.softmax(hgrn_lb.astype(F32), axis=1), axis=1)
    lb_all = lb_all - lb_all[:, :1]

    w_in_b = w_in.astype(BF16)
    w_branch_b = w_branch.astype(BF16)
    w_out_b = w_out.astype(BF16)
    peer_wq_b = peer_wq.astype(BF16)
    peer_u_b = peer_u.astype(BF16)
    peer_v_b = peer_v.astype(BF16)

    cdk = cache_diff_k.reshape(DEC_BATCH, DEPTH, PAST_LEN, DIFF_HEADS * 2 * DIFF_DK)
    cdv = cache_diff_v.reshape(DEC_BATCH, DEPTH, PAST_LEN, DIFF_HEADS * 2 * DIFF_DK)
    cgk = cache_gqa_k.reshape(DEC_BATCH, DEPTH, PAST_LEN, GQA_KV_HEADS * GQA_DH)
    cgv = cache_gqa_v.reshape(DEC_BATCH, DEPTH, PAST_LEN, GQA_KV_HEADS * GQA_DH)

    nk, nv, ngk, ngv, nst = [], [], [], [], []
    for l in range(DEPTH):
        lam_init = 0.8 - 0.6 * math.exp(-0.3 * l)
        mod3 = mods[l].reshape(8, 6, D_MODEL)

        z = _norm_matmul(x, mod3, norm_mix[l][None, :], w_in_b[l], sh_idx=0, sc_idx=1,
                         tm=1024, tn=512, out_dtype=F32, emit_h=False, name="in_proj")[0]

        o_f, s_f = _hgrn(z, lb_all[:, l], state_hgrn, l, rev=False)
        o_b, s_b = _hgrn(z, lb_all[:, l], state_hgrn, l, rev=True)

        nq_d = jnp.tile(diff_qk_norm[l, 0], 2)[None, :]
        nk_d = jnp.tile(diff_qk_norm[l, 1], 2)[None, :]
        sub = diff_subln[l][None, :]
        diff_p, ck_new, cv_new = _diff_attention(z, nq_d, nk_d, sub, diff_lambda[l], None, None, l,
                                                 lam_init, latent=False)
        diff_s = _diff_attention(z, nq_d, nk_d, sub, diff_lambda[l], cdk, cdv, l, lam_init, latent=True)[0]

        nq_g = gqa_qk_norm[l, 0][None, :]
        nk_g = gqa_qk_norm[l, 1][None, :]
        gqa_p, gk_new, gv_new = _gqa_attention(z, nq_g, nk_g, None, None, l, latent=False)
        gqa_s = _gqa_attention(z, nq_g, nk_g, cgk, cgv, l, latent=True)[0]

        m = _merge(o_f, o_b, z, hgrn_norm[l][None, :], diff_p, diff_s, gqa_p, gqa_s, w_branch_b[l])
        x = _out_proj(m, w_out_b[l], x, mod3, gate_idx=2)

        qp, h2 = _norm_matmul(x, mod3, norm_ffn[l][None, :], peer_wq_b[l], sh_idx=3, sc_idx=4,
                              tm=1024, tn=512, out_dtype=BF16, emit_h=True, name="peer_query")
        r1, r2 = _peer_route(qp, peer_keys[l])
        x = _peer_experts(h2, peer_u_b[l], peer_v_b[l], r1, r2, x, mod3, gate_idx=5)

        nk.append(ck_new.reshape(BATCH, SEQ, DIFF_HEADS, 2 * DIFF_DK))
        nv.append(cv_new.reshape(BATCH, SEQ, DIFF_HEADS, 2 * DIFF_DK))
        ngk.append(gk_new.reshape(BATCH, SEQ, GQA_KV_HEADS, GQA_DH))
        ngv.append(gv_new.reshape(BATCH, SEQ, GQA_KV_HEADS, GQA_DH))
        nst.append(jnp.concatenate([s_f, s_b], axis=1))

    y_prompt = x[:TP].reshape(BATCH, SEQ, D_MODEL)
    y_sample = x[TP:].reshape(DEC_BATCH, DEC_SEQ, D_MODEL)
    return (y_prompt, y_sample, jnp.stack(nk, axis=1), jnp.stack(nv, axis=1), jnp.stack(ngk, axis=1),
            jnp.stack(ngv, axis=1), jnp.stack(nst, axis=1))
```

```python
import functools
import math

import numpy as np
import jax
import jax.numpy as jnp
from jax import lax
from jax.experimental import pallas as pl
from jax.experimental.pallas import tpu as pltpu

F32 = jnp.float32
BF16 = jnp.bfloat16

D_MODEL = 2048
BATCH = 16
SEQ = 256
DEPTH = 2
DEC_BATCH = 2
DEC_SEQ = 1024
PAST_LEN = 256
GRID_W = 64
ROPE_THETA = 10000.0
NORM_EPS = 1e-6

HGRN_HEADS = 8
HGRN_DK = 128
HGRN_W = 1024
DIFF_HEADS = 8
DIFF_DK = 64
GQA_HEADS = 8
GQA_KV_HEADS = 2
GQA_DH = 128
GQA_REP = 4
N_BRANCH = 3
BRANCH_W = 1024
PEER_HEADS = 8
N_KEYS = 128
N_EXPERTS = N_KEYS * N_KEYS
PEER_TOPK = 16
IN_WIDTH = 15872

TP = BATCH * SEQ
TS = DEC_BATCH * DEC_SEQ
TT = TP + TS

C_HQ, C_HF0, C_HF1, C_HI, C_HG = 0, 1024, 2048, 3072, 4096
C_DQ, C_DK, C_DV = 5120, 6144, 7168
C_GQ, C_GK, C_GV = 8192, 9216, 9472
C_GATE = 9728

LANES = 128
CH = 128
NCH = TT // CH
NPC = TP // CH
CH_PER_PROMPT = SEQ // CH
CH_PER_SAMPLE = DEC_SEQ // CH
N_LEVELS = 7
NEG_INF = float("-inf")
POS_INF = float("inf")
MIN_NORMAL = 1.1754944e-38

_NT = (((1,), (1,)), ((), ()))
_TN = (((0,), (0,)), ((), ()))


def _cp(sem, vmem_mb):
    return pltpu.CompilerParams(dimension_semantics=sem, vmem_limit_bytes=vmem_mb * 1024 * 1024)


def _dot(a, b):
    return jnp.dot(a, b, preferred_element_type=F32)


def _dot_nt(a, b):
    return lax.dot_general(a, b, _NT, preferred_element_type=F32)


def _dot_tn(a, b):
    return lax.dot_general(a, b, _TN, preferred_element_type=F32)


def _split_bf16(x):
    hi = x.astype(BF16)
    lo = (x - hi.astype(F32)).astype(BF16)
    return hi, lo


def _seg_matrix(seg):
    r = lax.broadcasted_iota(jnp.int32, (LANES, LANES), 0) // seg
    c = lax.broadcasted_iota(jnp.int32, (LANES, LANES), 1) // seg
    return (r == c).astype(BF16)


def _rmsnorm_seg(x, gain_row, seg_mat, seg):
    hi, lo = _split_bf16(x * x)
    ss = _dot(hi, seg_mat) + _dot(lo, seg_mat)
    return x * lax.rsqrt(ss * (1.0 / seg) + NORM_EPS) * gain_row


def _mod_row(i, tm):
    n_p = TP // tm
    per = DEC_SEQ // tm
    return jnp.where(i < n_p, 0, 1 + (i - n_p) // per)


def _mod_kernel(cond_ref, w_ref, b_ref, out_ref):
    a = cond_ref[...]
    a = a * jax.nn.sigmoid(a)
    out_ref[0] = _dot(a.astype(BF16), w_ref[0].astype(BF16)) + b_ref[0]


def _modulation(cond8, mod_w, mod_b):
    tn = 1024
    n6 = 6 * D_MODEL
    return pl.pallas_call(
        _mod_kernel,
        grid=(DEPTH, n6 // tn),
        in_specs=[
            pl.BlockSpec((8, D_MODEL), lambda l, j: (0, 0)),
            pl.BlockSpec((1, D_MODEL, tn), lambda l, j: (l, 0, j)),
            pl.BlockSpec((1, 1, tn), lambda l, j: (l, 0, j)),
        ],
        out_specs=pl.BlockSpec((1, 8, tn), lambda l, j: (l, 0, j)),
        out_shape=jax.ShapeDtypeStruct((DEPTH, 8, n6), F32),
        compiler_params=_cp(("parallel", "parallel"), 40),
        name="modulation",
    )(cond8, mod_w, mod_b.reshape(DEPTH, 1, n6))


def _norm_mm_kernel(x_ref, mod_ref, gain_ref, w_ref, *rest, sh_idx, sc_idx, emit_h):
    if emit_h:
        z_ref, h_ref, h_scr = rest
    else:
        z_ref, h_scr = rest

    @pl.when(pl.program_id(1) == 0)
    def _():
        x = x_ref[...]
        ms = jnp.mean(x * x, axis=-1, keepdims=True)
        y = x * lax.rsqrt(ms + NORM_EPS) * gain_ref[...]
        h = y * (1.0 + mod_ref[0, 0, sc_idx:sc_idx + 1, :]) + mod_ref[0, 0, sh_idx:sh_idx + 1, :]
        hb = h.astype(BF16)
        h_scr[...] = hb
        if emit_h:
            h_ref[...] = hb

    z_ref[...] = _dot(h_scr[...], w_ref[0].astype(BF16)).astype(z_ref.dtype)


def _norm_matmul(x, mods4, layer, gain, w, *, sh_idx, sc_idx, tm, tn, out_dtype, emit_h, name):
    n = w.shape[2]
    out_shape = [jax.ShapeDtypeStruct((TT, n), out_dtype)]
    out_specs = [pl.BlockSpec((tm, tn), lambda i, j: (i, j))]
    if emit_h:
        out_shape.append(jax.ShapeDtypeStruct((TT, D_MODEL), BF16))
        out_specs.append(pl.BlockSpec((tm, D_MODEL), lambda i, j: (i, 0)))
    return pl.pallas_call(
        functools.partial(_norm_mm_kernel, sh_idx=sh_idx, sc_idx=sc_idx, emit_h=emit_h),
        grid=(TT // tm, n // tn),
        in_specs=[
            pl.BlockSpec((tm, D_MODEL), lambda i, j: (i, 0)),
            pl.BlockSpec((1, 1, 6, D_MODEL), lambda i, j: (layer, _mod_row(i, tm), 0, 0)),
            pl.BlockSpec((1, D_MODEL), lambda i, j: (0, 0)),
            pl.BlockSpec((1, D_MODEL, tn), lambda i, j: (layer, 0, j)),
        ],
        out_specs=out_specs,
        out_shape=out_shape,
        scratch_shapes=[pltpu.VMEM((tm, D_MODEL), BF16)],
        compiler_params=_cp(("parallel", "arbitrary"), 48),
        name=name,
    )(x, mods4, gain, w)


def _level_table(rev):
    t = np.arange(CH)[:, None]
    s = np.arange(CH)[None, :]
    x = t ^ s
    lev = np.full((CH, CH), -1, np.int32)
    nz = x > 0
    lev[nz] = np.floor(np.log2(x[nz])).astype(np.int32)
    valid = (t < s) if rev else (t > s)
    lev = np.where(valid, lev, -1)
    lev[np.arange(CH), np.arange(CH)] = N_LEVELS
    return lev.astype(np.int32)


def _bmid(b, b3, m, rev):
    off = m if rev else m - 1
    if m >= 8:
        pieces = []
        for j in range(CH // (2 * m)):
            idx = j * 2 * m + off
            pieces.append(jnp.broadcast_to(b[idx:idx + 1, :], (2 * m, LANES)))
        return pieces[0] if len(pieces) == 1 else jnp.concatenate(pieces, axis=0)
    sub = lax.broadcasted_iota(jnp.int32, (CH // 8, 8, LANES), 1)
    out = None
    for j in range(8 // (2 * m)):
        idx = j * 2 * m + off
        piece = jnp.broadcast_to(b3[:, idx:idx + 1, :], (CH // 8, 8, LANES))
        out = piece if out is None else jnp.where(sub >= j * 2 * m, piece, out)
    return out.reshape(CH, LANES)


def _hgrn_kernel(q_ref, f_ref, v_ref, lb_ref, s0_ref, lev_ref, o_ref, sfin_ref, st_scr, *, rev):
    i = pl.program_id(0)
    c = (NCH - 1 - i) if rev else i
    is_prompt = c < NPC
    cp_first = (c % CH_PER_PROMPT) == 0
    cp_last = (c % CH_PER_PROMPT) == CH_PER_PROMPT - 1
    cs = jnp.maximum(c - NPC, 0)
    cs_first = (cs % CH_PER_SAMPLE) == 0
    cs_last = (cs % CH_PER_SAMPLE) == CH_PER_SAMPLE - 1
    if rev:
        start_p, end_p, start_s = cp_last, cp_first, cs_last
    else:
        start_p, end_p, start_s = cp_first, cp_last, cs_first

    @pl.when(jnp.logical_and(is_prompt, start_p))
    def _():
        st_scr[...] = jnp.zeros_like(st_scr)

    @pl.when(jnp.logical_and(jnp.logical_not(is_prompt), start_s))
    def _():
        for h in range(HGRN_HEADS):
            st_scr[h] = s0_ref[0, 0, 0, h].T

    row = lax.broadcasted_iota(jnp.int32, (CH, CH), 0)
    col = lax.broadcasted_iota(jnp.int32, (CH, CH), 1)
    tri = ((col >= row) if rev else (col <= row)).astype(BF16)
    lev = lev_ref[...]
    q_bit = 0 if rev else 1

    for h in range(HGRN_HEADS):
        sl = slice(h * HGRN_DK, (h + 1) * HGRN_DK)
        q = q_ref[:, sl]
        zf = f_ref[:, sl]
        v = v_ref[:, sl]
        lb = lb_ref[:, sl]
        f = lb + (1.0 - lb) * jax.nn.sigmoid(zf)
        kk = (1.0 - lb) * jax.nn.sigmoid(-zf)
        logf = jnp.log(jnp.maximum(f, MIN_NORMAL))
        hi, lo = _split_bf16(logf)
        b = _dot(tri, hi) + _dot(tri, lo)
        b3 = b.reshape(CH // 8, 8, LANES)
        vb = v.astype(BF16)

        a = jnp.where(lev == N_LEVELS, _dot_nt(q.astype(BF16), kk.astype(BF16)), 0.0)
        for lm in range(N_LEVELS):
            e = jnp.exp(-jnp.abs(b - _bmid(b, b3, 1 << lm, rev)))
            q_side = ((row >> lm) & 1) == q_bit
            x = (jnp.where(q_side, q, kk) * e).astype(BF16)
            a = jnp.where(lev == lm, _dot_nt(x, x), a)

        b_end = b[0:1, :] if rev else b[CH - 1:CH, :]
        qd = (q * jnp.exp(b)).astype(BF16)
        kd = (kk * jnp.exp(b_end - b)).astype(BF16)
        st = st_scr[h]
        o_ref[:, sl] = _dot(a.astype(BF16), vb) + _dot_nt(qd, st.astype(BF16))
        st_scr[h] = st * jnp.exp(b_end) + _dot_tn(vb, kd)

    @pl.when(jnp.logical_and(is_prompt, end_p))
    def _():
        for h in range(HGRN_HEADS):
            sfin_ref[0, 0, h] = st_scr[h].T


def _hgrn(z, lb_l, state_hgrn, layer, rev):
    d = 1 if rev else 0
    lev = jnp.asarray(_level_table(rev))

    def cidx(i):
        return (NCH - 1 - i) if rev else i

    def s0_map(i):
        b = jnp.clip((cidx(i) - NPC) // CH_PER_SAMPLE, 0, DEC_BATCH - 1)
        return (b, layer, d, 0, 0, 0)

    def sfin_map(i):
        return (jnp.minimum(cidx(i) // CH_PER_PROMPT, BATCH - 1), 0, 0, 0, 0)

    wblk = HGRN_W
    o, sfin = pl.pallas_call(
        functools.partial(_hgrn_kernel, rev=rev),
        grid=(NCH,),
        in_specs=[
            pl.BlockSpec((CH, wblk), lambda i: (cidx(i), C_HQ // wblk)),
            pl.BlockSpec((CH, wblk), lambda i: (cidx(i), (C_HF1 if rev else C_HF0) // wblk)),
            pl.BlockSpec((CH, wblk), lambda i: (cidx(i), C_HI // wblk)),
            pl.BlockSpec((1, wblk), lambda i: (0, 0)),
            pl.BlockSpec((1, 1, 1, HGRN_HEADS, HGRN_DK, HGRN_DK), s0_map),
            pl.BlockSpec((CH, CH), lambda i: (0, 0)),
        ],
        out_specs=[
            pl.BlockSpec((CH, wblk), lambda i: (cidx(i), 0)),
            pl.BlockSpec((1, 1, HGRN_HEADS, HGRN_DK, HGRN_DK), sfin_map),
        ],
        out_shape=[
            jax.ShapeDtypeStruct((TT, HGRN_W), F32),
            jax.ShapeDtypeStruct((BATCH, 1, HGRN_HEADS, HGRN_DK, HGRN_DK), F32),
        ],
        scratch_shapes=[pltpu.VMEM((HGRN_HEADS, HGRN_DK, HGRN_DK), F32)],
        compiler_params=_cp(("arbitrary",), 32),
        name="hgrn_bwd" if rev else "hgrn_fwd",
    )(z, z, z, lb_l[d:d + 1], state_hgrn, lev)
    return o, sfin


def _rope_tables(dim, copies):
    nfreq = dim // 4
    inv_freq = ROPE_THETA ** (-np.arange(nfreq, dtype=np.float64) / nfreq)
    t = np.arange(DEC_SEQ)
    pos_row = (t // GRID_W).astype(np.float64)
    pos_col = (t % GRID_W).astype(np.float64)
    lane = np.arange(dim)
    use_col = (lane // (dim // 2)) == 1
    fidx = lane % nfreq
    first = (lane % (dim // 2)) < nfreq
    pos = np.where(use_col[None, :], pos_col[:, None], pos_row[:, None])
    ang = pos * inv_freq[fidx][None, :]
    cos = np.cos(ang)
    sin = np.where(first[None, :], -np.sin(ang), np.sin(ang))
    cos = np.tile(cos, (1, copies)).astype(np.float32)
    sin = np.tile(sin, (1, copies)).astype(np.float32)
    return jnp.asarray(cos), jnp.asarray(sin)


def _rope(x, cos, sin, dim):
    nfreq = dim // 4
    lane = lax.broadcasted_iota(jnp.int32, x.shape, 1)
    first = (lane % (dim // 2)) < nfreq
    partner = jnp.where(first, pltpu.roll(x, LANES - nfreq, 1), pltpu.roll(x, nfreq, 1))
    return x * cos + partner * sin


def _diff_kernel(*refs, latent, lam_init, tq):
    if latent:
        (q_ref, k_ref, v_ref, nq_ref, nk_ref, sub_ref, lam_ref, cos_ref, sin_ref, ck_ref, cv_ref,
         out_ref, k_scr, v_scr) = refs
    else:
        (q_ref, k_ref, v_ref, nq_ref, nk_ref, sub_ref, lam_ref,
         out_ref, ckout_ref, cvout_ref, k_scr, v_scr) = refs
    n = q_ref.shape[0]
    seg64 = _seg_matrix(DIFF_DK)
    seg128 = _seg_matrix(LANES)
    qn = _rmsnorm_seg(q_ref[...], nq_ref[...], seg64, DIFF_DK)
    kn = _rmsnorm_seg(k_ref[...], nk_ref[...], seg64, DIFF_DK)
    v = v_ref[...]
    if latent:
        qn = _rope(qn, cos_ref[...], sin_ref[...], DIFF_DK)
        kn = _rope(kn, cos_ref[...], sin_ref[...], DIFF_DK)
        k_scr[n:, :] = ck_ref[0, 0].astype(BF16)
        v_scr[n:, :] = cv_ref[0, 0].astype(BF16)
    else:
        ckout_ref[...] = kn
        cvout_ref[...] = v
    k_scr[0:n, :] = kn.astype(BF16)
    v_scr[0:n, :] = v.astype(BF16)

    lp = lam_ref[...]
    lam = (jnp.exp(jnp.sum(lp[0:1] * lp[1:2], axis=-1, keepdims=True))
           - jnp.exp(jnp.sum(lp[2:3] * lp[3:4], axis=-1, keepdims=True)) + lam_init)

    lane = lax.broadcasted_iota(jnp.int32, (tq, LANES), 1)
    kall = k_scr[...]
    vall = v_scr[...]
    scale = DIFF_DK ** -0.5
    for blk in range(n // tq):
        qb = qn[blk * tq:(blk + 1) * tq, :] * scale
        q1 = jnp.where(lane < DIFF_DK, qb, 0.0).astype(BF16)
        q2 = jnp.where(lane >= DIFF_DK, qb, 0.0).astype(BF16)
        s1 = _dot_nt(q1, kall)
        s2 = _dot_nt(q2, kall)
        e1 = jnp.exp(s1 - jnp.max(s1, axis=-1, keepdims=True))
        e2 = jnp.exp(s2 - jnp.max(s2, axis=-1, keepdims=True))
        r1 = 1.0 / jnp.sum(e1, axis=-1, keepdims=True)
        r2 = lam / jnp.sum(e2, axis=-1, keepdims=True)
        p = e1 * r1 - e2 * r2
        o = _dot(p.astype(BF16), vall)
        on = _rmsnorm_seg(o, sub_ref[...], seg128, LANES) * (1.0 - lam_init)
        out_ref[blk * tq:(blk + 1) * tq, :] = on.astype(out_ref.dtype)


def _diff_attention(z, nq, nk, sub, lam_p, cache_k, cache_v, layer, lam_init, latent):
    hw = 2 * DIFF_DK
    if latent:
        n, nseq, row0, skv = DEC_SEQ, DEC_BATCH, TP // DEC_SEQ, DEC_SEQ + PAST_LEN
    else:
        n, nseq, row0, skv = SEQ, BATCH, 0, SEQ
    tq = 256
    zspec = lambda c0: pl.BlockSpec((n, hw), lambda s, h: (row0 + s, c0 // hw + h))
    vec = pl.BlockSpec((1, hw), lambda s, h: (0, 0))
    in_specs = [zspec(C_DQ), zspec(C_DK), zspec(C_DV), vec, vec, vec,
                pl.BlockSpec((4, DIFF_DK), lambda s, h: (0, 0))]
    args = [z, z, z, nq, nk, sub, lam_p]
    out_blk = pl.BlockSpec((n, hw), lambda s, h: (s, h))
    out_specs = [out_blk]
    out_shape = [jax.ShapeDtypeStruct((nseq * n, DIFF_HEADS * hw), BF16)]
    if latent:
        cos, sin = _rope_tables(DIFF_DK, 2)
        tab = pl.BlockSpec((DEC_SEQ, hw), lambda s, h: (0, 0))
        cspec = pl.BlockSpec((1, 1, PAST_LEN, hw), lambda s, h: (s, layer, 0, h))
        in_specs += [tab, tab, cspec, cspec]
        args += [cos, sin, cache_k, cache_v]
    else:
        out_specs += [out_blk, out_blk]
        out_shape += [jax.ShapeDtypeStruct((TP, DIFF_HEADS * hw), F32)] * 2
    return pl.pallas_call(
        functools.partial(_diff_kernel, latent=latent, lam_init=lam_init, tq=tq),
        grid=(nseq, DIFF_HEADS),
        in_specs=in_specs,
        out_specs=out_specs,
        out_shape=out_shape,
        scratch_shapes=[pltpu.VMEM((skv, hw), BF16), pltpu.VMEM((skv, hw), BF16)],
        compiler_params=_cp(("parallel", "parallel"), 48),
        name="diff_latent" if latent else "diff_context",
    )(*args)


def _gqa_kernel(*refs, latent, tq):
    if latent:
        (q_ref, k_ref, v_ref, nq_ref, nk_ref, cos_ref, sin_ref, ck_ref, cv_ref,
         out_ref, k_scr, v_scr) = refs
    else:
        (q_ref, k_ref, v_ref, nq_ref, nk_ref,
         out_ref, ckout_ref, cvout_ref, k_scr, v_scr) = refs
    n = q_ref.shape[0]
    seg128 = _seg_matrix(LANES)
    kn = _rmsnorm_seg(k_ref[...], nk_ref[...], seg128, GQA_DH)
    v = v_ref[...]
    if latent:
        kn = _rope(kn, cos_ref[...], sin_ref[...], GQA_DH)
        k_scr[n:, :] = ck_ref[0, 0].astype(BF16)
        v_scr[n:, :] = cv_ref[0, 0].astype(BF16)
    else:
        ckout_ref[...] = kn
        cvout_ref[...] = v
    k_scr[0:n, :] = kn.astype(BF16)
    v_scr[0:n, :] = v.astype(BF16)
    kall = k_scr[...]
    vall = v_scr[...]
    scale = GQA_DH ** -0.5
    for r in range(GQA_REP):
        sl = slice(r * GQA_DH, (r + 1) * GQA_DH)
        qn = _rmsnorm_seg(q_ref[:, sl], nq_ref[...], seg128, GQA_DH)
        if latent:
            qn = _rope(qn, cos_ref[...], sin_ref[...], GQA_DH)
        qb16 = qn.astype(BF16)
        for blk in range(n // tq):
            s = _dot_nt(qb16[blk * tq:(blk + 1) * tq, :], kall) * scale
            e = jnp.exp(s - jnp.max(s, axis=-1, keepdims=True))
            p = e * (1.0 / jnp.sum(e, axis=-1, keepdims=True))
            out_ref[blk * tq:(blk + 1) * tq, sl] = _dot(p.astype(BF16), vall).astype(out_ref.dtype)


def _gqa_attention(z, nq, nk, cache_k, cache_v, layer, latent):
    qw = GQA_REP * GQA_DH
    if latent:
        n, nseq, row0, skv = DEC_SEQ, DEC_BATCH, TP // DEC_SEQ, DEC_SEQ + PAST_LEN
    else:
        n, nseq, row0, skv = SEQ, BATCH, 0, SEQ
    tq = 256
    vec = pl.BlockSpec((1, GQA_DH), lambda s, g: (0, 0))
    in_specs = [
        pl.BlockSpec((n, qw), lambda s, g: (row0 + s, C_GQ // qw + g)),
        pl.BlockSpec((n, GQA_DH), lambda s, g: (row0 + s, C_GK // GQA_DH + g)),
        pl.BlockSpec((n, GQA_DH), lambda s, g: (row0 + s, C_GV // GQA_DH + g)),
        vec, vec]
    args = [z, z, z, nq, nk]
    out_specs = [pl.BlockSpec((n, qw), lambda s, g: (s, g))]
    out_shape = [jax.ShapeDtypeStruct((nseq * n, GQA_HEADS * GQA_DH), BF16)]
    if latent:
        cos, sin = _rope_tables(GQA_DH, 1)
        tab = pl.BlockSpec((DEC_SEQ, GQA_DH), lambda s, g: (0, 0))
        cspec = pl.BlockSpec((1, 1, PAST_LEN, GQA_DH), lambda s, g: (s, layer, 0, g))
        in_specs += [tab, tab, cspec, cspec]
        args += [cos, sin, cache_k, cache_v]
    else:
        cblk = pl.BlockSpec((n, GQA_DH), lambda s, g: (s, g))
        out_specs += [cblk, cblk]
        out_shape += [jax.ShapeDtypeStruct((TP, GQA_KV_HEADS * GQA_DH), F32)] * 2
    return pl.pallas_call(
        functools.partial(_gqa_kernel, latent=latent, tq=tq),
        grid=(nseq, GQA_KV_HEADS),
        in_specs=in_specs,
        out_specs=out_specs,
        out_shape=out_shape,
        scratch_shapes=[pltpu.VMEM((skv, GQA_DH), BF16), pltpu.VMEM((skv, GQA_DH), BF16)],
        compiler_params=_cp(("parallel", "parallel"), 48),
        name="gqa_latent" if latent else "gqa_context",
    )(*args)


def _merge_kernel(of_ref, ob_ref, hg_ref, hn_ref, dp_ref, ds_ref, gp_ref, gs_ref,
                  g0_ref, g1_ref, g2_ref, wb_ref, m_ref, br_scr, *, n_prompt_blocks):
    i = pl.program_id(0)

    @pl.when(pl.program_id(1) == 0)
    def _():
        seg128 = _seg_matrix(LANES)
        for h in range(HGRN_HEADS):
            sl = slice(h * HGRN_DK, (h + 1) * HGRN_DK)
            o = of_ref[:, sl] + ob_ref[:, sl]
            g = hg_ref[:, sl]
            y = _rmsnorm_seg(o, hn_ref[...], seg128, HGRN_DK) * (g * jax.nn.sigmoid(g))
            br_scr[0, :, sl] = y.astype(BF16)

        @pl.when(i < n_prompt_blocks)
        def _():
            br_scr[1] = dp_ref[...]
            br_scr[2] = gp_ref[...]

        @pl.when(i >= n_prompt_blocks)
        def _():
            br_scr[1] = ds_ref[...]
            br_scr[2] = gs_ref[...]

    acc = jax.nn.sigmoid(g0_ref[...]) * _dot(br_scr[0], wb_ref[0, 0].astype(BF16))
    acc += jax.nn.sigmoid(g1_ref[...]) * _dot(br_scr[1], wb_ref[0, 1].astype(BF16))
    acc += jax.nn.sigmoid(g2_ref[...]) * _dot(br_scr[2], wb_ref[0, 2].astype(BF16))
    m_ref[...] = acc.astype(m_ref.dtype)


def _merge(o_f, o_b, z, hgrn_norm_row, diff_p, diff_s, gqa_p, gqa_s, w_branch, layer):
    tm, tn = 512, 512
    npb = TP // tm
    nsb = TS // tm
    row = lambda i, j: (i, 0)
    prow = lambda i, j: (jnp.minimum(i, npb - 1), 0)
    srow = lambda i, j: (jnp.clip(i - npb, 0, nsb - 1), 0)
    gate = lambda c: pl.BlockSpec((tm, tn), lambda i, j: (i, (C_GATE + c * D_MODEL) // tn + j))
    return pl.pallas_call(
        functools.partial(_merge_kernel, n_prompt_blocks=npb),
        grid=(TT // tm, D_MODEL // tn),
        in_specs=[
            pl.BlockSpec((tm, HGRN_W), row),
            pl.BlockSpec((tm, HGRN_W), row),
            pl.BlockSpec((tm, HGRN_W), lambda i, j: (i, C_HG // HGRN_W)),
            pl.BlockSpec((1, HGRN_DK), lambda i, j: (0, 0)),
            pl.BlockSpec((tm, BRANCH_W), prow),
            pl.BlockSpec((tm, BRANCH_W), srow),
            pl.BlockSpec((tm, BRANCH_W), prow),
            pl.BlockSpec((tm, BRANCH_W), srow),
            gate(0), gate(1), gate(2),
            pl.BlockSpec((1, N_BRANCH, BRANCH_W, tn), lambda i, j: (layer, 0, 0, j)),
        ],
        out_specs=pl.BlockSpec((tm, tn), lambda i, j: (i, j)),
        out_shape=jax.ShapeDtypeStruct((TT, D_MODEL), BF16),
        scratch_shapes=[pltpu.VMEM((N_BRANCH, tm, BRANCH_W), BF16)],
        compiler_params=_cp(("parallel", "arbitrary"), 48),
        name="branch_merge",
    )(o_f, o_b, z, hgrn_norm_row, diff_p, diff_s, gqa_p, gqa_s, z, z, z, w_branch)


def _out_proj_kernel(m_ref, w_ref, x_ref, mod_ref, y_ref, *, gate_idx):
    y_ref[...] = (x_ref[...] + mod_ref[0, 0, gate_idx:gate_idx + 1, :]
                  * _dot(m_ref[...], w_ref[0].astype(BF16)))


def _out_proj(m, w_out, x, mods4, layer, gate_idx):
    tm, tn = 512, 1024
    return pl.pallas_call(
        functools.partial(_out_proj_kernel, gate_idx=gate_idx),
        grid=(TT // tm, D_MODEL // tn),
        in_specs=[
            pl.BlockSpec((tm, D_MODEL), lambda i, j: (i, 0)),
            pl.BlockSpec((1, D_MODEL, tn), lambda i, j: (layer, 0, j)),
            pl.BlockSpec((tm, tn), lambda i, j: (i, j)),
            pl.BlockSpec((1, 1, 6, tn), lambda i, j: (layer, _mod_row(i, tm), 0, j)),
        ],
        out_specs=pl.BlockSpec((tm, tn), lambda i, j: (i, j)),
        out_shape=jax.ShapeDtypeStruct((TT, D_MODEL), F32),
        compiler_params=_cp(("parallel", "parallel"), 48),
        name="out_proj",
    )(m, w_out, x, mods4)


_STAIR = [PEER_TOPK // (r + 1) for r in range(8)]


def _extract_top(s, k, want_rank):
    vals = []
    rank = jnp.full(s.shape, float(k), F32) if want_rank else None
    for r in range(k):
        m = jnp.max(s, axis=0, keepdims=True)
        hit = s >= m
        vals.append(m)
        if want_rank:
            rank = jnp.where(hit, float(r), rank)
        s = jnp.where(hit, NEG_INF, s)
    return vals, rank


def _route_kernel(q_ref, keys_ref, r1_ref, r2_ref, *, tm):
    kb = keys_ref[0].astype(BF16)
    sub8 = lax.broadcasted_iota(jnp.int32, (8, LANES), 0)
    for g in range(tm // LANES):
        rows = slice(g * LANES, (g + 1) * LANES)
        cols = slice(g * LANES, (g + 1) * LANES)
        s1 = _dot_nt(kb[0], q_ref[rows, 0:N_KEYS])
        s2 = _dot_nt(kb[1], q_ref[rows, N_KEYS:2 * N_KEYS])
        v1, _ = _extract_top(s1, PEER_TOPK, False)
        v2, rank2 = _extract_top(s2, PEER_TOPK, True)
        sv2 = jnp.concatenate(v2, axis=0)
        sv1_hi = jnp.concatenate(v1[8:], axis=0)
        cand = [v1[0] + sv2, v1[1] + sv2[0:8]]
        for r in range(2, 8):
            cand.append(jnp.where(sub8 < _STAIR[r], v1[r] + sv2[0:8], NEG_INF))
        cand.append(sv1_hi + v2[0])
        cand = jnp.concatenate(cand, axis=0)
        tau = _extract_top(cand, PEER_TOPK, False)[0][-1]
        cmax = v1[0] + v2[0]
        zsum = jnp.sum(jnp.where(cand >= tau, jnp.exp(cand - cmax), 0.0), axis=0, keepdims=True)
        rz = 1.0 / zsum
        cnt = jnp.zeros((N_KEYS, LANES), F32)
        for r in range(PEER_TOPK):
            cnt_r = jnp.sum(jnp.where(v1[r] + sv2 >= tau, 1.0, 0.0), axis=0, keepdims=True)
            cnt = jnp.where(s1 == v1[r], cnt_r, cnt)
        r1_ref[0, 0, :, cols] = cnt
        r1_ref[0, 1, :, cols] = jnp.exp(s1 - v1[0]) * rz
        r2_ref[0, 0, :, cols] = rank2
        r2_ref[0, 1, :, cols] = jnp.exp(s2 - v2[0])


def _peer_route(qp, peer_keys, layer):
    tm = 512
    blk = pl.BlockSpec((1, 2, N_KEYS, tm), lambda i, h: (h, 0, 0, i))
    return pl.pallas_call(
        functools.partial(_route_kernel, tm=tm),
        grid=(TT // tm, PEER_HEADS),
        in_specs=[
            pl.BlockSpec((tm, 2 * N_KEYS), lambda i, h: (i, h)),
            pl.BlockSpec((1, 2, N_KEYS, N_KEYS), lambda i, h: (layer, 0, 0, 0)),
        ],
        out_specs=[blk, blk],
        out_shape=[jax.ShapeDtypeStruct((PEER_HEADS, 2, N_KEYS, TT), F32)] * 2,
        compiler_params=_cp(("parallel", "parallel"), 32),
        name="peer_route",
    )(qp, peer_keys)


def _expert_kernel(h_ref, u_ref, v_ref, r1_ref, r2_ref, x_ref, mod_ref, y_ref,
                   r2_scr, at_scr, g_scr, acc_scr, *, tm, ce, n_parts, gate_idx):
    j = pl.program_id(1)

    @pl.when(j == 0)
    def _():
        acc_scr[...] = jnp.zeros_like(acc_scr)
        r2_scr[...] = r2_ref[...].astype(BF16)

    hb = h_ref[...]
    pe = ce // n_parts
    packed = (N_KEYS // 16, 16, LANES)
    contrib = None
    for p in range(n_parts):
        at_scr[p] = _dot_nt(u_ref[0, p * pe:(p + 1) * pe, :], hb)
        for ib in range(pe // N_KEYS):
            il = p * (pe // N_KEYS) + ib
            rows = slice(ib * N_KEYS, (ib + 1) * N_KEYS)
            for g in range(tm // LANES):
                cols = slice(g * LANES, (g + 1) * LANES)
                w = jnp.zeros(packed, BF16)
                for h in range(PEER_HEADS):
                    rk = r2_scr[h, 0, :, cols].reshape(packed)
                    e2 = r2_scr[h, 1, :, cols].reshape(packed)
                    cnt = jnp.broadcast_to(r1_ref[h, 0, il:il + 1, cols], (16, LANES)).astype(BF16)[None]
                    e1 = jnp.broadcast_to(r1_ref[h, 1, il:il + 1, cols], (16, LANES)).astype(BF16)[None]
                    w = w + jnp.where(rk < cnt, e2, jnp.zeros_like(e2)) * e1
                a = at_scr[p, rows, cols]
                act = 0.5 * a * (1.0 + lax.erf(a * (2.0 ** -0.5)))
                gt = act.astype(BF16).reshape(packed) * w
                g_scr[p, rows, cols] = gt.reshape(N_KEYS, LANES)
        d = _dot_tn(g_scr[p], v_ref[0, p * pe:(p + 1) * pe, :])
        contrib = d if contrib is None else contrib + d
    acc_scr[...] += contrib

    @pl.when(j == pl.num_programs(1) - 1)
    def _():
        y_ref[...] = x_ref[...] + mod_ref[0, 0, gate_idx:gate_idx + 1, :] * acc_scr[...]


def _peer_experts(h2, u, v, r1, r2, x, mods4, layer, gate_idx):
    tm, ce, n_parts = 512, 1024, 2
    n_i1 = ce // N_KEYS
    return pl.pallas_call(
        functools.partial(_expert_kernel, tm=tm, ce=ce, n_parts=n_parts, gate_idx=gate_idx),
        grid=(TT // tm, N_EXPERTS // ce),
        in_specs=[
            pl.BlockSpec((tm, D_MODEL), lambda i, j: (i, 0)),
            pl.BlockSpec((1, ce, D_MODEL), lambda i, j: (layer, j, 0)),
            pl.BlockSpec((1, ce, D_MODEL), lambda i, j: (layer, j, 0)),
            pl.BlockSpec((PEER_HEADS, 2, n_i1, tm), lambda i, j: (0, 0, j, i)),
            pl.BlockSpec((PEER_HEADS, 2, N_KEYS, tm), lambda i, j: (0, 0, 0, i)),
            pl.BlockSpec((tm, D_MODEL), lambda i, j: (i, 0)),
            pl.BlockSpec((1, 1, 6, D_MODEL), lambda i, j: (layer, _mod_row(i, tm), 0, 0)),
        ],
        out_specs=pl.BlockSpec((tm, D_MODEL), lambda i, j: (i, 0)),
        out_shape=jax.ShapeDtypeStruct((TT, D_MODEL), F32),
        scratch_shapes=[
            pltpu.VMEM((PEER_HEADS, 2, N_KEYS, tm), BF16),
            pltpu.VMEM((n_parts, ce // n_parts, tm), F32),
            pltpu.VMEM((n_parts, ce // n_parts, tm), BF16),
            pltpu.VMEM((tm, D_MODEL), F32),
        ],
        compiler_params=_cp(("parallel", "arbitrary"), 56),
        name="peer_experts",
    )(h2, u, v, r1, r2, x, mods4)


def kernel(x_prompt, x_sample, c, cache_diff_k, cache_diff_v, cache_gqa_k, cache_gqa_v, state_hgrn, c_ctx,
           mod_w, mod_b, norm_mix, norm_ffn, w_in, hgrn_lb, hgrn_norm, diff_qk_norm, diff_lambda, diff_subln,
           gqa_qk_norm, w_branch, w_out, peer_wq, peer_keys, peer_u, peer_v):
    x = jnp.concatenate([x_prompt.reshape(TP, D_MODEL), x_sample.reshape(TS, D_MODEL)], axis=0)
    cond8 = jnp.concatenate([c_ctx[None, :], c, jnp.zeros((8 - 1 - DEC_BATCH, D_MODEL), F32)], axis=0)
    mods4 = _modulation(cond8, mod_w, mod_b).reshape(DEPTH, 8, 6, D_MODEL)

    lb_all = jnp.cumsum(jax.nn.softmax(hgrn_lb.astype(F32), axis=1), axis=1)
    lb_all = lb_all - lb_all[:, :1]

    w_in_b, w_branch_b, w_out_b, peer_wq_b = w_in, w_branch, w_out, peer_wq
    peer_u_b = peer_u.astype(BF16)
    peer_v_b = peer_v.astype(BF16)

    cdk = cache_diff_k.reshape(DEC_BATCH, DEPTH, PAST_LEN, DIFF_HEADS * 2 * DIFF_DK)
    cdv = cache_diff_v.reshape(DEC_BATCH, DEPTH, PAST_LEN, DIFF_HEADS * 2 * DIFF_DK)
    cgk = cache_gqa_k.reshape(DEC_BATCH, DEPTH, PAST_LEN, GQA_KV_HEADS * GQA_DH)
    cgv = cache_gqa_v.reshape(DEC_BATCH, DEPTH, PAST_LEN, GQA_KV_HEADS * GQA_DH)

    nk, nv, ngk, ngv, nst = [], [], [], [], []
    for l in range(DEPTH):
        lam_init = 0.8 - 0.6 * math.exp(-0.3 * l)
        z = _norm_matmul(x, mods4, l, norm_mix[l][None, :], w_in_b, sh_idx=0, sc_idx=1,
                         tm=1024, tn=512, out_dtype=F32, emit_h=False, name="in_proj")[0]

        o_f, s_f = _hgrn(z, lb_all[:, l], state_hgrn, l, rev=False)
        o_b, s_b = _hgrn(z, lb_all[:, l], state_hgrn, l, rev=True)

        nq_d = jnp.tile(diff_qk_norm[l, 0], 2)[None, :]
        nk_d = jnp.tile(diff_qk_norm[l, 1], 2)[None, :]
        sub = diff_subln[l][None, :]
        diff_p, ck_new, cv_new = _diff_attention(z, nq_d, nk_d, sub, diff_lambda[l], None, None, l,
                                                 lam_init, latent=False)
        diff_s = _diff_attention(z, nq_d, nk_d, sub, diff_lambda[l], cdk, cdv, l, lam_init, latent=True)[0]

        nq_g = gqa_qk_norm[l, 0][None, :]
        nk_g = gqa_qk_norm[l, 1][None, :]
        gqa_p, gk_new, gv_new = _gqa_attention(z, nq_g, nk_g, None, None, l, latent=False)
        gqa_s = _gqa_attention(z, nq_g, nk_g, cgk, cgv, l, latent=True)[0]

        m = _merge(o_f, o_b, z, hgrn_norm[l][None, :], diff_p, diff_s, gqa_p, gqa_s, w_branch_b, l)
        x = _out_proj(m, w_out_b, x, mods4, l, gate_idx=2)

        qp, h2 = _norm_matmul(x, mods4, l, norm_ffn[l][None, :], peer_wq_b, sh_idx=3, sc_idx=4,
                              tm=1024, tn=512, out_dtype=BF16, emit_h=True, name="peer_query")
        r1, r2 = _peer_route(qp, peer_keys, l)
        x = _peer_experts(h2, peer_u_b, peer_v_b, r1, r2, x, mods4, l, gate_idx=5)

        nk.append(ck_new.reshape(BATCH, SEQ, DIFF_HEADS, 2 * DIFF_DK))
        nv.append(cv_new.reshape(BATCH, SEQ, DIFF_HEADS, 2 * DIFF_DK))
        ngk.append(gk_new.reshape(BATCH, SEQ, GQA_KV_HEADS, GQA_DH))
        ngv.append(gv_new.reshape(BATCH, SEQ, GQA_KV_HEADS, GQA_DH))
        nst.append(jnp.concatenate([s_f, s_b], axis=1))

    y_prompt = x[:TP].reshape(BATCH, SEQ, D_MODEL)
    y_sample = x[TP:].reshape(DEC_BATCH, DEC_SEQ, D_MODEL)
    return (y_prompt, y_sample, jnp.stack(nk, axis=1), jnp.stack(nv, axis=1), jnp.stack(ngk, axis=1),
            jnp.stack(ngv, axis=1), jnp.stack(nst, axis=1))
```

```python
import functools
import math

import numpy as np
import jax
import jax.numpy as jnp
from jax import lax
from jax.experimental import pallas as pl
from jax.experimental.pallas import tpu as pltpu

F32 = jnp.float32
BF16 = jnp.bfloat16

D_MODEL = 2048
BATCH = 16
SEQ = 256
DEPTH = 2
DEC_BATCH = 2
DEC_SEQ = 1024
PAST_LEN = 256
GRID_W = 64
ROPE_THETA = 10000.0
NORM_EPS = 1e-6

HGRN_HEADS = 8
HGRN_DK = 128
HGRN_W = 1024
DIFF_HEADS = 8
DIFF_DK = 64
GQA_HEADS = 8
GQA_KV_HEADS = 2
GQA_DH = 128
GQA_REP = 4
N_BRANCH = 3
BRANCH_W = 1024
PEER_HEADS = 8
N_KEYS = 128
N_EXPERTS = N_KEYS * N_KEYS
PEER_TOPK = 16
IN_WIDTH = 15872

TP = BATCH * SEQ
TS = DEC_BATCH * DEC_SEQ
TT = TP + TS

C_HQ, C_HF0, C_HF1, C_HI, C_HG = 0, 1024, 2048, 3072, 4096
C_DQ, C_DK, C_DV = 5120, 6144, 7168
C_GQ, C_GK, C_GV = 8192, 9216, 9472
C_GATE = 9728

LANES = 128
CH = 128
NCH = TT // CH
NPC = TP // CH
CH_PER_PROMPT = SEQ // CH
CH_PER_SAMPLE = DEC_SEQ // CH
N_LEVELS = 7
NEG_INF = float("-inf")
POS_INF = float("inf")
MIN_NORMAL = 1.1754944e-38

_NT = (((1,), (1,)), ((), ()))
_TN = (((0,), (0,)), ((), ()))


def _cp(sem, vmem_mb):
    return pltpu.CompilerParams(dimension_semantics=sem, vmem_limit_bytes=vmem_mb * 1024 * 1024)


def _dot(a, b):
    return jnp.dot(a, b, preferred_element_type=F32)


def _dot_nt(a, b):
    return lax.dot_general(a, b, _NT, preferred_element_type=F32)


def _dot_tn(a, b):
    return lax.dot_general(a, b, _TN, preferred_element_type=F32)


def _split_bf16(x):
    hi = x.astype(BF16)
    lo = (x - hi.astype(F32)).astype(BF16)
    return hi, lo


def _seg_matrix(seg):
    r = lax.broadcasted_iota(jnp.int32, (LANES, LANES), 0) // seg
    c = lax.broadcasted_iota(jnp.int32, (LANES, LANES), 1) // seg
    return (r == c).astype(BF16)


def _rmsnorm_seg(x, gain_row, seg_mat, seg):
    hi, lo = _split_bf16(x * x)
    ss = _dot(hi, seg_mat) + _dot(lo, seg_mat)
    return x * lax.rsqrt(ss * (1.0 / seg) + NORM_EPS) * gain_row


def _mod_row(i, tm):
    n_p = TP // tm
    per = DEC_SEQ // tm
    return jnp.where(i < n_p, 0, 1 + (i - n_p) // per)


def _mod_kernel(cond_ref, w_ref, b_ref, out_ref):
    a = cond_ref[...]
    a = a * jax.nn.sigmoid(a)
    out_ref[0] = _dot(a.astype(BF16), w_ref[0].astype(BF16)) + b_ref[0]


def _modulation(cond8, mod_w, mod_b):
    tn = 1024
    n6 = 6 * D_MODEL
    return pl.pallas_call(
        _mod_kernel,
        grid=(DEPTH, n6 // tn),
        in_specs=[
            pl.BlockSpec((8, D_MODEL), lambda l, j: (0, 0)),
            pl.BlockSpec((1, D_MODEL, tn), lambda l, j: (l, 0, j)),
            pl.BlockSpec((1, 1, tn), lambda l, j: (l, 0, j)),
        ],
        out_specs=pl.BlockSpec((1, 8, tn), lambda l, j: (l, 0, j)),
        out_shape=jax.ShapeDtypeStruct((DEPTH, 8, n6), F32),
        compiler_params=_cp(("parallel", "parallel"), 40),
        name="modulation",
    )(cond8, mod_w, mod_b.reshape(DEPTH, 1, n6))


def _norm_mm_kernel(x_ref, mod_ref, gain_ref, w_ref, *rest, sh_idx, sc_idx, emit_h):
    if emit_h:
        z_ref, h_ref, h_scr = rest
    else:
        z_ref, h_scr = rest

    @pl.when(pl.program_id(1) == 0)
    def _():
        x = x_ref[...]
        ms = jnp.mean(x * x, axis=-1, keepdims=True)
        y = x * lax.rsqrt(ms + NORM_EPS) * gain_ref[...]
        h = y * (1.0 + mod_ref[0, 0, sc_idx:sc_idx + 1, :]) + mod_ref[0, 0, sh_idx:sh_idx + 1, :]
        hb = h.astype(BF16)
        h_scr[...] = hb
        if emit_h:
            h_ref[...] = hb

    z_ref[...] = _dot(h_scr[...], w_ref[0].astype(BF16)).astype(z_ref.dtype)


def _norm_matmul(x, mods4, layer, gain, w, *, sh_idx, sc_idx, tm, tn, out_dtype, emit_h, name):
    n = w.shape[2]
    out_shape = [jax.ShapeDtypeStruct((TT, n), out_dtype)]
    out_specs = [pl.BlockSpec((tm, tn), lambda i, j: (i, j))]
    if emit_h:
        out_shape.append(jax.ShapeDtypeStruct((TT, D_MODEL), BF16))
        out_specs.append(pl.BlockSpec((tm, D_MODEL), lambda i, j: (i, 0)))
    return pl.pallas_call(
        functools.partial(_norm_mm_kernel, sh_idx=sh_idx, sc_idx=sc_idx, emit_h=emit_h),
        grid=(TT // tm, n // tn),
        in_specs=[
            pl.BlockSpec((tm, D_MODEL), lambda i, j: (i, 0)),
            pl.BlockSpec((1, 1, 6, D_MODEL), lambda i, j: (layer, _mod_row(i, tm), 0, 0)),
            pl.BlockSpec((1, D_MODEL), lambda i, j: (0, 0)),
            pl.BlockSpec((1, D_MODEL, tn), lambda i, j: (layer, 0, j)),
        ],
        out_specs=out_specs,
        out_shape=out_shape,
        scratch_shapes=[pltpu.VMEM((tm, D_MODEL), BF16)],
        compiler_params=_cp(("parallel", "arbitrary"), 48),
        name=name,
    )(x, mods4, gain, w)


def _level_table(rev):
    t = np.arange(CH)[:, None]
    s = np.arange(CH)[None, :]
    x = t ^ s
    lev = np.full((CH, CH), -1, np.int32)
    nz = x > 0
    lev[nz] = np.floor(np.log2(x[nz])).astype(np.int32)
    valid = (t < s) if rev else (t > s)
    lev = np.where(valid, lev, -1)
    lev[np.arange(CH), np.arange(CH)] = N_LEVELS
    return lev.astype(np.int32)


def _bmid(b, b3, m, rev):
    off = m if rev else m - 1
    if m >= 8:
        pieces = []
        for j in range(CH // (2 * m)):
            idx = j * 2 * m + off
            pieces.append(jnp.broadcast_to(b[idx:idx + 1, :], (2 * m, LANES)))
        return pieces[0] if len(pieces) == 1 else jnp.concatenate(pieces, axis=0)
    sub = lax.broadcasted_iota(jnp.int32, (CH // 8, 8, LANES), 1)
    out = None
    for j in range(8 // (2 * m)):
        idx = j * 2 * m + off
        piece = jnp.broadcast_to(b3[:, idx:idx + 1, :], (CH // 8, 8, LANES))
        out = piece if out is None else jnp.where(sub >= j * 2 * m, piece, out)
    return out.reshape(CH, LANES)


def _hgrn_kernel(q_ref, f_ref, v_ref, lb_ref, s0_ref, lev_ref, o_ref, sfin_ref, st_scr, *, rev):
    i = pl.program_id(0)
    c = (NCH - 1 - i) if rev else i
    is_prompt = c < NPC
    cp_first = (c % CH_PER_PROMPT) == 0
    cp_last = (c % CH_PER_PROMPT) == CH_PER_PROMPT - 1
    cs = jnp.maximum(c - NPC, 0)
    cs_first = (cs % CH_PER_SAMPLE) == 0
    cs_last = (cs % CH_PER_SAMPLE) == CH_PER_SAMPLE - 1
    if rev:
        start_p, end_p, start_s = cp_last, cp_first, cs_last
    else:
        start_p, end_p, start_s = cp_first, cp_last, cs_first

    @pl.when(jnp.logical_and(is_prompt, start_p))
    def _():
        st_scr[...] = jnp.zeros_like(st_scr)

    @pl.when(jnp.logical_and(jnp.logical_not(is_prompt), start_s))
    def _():
        for h in range(HGRN_HEADS):
            st_scr[h] = s0_ref[0, 0, 0, h].T

    row = lax.broadcasted_iota(jnp.int32, (CH, CH), 0)
    col = lax.broadcasted_iota(jnp.int32, (CH, CH), 1)
    tri = ((col >= row) if rev else (col <= row)).astype(BF16)
    lev = lev_ref[...]
    q_bit = 0 if rev else 1

    for h in range(HGRN_HEADS):
        sl = slice(h * HGRN_DK, (h + 1) * HGRN_DK)
        q = q_ref[:, sl]
        zf = f_ref[:, sl]
        v = v_ref[:, sl]
        lb = lb_ref[:, sl]
        f = lb + (1.0 - lb) * jax.nn.sigmoid(zf)
        kk = (1.0 - lb) * jax.nn.sigmoid(-zf)
        logf = jnp.log(jnp.maximum(f, MIN_NORMAL))
        hi, lo = _split_bf16(logf)
        b = _dot(tri, hi) + _dot(tri, lo)
        b3 = b.reshape(CH // 8, 8, LANES)
        vb = v.astype(BF16)

        a = jnp.where(lev == N_LEVELS, _dot_nt(q.astype(BF16), kk.astype(BF16)), 0.0)
        for lm in range(N_LEVELS):
            e = jnp.exp(-jnp.abs(b - _bmid(b, b3, 1 << lm, rev)))
            q_side = ((row >> lm) & 1) == q_bit
            x = (jnp.where(q_side, q, kk) * e).astype(BF16)
            a = jnp.where(lev == lm, _dot_nt(x, x), a)

        b_end = b[0:1, :] if rev else b[CH - 1:CH, :]
        qd = (q * jnp.exp(b)).astype(BF16)
        kd = (kk * jnp.exp(b_end - b)).astype(BF16)
        st = st_scr[h]
        o_ref[:, sl] = _dot(a.astype(BF16), vb) + _dot_nt(qd, st.astype(BF16))
        st_scr[h] = st * jnp.exp(b_end) + _dot_tn(vb, kd)

    @pl.when(jnp.logical_and(is_prompt, end_p))
    def _():
        for h in range(HGRN_HEADS):
            sfin_ref[0, 0, h] = st_scr[h].T


def _hgrn(z, lb_l, state_hgrn, layer, rev):
    d = 1 if rev else 0
    lev = jnp.asarray(_level_table(rev))

    def cidx(i):
        return (NCH - 1 - i) if rev else i

    def s0_map(i):
        b = jnp.clip((cidx(i) - NPC) // CH_PER_SAMPLE, 0, DEC_BATCH - 1)
        return (b, layer, d, 0, 0, 0)

    def sfin_map(i):
        return (jnp.minimum(cidx(i) // CH_PER_PROMPT, BATCH - 1), 0, 0, 0, 0)

    wblk = HGRN_W
    o, sfin = pl.pallas_call(
        functools.partial(_hgrn_kernel, rev=rev),
        grid=(NCH,),
        in_specs=[
            pl.BlockSpec((CH, wblk), lambda i: (cidx(i), C_HQ // wblk)),
            pl.BlockSpec((CH, wblk), lambda i: (cidx(i), (C_HF1 if rev else C_HF0) // wblk)),
            pl.BlockSpec((CH, wblk), lambda i: (cidx(i), C_HI // wblk)),
            pl.BlockSpec((1, wblk), lambda i: (0, 0)),
            pl.BlockSpec((1, 1, 1, HGRN_HEADS, HGRN_DK, HGRN_DK), s0_map),
            pl.BlockSpec((CH, CH), lambda i: (0, 0)),
        ],
        out_specs=[
            pl.BlockSpec((CH, wblk), lambda i: (cidx(i), 0)),
            pl.BlockSpec((1, 1, HGRN_HEADS, HGRN_DK, HGRN_DK), sfin_map),
        ],
        out_shape=[
            jax.ShapeDtypeStruct((TT, HGRN_W), F32),
            jax.ShapeDtypeStruct((BATCH, 1, HGRN_HEADS, HGRN_DK, HGRN_DK), F32),
        ],
        scratch_shapes=[pltpu.VMEM((HGRN_HEADS, HGRN_DK, HGRN_DK), F32)],
        compiler_params=_cp(("arbitrary",), 32),
        name="hgrn_bwd" if rev else "hgrn_fwd",
    )(z, z, z, lb_l[d:d + 1], state_hgrn, lev)
    return o, sfin


def _rope_tables(dim, copies):
    nfreq = dim // 4
    inv_freq = ROPE_THETA ** (-np.arange(nfreq, dtype=np.float64) / nfreq)
    t = np.arange(DEC_SEQ)
    pos_row = (t // GRID_W).astype(np.float64)
    pos_col = (t % GRID_W).astype(np.float64)
    lane = np.arange(dim)
    use_col = (lane // (dim // 2)) == 1
    fidx = lane % nfreq
    first = (lane % (dim // 2)) < nfreq
    pos = np.where(use_col[None, :], pos_col[:, None], pos_row[:, None])
    ang = pos * inv_freq[fidx][None, :]
    cos = np.cos(ang)
    sin = np.where(first[None, :], -np.sin(ang), np.sin(ang))
    cos = np.tile(cos, (1, copies)).astype(np.float32)
    sin = np.tile(sin, (1, copies)).astype(np.float32)
    return jnp.asarray(cos), jnp.asarray(sin)


def _rope(x, cos, sin, dim):
    nfreq = dim // 4
    lane = lax.broadcasted_iota(jnp.int32, x.shape, 1)
    first = (lane % (dim // 2)) < nfreq
    partner = jnp.where(first, pltpu.roll(x, LANES - nfreq, 1), pltpu.roll(x, nfreq, 1))
    return x * cos + partner * sin


def _diff_kernel(*refs, latent, lam_init, tq):
    if latent:
        (q_ref, k_ref, v_ref, nq_ref, nk_ref, sub_ref, lam_ref, cos_ref, sin_ref, ck_ref, cv_ref,
         out_ref, k_scr, v_scr) = refs
    else:
        (q_ref, k_ref, v_ref, nq_ref, nk_ref, sub_ref, lam_ref,
         out_ref, ckout_ref, cvout_ref, k_scr, v_scr) = refs
    n = q_ref.shape[0]
    seg64 = _seg_matrix(DIFF_DK)
    seg128 = _seg_matrix(LANES)
    qn = _rmsnorm_seg(q_ref[...], nq_ref[...], seg64, DIFF_DK)
    kn = _rmsnorm_seg(k_ref[...], nk_ref[...], seg64, DIFF_DK)
    v = v_ref[...]
    if latent:
        qn = _rope(qn, cos_ref[...], sin_ref[...], DIFF_DK)
        kn = _rope(kn, cos_ref[...], sin_ref[...], DIFF_DK)
        k_scr[n:, :] = ck_ref[0, 0].astype(BF16)
        v_scr[n:, :] = cv_ref[0, 0].astype(BF16)
    else:
        ckout_ref[...] = kn
        cvout_ref[...] = v
    k_scr[0:n, :] = kn.astype(BF16)
    v_scr[0:n, :] = v.astype(BF16)

    lp = lam_ref[...]
    lam = (jnp.exp(jnp.sum(lp[0:1] * lp[1:2], axis=-1, keepdims=True))
           - jnp.exp(jnp.sum(lp[2:3] * lp[3:4], axis=-1, keepdims=True)) + lam_init)

    lane = lax.broadcasted_iota(jnp.int32, (tq, LANES), 1)
    kall = k_scr[...]
    vall = v_scr[...]
    scale = DIFF_DK ** -0.5
    for blk in range(n // tq):
        qb = qn[blk * tq:(blk + 1) * tq, :] * scale
        q1 = jnp.where(lane < DIFF_DK, qb, 0.0).astype(BF16)
        q2 = jnp.where(lane >= DIFF_DK, qb, 0.0).astype(BF16)
        s1 = _dot_nt(q1, kall)
        s2 = _dot_nt(q2, kall)
        e1 = jnp.exp(s1 - jnp.max(s1, axis=-1, keepdims=True))
        e2 = jnp.exp(s2 - jnp.max(s2, axis=-1, keepdims=True))
        r1 = 1.0 / jnp.sum(e1, axis=-1, keepdims=True)
        r2 = lam / jnp.sum(e2, axis=-1, keepdims=True)
        p = e1 * r1 - e2 * r2
        o = _dot(p.astype(BF16), vall)
        on = _rmsnorm_seg(o, sub_ref[...], seg128, LANES) * (1.0 - lam_init)
        out_ref[blk * tq:(blk + 1) * tq, :] = on.astype(out_ref.dtype)


def _diff_attention(z, nq, nk, sub, lam_p, cache_k, cache_v, layer, lam_init, latent):
    hw = 2 * DIFF_DK
    if latent:
        n, nseq, row0, skv = DEC_SEQ, DEC_BATCH, TP // DEC_SEQ, DEC_SEQ + PAST_LEN
    else:
        n, nseq, row0, skv = SEQ, BATCH, 0, SEQ
    tq = 256
    zspec = lambda c0: pl.BlockSpec((n, hw), lambda s, h: (row0 + s, c0 // hw + h))
    vec = pl.BlockSpec((1, hw), lambda s, h: (0, 0))
    in_specs = [zspec(C_DQ), zspec(C_DK), zspec(C_DV), vec, vec, vec,
                pl.BlockSpec((4, DIFF_DK), lambda s, h: (0, 0))]
    args = [z, z, z, nq, nk, sub, lam_p]
    out_blk = pl.BlockSpec((n, hw), lambda s, h: (s, h))
    out_specs = [out_blk]
    out_shape = [jax.ShapeDtypeStruct((nseq * n, DIFF_HEADS * hw), BF16)]
    if latent:
        cos, sin = _rope_tables(DIFF_DK, 2)
        tab = pl.BlockSpec((DEC_SEQ, hw), lambda s, h: (0, 0))
        cspec = pl.BlockSpec((1, 1, PAST_LEN, hw), lambda s, h: (s, layer, 0, h))
        in_specs += [tab, tab, cspec, cspec]
        args += [cos, sin, cache_k, cache_v]
    else:
        out_specs += [out_blk, out_blk]
        out_shape += [jax.ShapeDtypeStruct((TP, DIFF_HEADS * hw), F32)] * 2
    return pl.pallas_call(
        functools.partial(_diff_kernel, latent=latent, lam_init=lam_init, tq=tq),
        grid=(nseq, DIFF_HEADS),
        in_specs=in_specs,
        out_specs=out_specs,
        out_shape=out_shape,
        scratch_shapes=[pltpu.VMEM((skv, hw), BF16), pltpu.VMEM((skv, hw), BF16)],
        compiler_params=_cp(("parallel", "parallel"), 48),
        name="diff_latent" if latent else "diff_context",
    )(*args)


def _gqa_kernel(*refs, latent, tq):
    if latent:
        (q_ref, k_ref, v_ref, nq_ref, nk_ref, cos_ref, sin_ref, ck_ref, cv_ref,
         out_ref, k_scr, v_scr) = refs
    else:
        (q_ref, k_ref, v_ref, nq_ref, nk_ref,
         out_ref, ckout_ref, cvout_ref, k_scr, v_scr) = refs
    n = q_ref.shape[0]
    seg128 = _seg_matrix(LANES)
    kn = _rmsnorm_seg(k_ref[...], nk_ref[...], seg128, GQA_DH)
    v = v_ref[...]
    if latent:
        kn = _rope(kn, cos_ref[...], sin_ref[...], GQA_DH)
        k_scr[n:, :] = ck_ref[0, 0].astype(BF16)
        v_scr[n:, :] = cv_ref[0, 0].astype(BF16)
    else:
        ckout_ref[...] = kn
        cvout_ref[...] = v
    k_scr[0:n, :] = kn.astype(BF16)
    v_scr[0:n, :] = v.astype(BF16)
    kall = k_scr[...]
    vall = v_scr[...]
    scale = GQA_DH ** -0.5
    for r in range(GQA_REP):
        sl = slice(r * GQA_DH, (r + 1) * GQA_DH)
        qn = _rmsnorm_seg(q_ref[:, sl], nq_ref[...], seg128, GQA_DH)
        if latent:
            qn = _rope(qn, cos_ref[...], sin_ref[...], GQA_DH)
        qb16 = qn.astype(BF16)
        for blk in range(n // tq):
            s = _dot_nt(qb16[blk * tq:(blk + 1) * tq, :], kall) * scale
            e = jnp.exp(s - jnp.max(s, axis=-1, keepdims=True))
            p = e * (1.0 / jnp.sum(e, axis=-1, keepdims=True))
            out_ref[blk * tq:(blk + 1) * tq, sl] = _dot(p.astype(BF16), vall).astype(out_ref.dtype)


def _gqa_attention(z, nq, nk, cache_k, cache_v, layer, latent):
    qw = GQA_REP * GQA_DH
    if latent:
        n, nseq, row0, skv = DEC_SEQ, DEC_BATCH, TP // DEC_SEQ, DEC_SEQ + PAST_LEN
    else:
        n, nseq, row0, skv = SEQ, BATCH, 0, SEQ
    tq = 256
    vec = pl.BlockSpec((1, GQA_DH), lambda s, g: (0, 0))
    in_specs = [
        pl.BlockSpec((n, qw), lambda s, g: (row0 + s, C_GQ // qw + g)),
        pl.BlockSpec((n, GQA_DH), lambda s, g: (row0 + s, C_GK // GQA_DH + g)),
        pl.BlockSpec((n, GQA_DH), lambda s, g: (row0 + s, C_GV // GQA_DH + g)),
        vec, vec]
    args = [z, z, z, nq, nk]
    out_specs = [pl.BlockSpec((n, qw), lambda s, g: (s, g))]
    out_shape = [jax.ShapeDtypeStruct((nseq * n, GQA_HEADS * GQA_DH), BF16)]
    if latent:
        cos, sin = _rope_tables(GQA_DH, 1)
        tab = pl.BlockSpec((DEC_SEQ, GQA_DH), lambda s, g: (0, 0))
        cspec = pl.BlockSpec((1, 1, PAST_LEN, GQA_DH), lambda s, g: (s, layer, 0, g))
        in_specs += [tab, tab, cspec, cspec]
        args += [cos, sin, cache_k, cache_v]
    else:
        cblk = pl.BlockSpec((n, GQA_DH), lambda s, g: (s, g))
        out_specs += [cblk, cblk]
        out_shape += [jax.ShapeDtypeStruct((TP, GQA_KV_HEADS * GQA_DH), F32)] * 2
    return pl.pallas_call(
        functools.partial(_gqa_kernel, latent=latent, tq=tq),
        grid=(nseq, GQA_KV_HEADS),
        in_specs=in_specs,
        out_specs=out_specs,
        out_shape=out_shape,
        scratch_shapes=[pltpu.VMEM((skv, GQA_DH), BF16), pltpu.VMEM((skv, GQA_DH), BF16)],
        compiler_params=_cp(("parallel", "parallel"), 48),
        name="gqa_latent" if latent else "gqa_context",
    )(*args)


def _merge_kernel(of_ref, ob_ref, hg_ref, hn_ref, dp_ref, ds_ref, gp_ref, gs_ref,
                  g0_ref, g1_ref, g2_ref, wb_ref, m_ref, br_scr, *, n_prompt_blocks):
    i = pl.program_id(0)

    @pl.when(pl.program_id(1) == 0)
    def _():
        seg128 = _seg_matrix(LANES)
        for h in range(HGRN_HEADS):
            sl = slice(h * HGRN_DK, (h + 1) * HGRN_DK)
            o = of_ref[:, sl] + ob_ref[:, sl]
            g = hg_ref[:, sl]
            y = _rmsnorm_seg(o, hn_ref[...], seg128, HGRN_DK) * (g * jax.nn.sigmoid(g))
            br_scr[0, :, sl] = y.astype(BF16)

        @pl.when(i < n_prompt_blocks)
        def _():
            br_scr[1] = dp_ref[...]
            br_scr[2] = gp_ref[...]

        @pl.when(i >= n_prompt_blocks)
        def _():
            br_scr[1] = ds_ref[...]
            br_scr[2] = gs_ref[...]

    acc = jax.nn.sigmoid(g0_ref[...]) * _dot(br_scr[0], wb_ref[0, 0])
    acc += jax.nn.sigmoid(g1_ref[...]) * _dot(br_scr[1], wb_ref[0, 1])
    acc += jax.nn.sigmoid(g2_ref[...]) * _dot(br_scr[2], wb_ref[0, 2])
    m_ref[...] = acc.astype(m_ref.dtype)


def _merge(o_f, o_b, z, hgrn_norm_row, diff_p, diff_s, gqa_p, gqa_s, w_branch, layer):
    tm, tn = 512, 512
    npb = TP // tm
    nsb = TS // tm
    row = lambda i, j: (i, 0)
    prow = lambda i, j: (jnp.minimum(i, npb - 1), 0)
    srow = lambda i, j: (jnp.clip(i - npb, 0, nsb - 1), 0)
    gate = lambda c: pl.BlockSpec((tm, tn), lambda i, j: (i, (C_GATE + c * D_MODEL) // tn + j))
    return pl.pallas_call(
        functools.partial(_merge_kernel, n_prompt_blocks=npb),
        grid=(TT // tm, D_MODEL // tn),
        in_specs=[
            pl.BlockSpec((tm, HGRN_W), row),
            pl.BlockSpec((tm, HGRN_W), row),
            pl.BlockSpec((tm, HGRN_W), lambda i, j: (i, C_HG // HGRN_W)),
            pl.BlockSpec((1, HGRN_DK), lambda i, j: (0, 0)),
            pl.BlockSpec((tm, BRANCH_W), prow),
            pl.BlockSpec((tm, BRANCH_W), srow),
            pl.BlockSpec((tm, BRANCH_W), prow),
            pl.BlockSpec((tm, BRANCH_W), srow),
            gate(0), gate(1), gate(2),
            pl.BlockSpec((1, N_BRANCH, BRANCH_W, tn), lambda i, j: (layer, 0, 0, j)),
        ],
        out_specs=pl.BlockSpec((tm, tn), lambda i, j: (i, j)),
        out_shape=jax.ShapeDtypeStruct((TT, D_MODEL), BF16),
        scratch_shapes=[pltpu.VMEM((N_BRANCH, tm, BRANCH_W), BF16)],
        compiler_params=_cp(("parallel", "arbitrary"), 48),
        name="branch_merge",
    )(o_f, o_b, z, hgrn_norm_row, diff_p, diff_s, gqa_p, gqa_s, z, z, z, w_branch)


def _out_proj_kernel(m_ref, w_ref, x_ref, mod_ref, y_ref, *, gate_idx):
    y_ref[...] = x_ref[...] + mod_ref[0, 0, gate_idx:gate_idx + 1, :] * _dot(m_ref[...], w_ref[0])


def _out_proj(m, w_out, x, mods4, layer, gate_idx):
    tm, tn = 512, 1024
    return pl.pallas_call(
        functools.partial(_out_proj_kernel, gate_idx=gate_idx),
        grid=(TT // tm, D_MODEL // tn),
        in_specs=[
            pl.BlockSpec((tm, D_MODEL), lambda i, j: (i, 0)),
            pl.BlockSpec((1, D_MODEL, tn), lambda i, j: (layer, 0, j)),
            pl.BlockSpec((tm, tn), lambda i, j: (i, j)),
            pl.BlockSpec((1, 1, 6, tn), lambda i, j: (layer, _mod_row(i, tm), 0, j)),
        ],
        out_specs=pl.BlockSpec((tm, tn), lambda i, j: (i, j)),
        out_shape=jax.ShapeDtypeStruct((TT, D_MODEL), F32),
        compiler_params=_cp(("parallel", "parallel"), 48),
        name="out_proj",
    )(m, w_out, x, mods4)


_STAIR = [PEER_TOPK // (r + 1) for r in range(8)]


def _extract_top(s, k, want_rank):
    vals = []
    rank = jnp.full(s.shape, float(k), F32) if want_rank else None
    for r in range(k):
        m = jnp.max(s, axis=0, keepdims=True)
        hit = s >= m
        vals.append(m)
        if want_rank:
            rank = jnp.where(hit, float(r), rank)
        s = jnp.where(hit, NEG_INF, s)
    return vals, rank


def _route_kernel(q_ref, keys_ref, r1_ref, r2_ref, *, tm):
    kb = keys_ref[0].astype(BF16)
    sub8 = lax.broadcasted_iota(jnp.int32, (8, LANES), 0)
    for g in range(tm // LANES):
        rows = slice(g * LANES, (g + 1) * LANES)
        cols = slice(g * LANES, (g + 1) * LANES)
        s1 = _dot_nt(kb[0], q_ref[rows, 0:N_KEYS])
        s2 = _dot_nt(kb[1], q_ref[rows, N_KEYS:2 * N_KEYS])
        v1, _ = _extract_top(s1, PEER_TOPK, False)
        v2, rank2 = _extract_top(s2, PEER_TOPK, True)
        sv2 = jnp.concatenate(v2, axis=0)
        sv1_hi = jnp.concatenate(v1[8:], axis=0)
        cand = [v1[0] + sv2, v1[1] + sv2[0:8]]
        for r in range(2, 8):
            cand.append(jnp.where(sub8 < _STAIR[r], v1[r] + sv2[0:8], NEG_INF))
        cand.append(sv1_hi + v2[0])
        cand = jnp.concatenate(cand, axis=0)
        tau = _extract_top(cand, PEER_TOPK, False)[0][-1]
        cmax = v1[0] + v2[0]
        zsum = jnp.sum(jnp.where(cand >= tau, jnp.exp(cand - cmax), 0.0), axis=0, keepdims=True)
        rz = 1.0 / zsum
        cnt = jnp.zeros((N_KEYS, LANES), F32)
        for r in range(PEER_TOPK):
            cnt_r = jnp.sum(jnp.where(v1[r] + sv2 >= tau, 1.0, 0.0), axis=0, keepdims=True)
            cnt = jnp.where(s1 == v1[r], cnt_r, cnt)
        r1_ref[0, 0, :, cols] = cnt
        r1_ref[0, 1, :, cols] = jnp.exp(s1 - v1[0]) * rz
        r2_ref[0, 0, :, cols] = rank2
        r2_ref[0, 1, :, cols] = jnp.exp(s2 - v2[0])


def _peer_route(qp, peer_keys, layer):
    tm = 512
    blk = pl.BlockSpec((1, 2, N_KEYS, tm), lambda i, h: (h, 0, 0, i))
    return pl.pallas_call(
        functools.partial(_route_kernel, tm=tm),
        grid=(TT // tm, PEER_HEADS),
        in_specs=[
            pl.BlockSpec((tm, 2 * N_KEYS), lambda i, h: (i, h)),
            pl.BlockSpec((1, 2, N_KEYS, N_KEYS), lambda i, h: (layer, 0, 0, 0)),
        ],
        out_specs=[blk, blk],
        out_shape=[jax.ShapeDtypeStruct((PEER_HEADS, 2, N_KEYS, TT), F32)] * 2,
        compiler_params=_cp(("parallel", "parallel"), 32),
        name="peer_route",
    )(qp, peer_keys)


EXPERT_HALF = 512
N_HALVES = N_EXPERTS // EXPERT_HALF


def _expert_weights(at_ref, r1_ref, row0, r2_scr, g_ref, tm, blocks):
    packed = (N_KEYS // 16, 16, LANES)
    for ib in blocks:
        il = row0 + ib
        rows = slice(ib * N_KEYS, (ib + 1) * N_KEYS)
        for g in range(tm // LANES):
            cols = slice(g * LANES, (g + 1) * LANES)
            w = jnp.zeros(packed, BF16)
            for h in range(PEER_HEADS):
                rk = r2_scr[h, 0, :, cols].reshape(packed)
                e2 = r2_scr[h, 1, :, cols].reshape(packed)
                cnt = jnp.broadcast_to(r1_ref[h, 0, il:il + 1, cols], (16, LANES)).astype(BF16)[None]
                e1 = jnp.broadcast_to(r1_ref[h, 1, il:il + 1, cols], (16, LANES)).astype(BF16)[None]
                w = w + jnp.where(rk < cnt, e2, jnp.zeros_like(e2)) * e1
            a = at_ref[rows, cols]
            act = 0.5 * a * (1.0 + lax.erf(a * (2.0 ** -0.5)))
            gt = act.astype(BF16).reshape(packed) * w
            g_ref[rows, cols] = gt.reshape(N_KEYS, LANES)


def _expert_kernel(h_ref, ua_ref, ub_ref, vp_ref, va_ref, r1p_ref, r1c_ref, r2_ref, x_ref, mod_ref, y_ref,
                   r2_scr, at0, at1, g0, g1, acc_scr, *, tm, n_steps, gate_idx):
    s = pl.program_id(1)
    rows_half = EXPERT_HALF // N_KEYS

    @pl.when(s == 0)
    def _():
        acc_scr[...] = jnp.zeros_like(acc_scr)
        r2_scr[...] = r2_ref[...].astype(BF16)
        at1[...] = jnp.zeros_like(at1)

    def scores_piece(u_ref, at_ref, q):
        r = slice((q // 2) * 256, (q // 2 + 1) * 256)
        c = slice((q % 2) * 256, (q % 2 + 1) * 256)
        at_ref[r, c] = _dot_nt(u_ref[0, r, :], h_ref[c, :])

    def values_piece(g_ref, v_ref, q):
        n = slice(q * 512, (q + 1) * 512)
        acc_scr[:, n] += _dot_tn(g_ref[...], v_ref[0, :, n])

    @pl.when(s < n_steps)
    def _():
        for q in range(4):
            scores_piece(ua_ref, at0, q)
            _expert_weights(at1, r1p_ref, rows_half, r2_scr, g1, tm, [q])
        for q in range(4):
            scores_piece(ub_ref, at1, q)
            values_piece(g1, vp_ref, q)
            _expert_weights(at0, r1c_ref, 0, r2_scr, g0, tm, [q])
        for q in range(4):
            values_piece(g0, va_ref, q)

    @pl.when(s == n_steps)
    def _():
        _expert_weights(at1, r1p_ref, rows_half, r2_scr, g1, tm, range(4))
        total = acc_scr[...] + _dot_tn(g1[...], vp_ref[0])
        y_ref[...] = x_ref[...] + mod_ref[0, 0, gate_idx:gate_idx + 1, :] * total


def _peer_experts(h2, u, v, r1, r2, x, mods4, layer, gate_idx):
    tm = 512
    n_steps = N_HALVES // 2
    n_i1 = 2 * EXPERT_HALF // N_KEYS
    last = N_HALVES - 1
    half = lambda f: pl.BlockSpec((1, EXPERT_HALF, D_MODEL), lambda i, s: (layer, f(s), 0))
    r1spec = lambda f: pl.BlockSpec((PEER_HEADS, 2, n_i1, tm), lambda i, s: (0, 0, f(s), i))
    return pl.pallas_call(
        functools.partial(_expert_kernel, tm=tm, n_steps=n_steps, gate_idx=gate_idx),
        grid=(TT // tm, n_steps + 1),
        in_specs=[
            pl.BlockSpec((tm, D_MODEL), lambda i, s: (i, 0)),
            half(lambda s: jnp.minimum(2 * s, last)),
            half(lambda s: jnp.minimum(2 * s + 1, last)),
            half(lambda s: jnp.maximum(2 * s - 1, 0)),
            half(lambda s: jnp.minimum(2 * s, last)),
            r1spec(lambda s: jnp.maximum(s - 1, 0)),
            r1spec(lambda s: jnp.minimum(s, n_steps - 1)),
            pl.BlockSpec((PEER_HEADS, 2, N_KEYS, tm), lambda i, s: (0, 0, 0, i)),
            pl.BlockSpec((tm, D_MODEL), lambda i, s: (i, 0)),
            pl.BlockSpec((1, 1, 6, D_MODEL), lambda i, s: (layer, _mod_row(i, tm), 0, 0)),
        ],
        out_specs=pl.BlockSpec((tm, D_MODEL), lambda i, s: (i, 0)),
        out_shape=jax.ShapeDtypeStruct((TT, D_MODEL), F32),
        scratch_shapes=[
            pltpu.VMEM((PEER_HEADS, 2, N_KEYS, tm), BF16),
            pltpu.VMEM((EXPERT_HALF, tm), F32),
            pltpu.VMEM((EXPERT_HALF, tm), F32),
            pltpu.VMEM((EXPERT_HALF, tm), BF16),
            pltpu.VMEM((EXPERT_HALF, tm), BF16),
            pltpu.VMEM((tm, D_MODEL), F32),
        ],
        compiler_params=_cp(("parallel", "arbitrary"), 58),
        name="peer_experts",
    )(h2, u, u, v, v, r1, r1, r2, x, mods4)


def kernel(x_prompt, x_sample, c, cache_diff_k, cache_diff_v, cache_gqa_k, cache_gqa_v, state_hgrn, c_ctx,
           mod_w, mod_b, norm_mix, norm_ffn, w_in, hgrn_lb, hgrn_norm, diff_qk_norm, diff_lambda, diff_subln,
           gqa_qk_norm, w_branch, w_out, peer_wq, peer_keys, peer_u, peer_v):
    x = jnp.concatenate([x_prompt.reshape(TP, D_MODEL), x_sample.reshape(TS, D_MODEL)], axis=0)
    cond8 = jnp.concatenate([c_ctx[None, :], c, jnp.zeros((8 - 1 - DEC_BATCH, D_MODEL), F32)], axis=0)
    mods4 = _modulation(cond8, mod_w, mod_b).reshape(DEPTH, 8, 6, D_MODEL)

    lb_all = jnp.cumsum(jax.nn.softmax(hgrn_lb.astype(F32), axis=1), axis=1)
    lb_all = lb_all - lb_all[:, :1]

    w_in_b, peer_wq_b = w_in, peer_wq
    w_branch_b = w_branch.astype(BF16)
    w_out_b = w_out.astype(BF16)
    peer_u_b = peer_u.astype(BF16)
    peer_v_b = peer_v.astype(BF16)

    cdk = cache_diff_k.reshape(DEC_BATCH, DEPTH, PAST_LEN, DIFF_HEADS * 2 * DIFF_DK)
    cdv = cache_diff_v.reshape(DEC_BATCH, DEPTH, PAST_LEN, DIFF_HEADS * 2 * DIFF_DK)
    cgk = cache_gqa_k.reshape(DEC_BATCH, DEPTH, PAST_LEN, GQA_KV_HEADS * GQA_DH)
    cgv = cache_gqa_v.reshape(DEC_BATCH, DEPTH, PAST_LEN, GQA_KV_HEADS * GQA_DH)

    nk, nv, ngk, ngv, nst = [], [], [], [], []
    for l in range(DEPTH):
        lam_init = 0.8 - 0.6 * math.exp(-0.3 * l)
        z = _norm_matmul(x, mods4, l, norm_mix[l][None, :], w_in_b, sh_idx=0, sc_idx=1,
                         tm=1024, tn=512, out_dtype=F32, emit_h=False, name="in_proj")[0]

        o_f, s_f = _hgrn(z, lb_all[:, l], state_hgrn, l, rev=False)
        o_b, s_b = _hgrn(z, lb_all[:, l], state_hgrn, l, rev=True)

        nq_d = jnp.tile(diff_qk_norm[l, 0], 2)[None, :]
        nk_d = jnp.tile(diff_qk_norm[l, 1], 2)[None, :]
        sub = diff_subln[l][None, :]
        diff_p, ck_new, cv_new = _diff_attention(z, nq_d, nk_d, sub, diff_lambda[l], None, None, l,
                                                 lam_init, latent=False)
        diff_s = _diff_attention(z, nq_d, nk_d, sub, diff_lambda[l], cdk, cdv, l, lam_init, latent=True)[0]

        nq_g = gqa_qk_norm[l, 0][None, :]
        nk_g = gqa_qk_norm[l, 1][None, :]
        gqa_p, gk_new, gv_new = _gqa_attention(z, nq_g, nk_g, None, None, l, latent=False)
        gqa_s = _gqa_attention(z, nq_g, nk_g, cgk, cgv, l, latent=True)[0]

        m = _merge(o_f, o_b, z, hgrn_norm[l][None, :], diff_p, diff_s, gqa_p, gqa_s, w_branch_b, l)
        x = _out_proj(m, w_out_b, x, mods4, l, gate_idx=2)

        qp, h2 = _norm_matmul(x, mods4, l, norm_ffn[l][None, :], peer_wq_b, sh_idx=3, sc_idx=4,
                              tm=1024, tn=512, out_dtype=BF16, emit_h=True, name="peer_query")
        r1, r2 = _peer_route(qp, peer_keys, l)
        x = _peer_experts(h2, peer_u_b, peer_v_b, r1, r2, x, mods4, l, gate_idx=5)

        nk.append(ck_new.reshape(BATCH, SEQ, DIFF_HEADS, 2 * DIFF_DK))
        nv.append(cv_new.reshape(BATCH, SEQ, DIFF_HEADS, 2 * DIFF_DK))
        ngk.append(gk_new.reshape(BATCH, SEQ, GQA_KV_HEADS, GQA_DH))
        ngv.append(gv_new.reshape(BATCH, SEQ, GQA_KV_HEADS, GQA_DH))
        nst.append(jnp.concatenate([s_f, s_b], axis=1))

    y_prompt = x[:TP].reshape(BATCH, SEQ, D_MODEL)
    y_sample = x[TP:].reshape(DEC_BATCH, DEC_SEQ, D_MODEL)
    return (y_prompt, y_sample, jnp.stack(nk, axis=1), jnp.stack(nv, axis=1), jnp.stack(ngk, axis=1),
            jnp.stack(ngv, axis=1), jnp.stack(nst, axis=1))
```

```python
import functools
import math

import numpy as np
import jax
import jax.numpy as jnp
from jax import lax
from jax.experimental import pallas as pl
from jax.experimental.pallas import tpu as pltpu

F32 = jnp.float32
BF16 = jnp.bfloat16

D_MODEL = 2048
BATCH = 16
SEQ = 256
DEPTH = 2
DEC_BATCH = 2
DEC_SEQ = 1024
PAST_LEN = 256
GRID_W = 64
ROPE_THETA = 10000.0
NORM_EPS = 1e-6

HGRN_HEADS = 8
HGRN_DK = 128
HGRN_W = 1024
DIFF_HEADS = 8
DIFF_DK = 64
GQA_HEADS = 8
GQA_KV_HEADS = 2
GQA_DH = 128
GQA_REP = 4
N_BRANCH = 3
BRANCH_W = 1024
PEER_HEADS = 8
N_KEYS = 128
N_EXPERTS = N_KEYS * N_KEYS
PEER_TOPK = 16
IN_WIDTH = 15872

TP = BATCH * SEQ
TS = DEC_BATCH * DEC_SEQ
TT = TP + TS

C_HQ, C_HF0, C_HF1, C_HI, C_HG = 0, 1024, 2048, 3072, 4096
C_DQ, C_DK, C_DV = 5120, 6144, 7168
C_GQ, C_GK, C_GV = 8192, 9216, 9472
C_GATE = 9728

LANES = 128
CH = 128
NCH = TT // CH
NPC = TP // CH
CH_PER_PROMPT = SEQ // CH
CH_PER_SAMPLE = DEC_SEQ // CH
N_LEVELS = 7
NEG_INF = float("-inf")
POS_INF = float("inf")
MIN_NORMAL = 1.1754944e-38

_NT = (((1,), (1,)), ((), ()))
_TN = (((0,), (0,)), ((), ()))


def _cp(sem, vmem_mb):
    return pltpu.CompilerParams(dimension_semantics=sem, vmem_limit_bytes=vmem_mb * 1024 * 1024)


def _dot(a, b):
    return jnp.dot(a, b, preferred_element_type=F32)


def _dot_nt(a, b):
    return lax.dot_general(a, b, _NT, preferred_element_type=F32)


def _dot_tn(a, b):
    return lax.dot_general(a, b, _TN, preferred_element_type=F32)


def _sigmoid(x):
    return 0.5 * jnp.tanh(0.5 * x) + 0.5


def _split_bf16(x):
    hi = x.astype(BF16)
    lo = (x - hi.astype(F32)).astype(BF16)
    return hi, lo


def _seg_matrix(seg):
    r = lax.broadcasted_iota(jnp.int32, (LANES, LANES), 0) // seg
    c = lax.broadcasted_iota(jnp.int32, (LANES, LANES), 1) // seg
    return (r == c).astype(BF16)


def _rmsnorm_seg(x, gain_row, seg_mat, seg):
    hi, lo = _split_bf16(x * x)
    ss = _dot(hi, seg_mat) + _dot(lo, seg_mat)
    return x * lax.rsqrt(ss * (1.0 / seg) + NORM_EPS) * gain_row


def _mod_row(i, tm):
    n_p = TP // tm
    per = DEC_SEQ // tm
    return jnp.where(i < n_p, 0, 1 + (i - n_p) // per)


def _mod_kernel(cond_ref, w_ref, b_ref, out_ref):
    a = cond_ref[...]
    a = a * jax.nn.sigmoid(a)
    out_ref[0] = _dot(a.astype(BF16), w_ref[0].astype(BF16)) + b_ref[0]


def _modulation(cond8, mod_w, mod_b):
    tn = 1024
    n6 = 6 * D_MODEL
    return pl.pallas_call(
        _mod_kernel,
        grid=(DEPTH, n6 // tn),
        in_specs=[
            pl.BlockSpec((8, D_MODEL), lambda l, j: (0, 0)),
            pl.BlockSpec((1, D_MODEL, tn), lambda l, j: (l, 0, j)),
            pl.BlockSpec((1, 1, tn), lambda l, j: (l, 0, j)),
        ],
        out_specs=pl.BlockSpec((1, 8, tn), lambda l, j: (l, 0, j)),
        out_shape=jax.ShapeDtypeStruct((DEPTH, 8, n6), F32),
        compiler_params=_cp(("parallel", "parallel"), 40),
        name="modulation",
    )(cond8, mod_w, mod_b.reshape(DEPTH, 1, n6))


def _norm_mm_kernel(x_ref, mod_ref, gain_ref, w_ref, *rest, sh_idx, sc_idx, emit_h):
    if emit_h:
        z_ref, h_ref, h_scr = rest
    else:
        z_ref, h_scr = rest

    @pl.when(pl.program_id(1) == 0)
    def _():
        x = x_ref[...]
        ms = jnp.mean(x * x, axis=-1, keepdims=True)
        y = x * lax.rsqrt(ms + NORM_EPS) * gain_ref[...]
        h = y * (1.0 + mod_ref[0, 0, sc_idx:sc_idx + 1, :]) + mod_ref[0, 0, sh_idx:sh_idx + 1, :]
        hb = h.astype(BF16)
        h_scr[...] = hb
        if emit_h:
            h_ref[...] = hb

    z_ref[...] = _dot(h_scr[...], w_ref[0].astype(BF16)).astype(z_ref.dtype)


def _norm_matmul(x, mods4, layer, gain, w, *, sh_idx, sc_idx, tm, tn, out_dtype, emit_h, name):
    n = w.shape[2]
    out_shape = [jax.ShapeDtypeStruct((TT, n), out_dtype)]
    out_specs = [pl.BlockSpec((tm, tn), lambda i, j: (i, j))]
    if emit_h:
        out_shape.append(jax.ShapeDtypeStruct((TT, D_MODEL), BF16))
        out_specs.append(pl.BlockSpec((tm, D_MODEL), lambda i, j: (i, 0)))
    return pl.pallas_call(
        functools.partial(_norm_mm_kernel, sh_idx=sh_idx, sc_idx=sc_idx, emit_h=emit_h),
        grid=(TT // tm, n // tn),
        in_specs=[
            pl.BlockSpec((tm, D_MODEL), lambda i, j: (i, 0)),
            pl.BlockSpec((1, 1, 6, D_MODEL), lambda i, j: (layer, _mod_row(i, tm), 0, 0)),
            pl.BlockSpec((1, D_MODEL), lambda i, j: (0, 0)),
            pl.BlockSpec((1, D_MODEL, tn), lambda i, j: (layer, 0, j)),
        ],
        out_specs=out_specs,
        out_shape=out_shape,
        scratch_shapes=[pltpu.VMEM((tm, D_MODEL), BF16)],
        compiler_params=_cp(("parallel", "arbitrary"), 48),
        name=name,
    )(x, mods4, gain, w)


def _level_table(rev):
    t = np.arange(CH)[:, None]
    s = np.arange(CH)[None, :]
    x = t ^ s
    lev = np.full((CH, CH), -1, np.int32)
    nz = x > 0
    lev[nz] = np.floor(np.log2(x[nz])).astype(np.int32)
    valid = (t < s) if rev else (t > s)
    lev = np.where(valid, lev, -1)
    lev[np.arange(CH), np.arange(CH)] = N_LEVELS
    return lev.astype(np.int32)


def _bmid(b, b3, m, rev):
    off = m if rev else m - 1
    if m >= 8:
        pieces = []
        for j in range(CH // (2 * m)):
            idx = j * 2 * m + off
            pieces.append(jnp.broadcast_to(b[idx:idx + 1, :], (2 * m, LANES)))
        return pieces[0] if len(pieces) == 1 else jnp.concatenate(pieces, axis=0)
    sub = lax.broadcasted_iota(jnp.int32, (CH // 8, 8, LANES), 1)
    out = None
    for j in range(8 // (2 * m)):
        idx = j * 2 * m + off
        piece = jnp.broadcast_to(b3[:, idx:idx + 1, :], (CH // 8, 8, LANES))
        out = piece if out is None else jnp.where(sub >= j * 2 * m, piece, out)
    return out.reshape(CH, LANES)


def _level_operand(q, kk, b, b3, m, rev, row):
    if m < 8:
        e = jnp.exp(-jnp.abs(b - _bmid(b, b3, m, rev)))
        q_side = ((row // m) % 2) == (0 if rev else 1)
        return jnp.where(q_side, q, kk) * e
    pieces = []
    for j in range(CH // (2 * m)):
        lo = slice(j * 2 * m, j * 2 * m + m)
        hi = slice(j * 2 * m + m, (j + 1) * 2 * m)
        mid = j * 2 * m + (m if rev else m - 1)
        bm = b[mid:mid + 1, :]
        if rev:
            pieces += [q[lo] * jnp.exp(b[lo] - bm), kk[hi] * jnp.exp(bm - b[hi])]
        else:
            pieces += [kk[lo] * jnp.exp(bm - b[lo]), q[hi] * jnp.exp(b[hi] - bm)]
    return jnp.concatenate(pieces, axis=0)


def _hgrn_kernel(*refs, rev):
    q_ref, f_ref, v_ref, lb_ref, s0_ref, lev_ref = refs[:6]
    o_ref, sfin_ref, st_scr = refs[-3:]
    i = pl.program_id(0)
    c = (NCH - 1 - i) if rev else i
    is_prompt = c < NPC
    cp_first = (c % CH_PER_PROMPT) == 0
    cp_last = (c % CH_PER_PROMPT) == CH_PER_PROMPT - 1
    cs = jnp.maximum(c - NPC, 0)
    cs_first = (cs % CH_PER_SAMPLE) == 0
    cs_last = (cs % CH_PER_SAMPLE) == CH_PER_SAMPLE - 1
    if rev:
        start_p, end_p, start_s = cp_last, cp_first, cs_last
    else:
        start_p, end_p, start_s = cp_first, cp_last, cs_first

    @pl.when(jnp.logical_and(is_prompt, start_p))
    def _():
        st_scr[...] = jnp.zeros_like(st_scr)

    @pl.when(jnp.logical_and(jnp.logical_not(is_prompt), start_s))
    def _():
        for h in range(HGRN_HEADS):
            st_scr[h] = s0_ref[0, 0, 0, h].T

    row = lax.broadcasted_iota(jnp.int32, (CH, CH), 0)
    col = lax.broadcasted_iota(jnp.int32, (CH, CH), 1)
    tri = ((col >= row) if rev else (col <= row)).astype(BF16)
    lev = lev_ref[...]

    for h in range(HGRN_HEADS):
        sl = slice(h * HGRN_DK, (h + 1) * HGRN_DK)
        q = q_ref[:, sl]
        zf = f_ref[:, sl]
        v = v_ref[:, sl]
        lb = lb_ref[:, sl]
        f = lb + (1.0 - lb) * jax.nn.sigmoid(zf)
        kk = (1.0 - lb) * jax.nn.sigmoid(-zf)
        logf = jnp.log(jnp.maximum(f, MIN_NORMAL))
        hi, lo = _split_bf16(logf)
        b = _dot(tri, hi) + _dot(tri, lo)
        b3 = b.reshape(CH // 8, 8, LANES)
        vb = v.astype(BF16)

        a = jnp.where(lev == N_LEVELS, _dot_nt(q.astype(BF16), kk.astype(BF16)), 0.0)
        for lm in range(N_LEVELS):
            x = _level_operand(q, kk, b, b3, 1 << lm, rev, row).astype(BF16)
            a = jnp.where(lev == lm, _dot_nt(x, x), a)

        b_end = b[0:1, :] if rev else b[CH - 1:CH, :]
        qd = (q * jnp.exp(b)).astype(BF16)
        kd = (kk * jnp.exp(b_end - b)).astype(BF16)
        st = st_scr[h]
        o_ref[:, sl] = _dot(a.astype(BF16), vb) + _dot_nt(qd, st.astype(BF16))
        st_scr[h] = st * jnp.exp(b_end) + _dot_tn(vb, kd)

    @pl.when(jnp.logical_and(is_prompt, end_p))
    def _():
        for h in range(HGRN_HEADS):
            sfin_ref[0, 0, 0, h] = st_scr[h].T


def _hgrn(z, lb_l, state_hgrn, layer, rev, prev=None):
    d = 1 if rev else 0
    lev = jnp.asarray(_level_table(rev))

    def cidx(i):
        return (NCH - 1 - i) if rev else i

    def s0_map(i):
        b = jnp.clip((cidx(i) - NPC) // CH_PER_SAMPLE, 0, DEC_BATCH - 1)
        return (b, layer, d, 0, 0, 0)

    def sfin_map(i):
        return (jnp.minimum(cidx(i) // CH_PER_PROMPT, BATCH - 1), layer, d, 0, 0, 0)

    wblk = HGRN_W
    in_specs = [
        pl.BlockSpec((CH, wblk), lambda i: (cidx(i), C_HQ // wblk)),
        pl.BlockSpec((CH, wblk), lambda i: (cidx(i), (C_HF1 if rev else C_HF0) // wblk)),
        pl.BlockSpec((CH, wblk), lambda i: (cidx(i), C_HI // wblk)),
        pl.BlockSpec((1, wblk), lambda i: (0, 0)),
        pl.BlockSpec((1, 1, 1, HGRN_HEADS, HGRN_DK, HGRN_DK), s0_map),
        pl.BlockSpec((CH, CH), lambda i: (0, 0)),
    ]
    args = [z, z, z, lb_l[d:d + 1], state_hgrn, lev]
    aliases = {}
    if prev is not None:
        in_specs.append(pl.BlockSpec(memory_space=pl.ANY))
        args.append(prev)
        aliases = {6: 1}
    o, sfin = pl.pallas_call(
        functools.partial(_hgrn_kernel, rev=rev),
        grid=(NCH,),
        in_specs=in_specs,
        out_specs=[
            pl.BlockSpec((CH, wblk), lambda i: (cidx(i), 0)),
            pl.BlockSpec((1, 1, 1, HGRN_HEADS, HGRN_DK, HGRN_DK), sfin_map),
        ],
        out_shape=[
            jax.ShapeDtypeStruct((TT, HGRN_W), F32),
            jax.ShapeDtypeStruct((BATCH, DEPTH, 2, HGRN_HEADS, HGRN_DK, HGRN_DK), F32),
        ],
        input_output_aliases=aliases,
        scratch_shapes=[pltpu.VMEM((HGRN_HEADS, HGRN_DK, HGRN_DK), F32)],
        compiler_params=_cp(("arbitrary",), 32),
        name="hgrn_bwd" if rev else "hgrn_fwd",
    )(*args)
    return o, sfin


def _rope_tables(dim, copies):
    nfreq = dim // 4
    inv_freq = ROPE_THETA ** (-np.arange(nfreq, dtype=np.float64) / nfreq)
    t = np.arange(DEC_SEQ)
    pos_row = (t // GRID_W).astype(np.float64)
    pos_col = (t % GRID_W).astype(np.float64)
    lane = np.arange(dim)
    use_col = (lane // (dim // 2)) == 1
    fidx = lane % nfreq
    first = (lane % (dim // 2)) < nfreq
    pos = np.where(use_col[None, :], pos_col[:, None], pos_row[:, None])
    ang = pos * inv_freq[fidx][None, :]
    cos = np.cos(ang)
    sin = np.where(first[None, :], -np.sin(ang), np.sin(ang))
    cos = np.tile(cos, (1, copies)).astype(np.float32)
    sin = np.tile(sin, (1, copies)).astype(np.float32)
    return jnp.asarray(cos), jnp.asarray(sin)


def _rope(x, cos, sin, dim):
    nfreq = dim // 4
    lane = lax.broadcasted_iota(jnp.int32, x.shape, 1)
    first = (lane % (dim // 2)) < nfreq
    partner = jnp.where(first, pltpu.roll(x, LANES - nfreq, 1), pltpu.roll(x, nfreq, 1))
    return x * cos + partner * sin


def _diff_kernel(*refs, latent, lam_init, tq, hpb):
    if latent:
        (q_ref, k_ref, v_ref, nq_ref, nk_ref, sub_ref, lam_ref, cos_ref, sin_ref, ck_ref, cv_ref,
         out_ref, k_scr, v_scr) = refs
    else:
        q_ref, k_ref, v_ref, nq_ref, nk_ref, sub_ref, lam_ref = refs[:7]
        out_ref, ckout_ref, cvout_ref, k_scr, v_scr = refs[-5:]
    n = q_ref.shape[0]
    seg64 = _seg_matrix(DIFF_DK)
    seg128 = _seg_matrix(LANES)
    lp = lam_ref[...]
    lam = (jnp.exp(jnp.sum(lp[0:1] * lp[1:2], axis=-1, keepdims=True))
           - jnp.exp(jnp.sum(lp[2:3] * lp[3:4], axis=-1, keepdims=True)) + lam_init)
    lane = lax.broadcasted_iota(jnp.int32, (tq, LANES), 1)
    scale = DIFF_DK ** -0.5

    for hh in range(hpb):
        hs = slice(hh * LANES, (hh + 1) * LANES)
        qn = _rmsnorm_seg(q_ref[:, hs], nq_ref[...], seg64, DIFF_DK)
        kn = _rmsnorm_seg(k_ref[:, hs], nk_ref[...], seg64, DIFF_DK)
        v = v_ref[:, hs]
        if latent:
            qn = _rope(qn, cos_ref[...], sin_ref[...], DIFF_DK)
            kn = _rope(kn, cos_ref[...], sin_ref[...], DIFF_DK)
            k_scr[hh, n:, :] = ck_ref[0, 0, :, hs].astype(BF16)
            v_scr[hh, n:, :] = cv_ref[0, 0, :, hs].astype(BF16)
        else:
            ckout_ref[0, 0, :, hs] = kn
            cvout_ref[0, 0, :, hs] = v
        k_scr[hh, 0:n, :] = kn.astype(BF16)
        v_scr[hh, 0:n, :] = v.astype(BF16)
        kall = k_scr[hh]
        vall = v_scr[hh]
        for blk in range(n // tq):
            qb = qn[blk * tq:(blk + 1) * tq, :] * scale
            q1 = jnp.where(lane < DIFF_DK, qb, 0.0).astype(BF16)
            q2 = jnp.where(lane >= DIFF_DK, qb, 0.0).astype(BF16)
            s1 = _dot_nt(q1, kall)
            s2 = _dot_nt(q2, kall)
            e1 = jnp.exp(s1 - jnp.max(s1, axis=-1, keepdims=True))
            e2 = jnp.exp(s2 - jnp.max(s2, axis=-1, keepdims=True))
            r1 = 1.0 / jnp.sum(e1, axis=-1, keepdims=True)
            r2 = lam / jnp.sum(e2, axis=-1, keepdims=True)
            p = e1 * r1 - e2 * r2
            o = _dot(p.astype(BF16), vall)
            on = _rmsnorm_seg(o, sub_ref[...], seg128, LANES) * (1.0 - lam_init)
            out_ref[blk * tq:(blk + 1) * tq, hs] = on.astype(out_ref.dtype)


def _diff_attention(z, nq, nk, sub, lam_p, cache_k, cache_v, layer, lam_init, latent, prev=None):
    hw = 2 * DIFF_DK
    if latent:
        n, nseq, row0, skv, hpb = DEC_SEQ, DEC_BATCH, TP // DEC_SEQ, DEC_SEQ + PAST_LEN, 1
    else:
        n, nseq, row0, skv, hpb = SEQ, BATCH, 0, SEQ, 4
    bw = hpb * hw
    tq = 256
    zspec = lambda c0: pl.BlockSpec((n, bw), lambda s, h: (row0 + s, c0 // bw + h))
    vec = pl.BlockSpec((1, hw), lambda s, h: (0, 0))
    in_specs = [zspec(C_DQ), zspec(C_DK), zspec(C_DV), vec, vec, vec,
                pl.BlockSpec((4, DIFF_DK), lambda s, h: (0, 0))]
    args = [z, z, z, nq, nk, sub, lam_p]
    out_specs = [pl.BlockSpec((n, bw), lambda s, h: (s, h))]
    out_shape = [jax.ShapeDtypeStruct((nseq * n, DIFF_HEADS * hw), BF16)]
    aliases = {}
    if latent:
        cos, sin = _rope_tables(DIFF_DK, 2)
        tab = pl.BlockSpec((DEC_SEQ, hw), lambda s, h: (0, 0))
        cspec = pl.BlockSpec((1, 1, PAST_LEN, bw), lambda s, h: (s, layer, 0, h))
        in_specs += [tab, tab, cspec, cspec]
        args += [cos, sin, cache_k, cache_v]
    else:
        cblk = pl.BlockSpec((1, 1, n, bw), lambda s, h: (s, layer, 0, h))
        out_specs += [cblk, cblk]
        out_shape += [jax.ShapeDtypeStruct((BATCH, DEPTH, SEQ, DIFF_HEADS * hw), F32)] * 2
        if prev is not None:
            in_specs += [pl.BlockSpec(memory_space=pl.ANY)] * 2
            args += list(prev)
            aliases = {7: 1, 8: 2}
    return pl.pallas_call(
        functools.partial(_diff_kernel, latent=latent, lam_init=lam_init, tq=tq, hpb=hpb),
        grid=(nseq, DIFF_HEADS // hpb),
        in_specs=in_specs,
        out_specs=out_specs,
        out_shape=out_shape,
        input_output_aliases=aliases,
        scratch_shapes=[pltpu.VMEM((hpb, skv, hw), BF16), pltpu.VMEM((hpb, skv, hw), BF16)],
        compiler_params=_cp(("parallel", "parallel"), 48),
        name="diff_latent" if latent else "diff_context",
    )(*args)


def _gqa_kernel(*refs, latent, tq):
    if latent:
        (q_ref, k_ref, v_ref, nq_ref, nk_ref, cos_ref, sin_ref, ck_ref, cv_ref,
         out_ref, k_scr, v_scr) = refs
    else:
        q_ref, k_ref, v_ref, nq_ref, nk_ref = refs[:5]
        out_ref, ckout_ref, cvout_ref, k_scr, v_scr = refs[-5:]
    n = q_ref.shape[0]
    seg128 = _seg_matrix(LANES)
    kn = _rmsnorm_seg(k_ref[...], nk_ref[...], seg128, GQA_DH)
    v = v_ref[...]
    if latent:
        kn = _rope(kn, cos_ref[...], sin_ref[...], GQA_DH)
        k_scr[n:, :] = ck_ref[0, 0].astype(BF16)
        v_scr[n:, :] = cv_ref[0, 0].astype(BF16)
    else:
        ckout_ref[0, 0] = kn
        cvout_ref[0, 0] = v
    k_scr[0:n, :] = kn.astype(BF16)
    v_scr[0:n, :] = v.astype(BF16)
    kall = k_scr[...]
    vall = v_scr[...]
    scale = GQA_DH ** -0.5
    for r in range(GQA_REP):
        sl = slice(r * GQA_DH, (r + 1) * GQA_DH)
        qn = _rmsnorm_seg(q_ref[:, sl], nq_ref[...], seg128, GQA_DH)
        if latent:
            qn = _rope(qn, cos_ref[...], sin_ref[...], GQA_DH)
        qb16 = qn.astype(BF16)
        for blk in range(n // tq):
            s = _dot_nt(qb16[blk * tq:(blk + 1) * tq, :], kall) * scale
            e = jnp.exp(s - jnp.max(s, axis=-1, keepdims=True))
            p = e * (1.0 / jnp.sum(e, axis=-1, keepdims=True))
            out_ref[blk * tq:(blk + 1) * tq, sl] = _dot(p.astype(BF16), vall).astype(out_ref.dtype)


def _gqa_attention(z, nq, nk, cache_k, cache_v, layer, latent, prev=None):
    qw = GQA_REP * GQA_DH
    aliases = {}
    if latent:
        n, nseq, row0, skv = DEC_SEQ, DEC_BATCH, TP // DEC_SEQ, DEC_SEQ + PAST_LEN
    else:
        n, nseq, row0, skv = SEQ, BATCH, 0, SEQ
    tq = 256
    vec = pl.BlockSpec((1, GQA_DH), lambda s, g: (0, 0))
    in_specs = [
        pl.BlockSpec((n, qw), lambda s, g: (row0 + s, C_GQ // qw + g)),
        pl.BlockSpec((n, GQA_DH), lambda s, g: (row0 + s, C_GK // GQA_DH + g)),
        pl.BlockSpec((n, GQA_DH), lambda s, g: (row0 + s, C_GV // GQA_DH + g)),
        vec, vec]
    args = [z, z, z, nq, nk]
    out_specs = [pl.BlockSpec((n, qw), lambda s, g: (s, g))]
    out_shape = [jax.ShapeDtypeStruct((nseq * n, GQA_HEADS * GQA_DH), BF16)]
    if latent:
        cos, sin = _rope_tables(GQA_DH, 1)
        tab = pl.BlockSpec((DEC_SEQ, GQA_DH), lambda s, g: (0, 0))
        cspec = pl.BlockSpec((1, 1, PAST_LEN, GQA_DH), lambda s, g: (s, layer, 0, g))
        in_specs += [tab, tab, cspec, cspec]
        args += [cos, sin, cache_k, cache_v]
    else:
        cblk = pl.BlockSpec((1, 1, n, GQA_DH), lambda s, g: (s, layer, 0, g))
        out_specs += [cblk, cblk]
        out_shape += [jax.ShapeDtypeStruct((BATCH, DEPTH, SEQ, GQA_KV_HEADS * GQA_DH), F32)] * 2
        if prev is not None:
            in_specs += [pl.BlockSpec(memory_space=pl.ANY)] * 2
            args += list(prev)
            aliases = {5: 1, 6: 2}
    return pl.pallas_call(
        functools.partial(_gqa_kernel, latent=latent, tq=tq),
        grid=(nseq, GQA_KV_HEADS),
        in_specs=in_specs,
        out_specs=out_specs,
        out_shape=out_shape,
        input_output_aliases=aliases,
        scratch_shapes=[pltpu.VMEM((skv, GQA_DH), BF16), pltpu.VMEM((skv, GQA_DH), BF16)],
        compiler_params=_cp(("parallel", "parallel"), 48),
        name="gqa_latent" if latent else "gqa_context",
    )(*args)


def _merge_kernel(of_ref, ob_ref, hg_ref, hn_ref, dp_ref, ds_ref, gp_ref, gs_ref,
                  g0_ref, g1_ref, g2_ref, wb_ref, m_ref, br_scr, *, n_prompt_blocks):
    i = pl.program_id(0)

    @pl.when(pl.program_id(1) == 0)
    def _():
        seg128 = _seg_matrix(LANES)
        for h in range(HGRN_HEADS):
            sl = slice(h * HGRN_DK, (h + 1) * HGRN_DK)
            o = of_ref[:, sl] + ob_ref[:, sl]
            g = hg_ref[:, sl]
            y = _rmsnorm_seg(o, hn_ref[...], seg128, HGRN_DK) * (g * _sigmoid(g))
            br_scr[0, :, sl] = y.astype(BF16)

        @pl.when(i < n_prompt_blocks)
        def _():
            br_scr[1] = dp_ref[...]
            br_scr[2] = gp_ref[...]

        @pl.when(i >= n_prompt_blocks)
        def _():
            br_scr[1] = ds_ref[...]
            br_scr[2] = gs_ref[...]

    acc = _sigmoid(g0_ref[...]) * _dot(br_scr[0], wb_ref[0, 0])
    acc += _sigmoid(g1_ref[...]) * _dot(br_scr[1], wb_ref[0, 1])
    acc += _sigmoid(g2_ref[...]) * _dot(br_scr[2], wb_ref[0, 2])
    m_ref[...] = acc.astype(m_ref.dtype)


def _merge(o_f, o_b, z, hgrn_norm_row, diff_p, diff_s, gqa_p, gqa_s, w_branch, layer):
    tm, tn = 512, 512
    npb = TP // tm
    nsb = TS // tm
    row = lambda i, j: (i, 0)
    prow = lambda i, j: (jnp.minimum(i, npb - 1), 0)
    srow = lambda i, j: (jnp.clip(i - npb, 0, nsb - 1), 0)
    gate = lambda c: pl.BlockSpec((tm, tn), lambda i, j: (i, (C_GATE + c * D_MODEL) // tn + j))
    return pl.pallas_call(
        functools.partial(_merge_kernel, n_prompt_blocks=npb),
        grid=(TT // tm, D_MODEL // tn),
        in_specs=[
            pl.BlockSpec((tm, HGRN_W), row),
            pl.BlockSpec((tm, HGRN_W), row),
            pl.BlockSpec((tm, HGRN_W), lambda i, j: (i, C_HG // HGRN_W)),
            pl.BlockSpec((1, HGRN_DK), lambda i, j: (0, 0)),
            pl.BlockSpec((tm, BRANCH_W), prow),
            pl.BlockSpec((tm, BRANCH_W), srow),
            pl.BlockSpec((tm, BRANCH_W), prow),
            pl.BlockSpec((tm, BRANCH_W), srow),
            gate(0), gate(1), gate(2),
            pl.BlockSpec((1, N_BRANCH, BRANCH_W, tn), lambda i, j: (layer, 0, 0, j)),
        ],
        out_specs=pl.BlockSpec((tm, tn), lambda i, j: (i, j)),
        out_shape=jax.ShapeDtypeStruct((TT, D_MODEL), BF16),
        scratch_shapes=[pltpu.VMEM((N_BRANCH, tm, BRANCH_W), BF16)],
        compiler_params=_cp(("parallel", "arbitrary"), 48),
        name="branch_merge",
    )(o_f, o_b, z, hgrn_norm_row, diff_p, diff_s, gqa_p, gqa_s, z, z, z, w_branch)


def _out_proj_kernel(m_ref, w_ref, x_ref, mod_ref, y_ref, *, gate_idx):
    y_ref[...] = x_ref[...] + mod_ref[0, 0, gate_idx:gate_idx + 1, :] * _dot(m_ref[...], w_ref[0])


def _out_proj(m, w_out, x, mods4, layer, gate_idx):
    tm, tn = 512, 1024
    return pl.pallas_call(
        functools.partial(_out_proj_kernel, gate_idx=gate_idx),
        grid=(TT // tm, D_MODEL // tn),
        in_specs=[
            pl.BlockSpec((tm, D_MODEL), lambda i, j: (i, 0)),
            pl.BlockSpec((1, D_MODEL, tn), lambda i, j: (layer, 0, j)),
            pl.BlockSpec((tm, tn), lambda i, j: (i, j)),
            pl.BlockSpec((1, 1, 6, tn), lambda i, j: (layer, _mod_row(i, tm), 0, j)),
        ],
        out_specs=pl.BlockSpec((tm, tn), lambda i, j: (i, j)),
        out_shape=jax.ShapeDtypeStruct((TT, D_MODEL), F32),
        compiler_params=_cp(("parallel", "parallel"), 48),
        name="out_proj",
    )(m, w_out, x, mods4)


_STAIR = [PEER_TOPK // (r + 1) for r in range(8)]


def _extract_top(s, k, want_rank):
    vals = []
    rank = jnp.full(s.shape, float(k), F32) if want_rank else None
    for r in range(k):
        m = jnp.max(s, axis=0, keepdims=True)
        hit = s >= m
        vals.append(m)
        if want_rank:
            rank = jnp.where(hit, float(r), rank)
        s = jnp.where(hit, NEG_INF, s)
    return vals, rank


def _route_kernel(q_ref, keys_ref, r1_ref, r2_ref, *, tm):
    kb = keys_ref[0].astype(BF16)
    sub8 = lax.broadcasted_iota(jnp.int32, (8, LANES), 0)
    for g in range(tm // LANES):
        rows = slice(g * LANES, (g + 1) * LANES)
        cols = slice(g * LANES, (g + 1) * LANES)
        s1 = _dot_nt(kb[0], q_ref[rows, 0:N_KEYS])
        s2 = _dot_nt(kb[1], q_ref[rows, N_KEYS:2 * N_KEYS])
        v1, _ = _extract_top(s1, PEER_TOPK, False)
        v2, rank2 = _extract_top(s2, PEER_TOPK, True)
        sv2 = jnp.concatenate(v2, axis=0)
        sv1_hi = jnp.concatenate(v1[8:], axis=0)
        cand = [v1[0] + sv2, v1[1] + sv2[0:8]]
        for r in range(2, 8):
            cand.append(jnp.where(sub8 < _STAIR[r], v1[r] + sv2[0:8], NEG_INF))
        cand.append(sv1_hi + v2[0])
        cand = jnp.concatenate(cand, axis=0)
        tau = _extract_top(cand, PEER_TOPK, False)[0][-1]
        cmax = v1[0] + v2[0]
        zsum = jnp.sum(jnp.where(cand >= tau, jnp.exp(cand - cmax), 0.0), axis=0, keepdims=True)
        rz = 1.0 / zsum
        cnt = jnp.zeros((N_KEYS, LANES), F32)
        for r in range(PEER_TOPK):
            cnt_r = jnp.sum(jnp.where(v1[r] + sv2 >= tau, 1.0, 0.0), axis=0, keepdims=True)
            cnt = jnp.where(s1 == v1[r], cnt_r, cnt)
        r1_ref[0, 0, :, cols] = cnt
        r1_ref[0, 1, :, cols] = jnp.exp(s1 - v1[0]) * rz
        r2_ref[0, 0, :, cols] = rank2
        r2_ref[0, 1, :, cols] = jnp.exp(s2 - v2[0])


def _peer_route(qp, peer_keys, layer):
    tm = 512
    blk = pl.BlockSpec((1, 2, N_KEYS, tm), lambda i, h: (h, 0, 0, i))
    return pl.pallas_call(
        functools.partial(_route_kernel, tm=tm),
        grid=(TT // tm, PEER_HEADS),
        in_specs=[
            pl.BlockSpec((tm, 2 * N_KEYS), lambda i, h: (i, h)),
            pl.BlockSpec((1, 2, N_KEYS, N_KEYS), lambda i, h: (layer, 0, 0, 0)),
        ],
        out_specs=[blk, blk],
        out_shape=[jax.ShapeDtypeStruct((PEER_HEADS, 2, N_KEYS, TT), F32)] * 2,
        compiler_params=_cp(("parallel", "parallel"), 32),
        name="peer_route",
    )(qp, peer_keys)


def _expert_kernel(h_ref, u_ref, v_ref, r1_ref, r2_ref, x_ref, mod_ref, y_ref,
                   r2_scr, at_scr, g_scr, acc_scr, *, tm, ce, n_parts, gate_idx):
    j = pl.program_id(1)

    @pl.when(j == 0)
    def _():
        acc_scr[...] = jnp.zeros_like(acc_scr)
        r2_scr[...] = r2_ref[...].astype(BF16)

    hb = h_ref[...]
    pe = ce // n_parts
    packed = (N_KEYS // 16, 16, LANES)
    contrib = None
    for p in range(n_parts):
        at_scr[p] = _dot_nt(u_ref[0, p * pe:(p + 1) * pe, :], hb)
        for ib in range(pe // N_KEYS):
            il = p * (pe // N_KEYS) + ib
            rows = slice(ib * N_KEYS, (ib + 1) * N_KEYS)
            for g in range(tm // LANES):
                cols = slice(g * LANES, (g + 1) * LANES)
                w = jnp.zeros(packed, BF16)
                for h in range(PEER_HEADS):
                    rk = r2_scr[h, 0, :, cols].reshape(packed)
                    e2 = r2_scr[h, 1, :, cols].reshape(packed)
                    cnt = jnp.broadcast_to(r1_ref[h, 0, il:il + 1, cols], (16, LANES)).astype(BF16)[None]
                    e1 = jnp.broadcast_to(r1_ref[h, 1, il:il + 1, cols], (16, LANES)).astype(BF16)[None]
                    w = w + jnp.where(rk < cnt, e2, jnp.zeros_like(e2)) * e1
                a = at_scr[p, rows, cols]
                act = 0.5 * a * (1.0 + lax.erf(a * (2.0 ** -0.5)))
                gt = act.astype(BF16).reshape(packed) * w
                g_scr[p, rows, cols] = gt.reshape(N_KEYS, LANES)
        d = _dot_tn(g_scr[p], v_ref[0, p * pe:(p + 1) * pe, :])
        contrib = d if contrib is None else contrib + d
    acc_scr[...] += contrib

    @pl.when(j == pl.num_programs(1) - 1)
    def _():
        y_ref[...] = x_ref[...] + mod_ref[0, 0, gate_idx:gate_idx + 1, :] * acc_scr[...]


def _peer_experts(h2, u, v, r1, r2, x, mods4, layer, gate_idx):
    tm, ce, n_parts = 512, 1024, 2
    n_i1 = ce // N_KEYS
    return pl.pallas_call(
        functools.partial(_expert_kernel, tm=tm, ce=ce, n_parts=n_parts, gate_idx=gate_idx),
        grid=(TT // tm, N_EXPERTS // ce),
        in_specs=[
            pl.BlockSpec((tm, D_MODEL), lambda i, j: (i, 0)),
            pl.BlockSpec((1, ce, D_MODEL), lambda i, j: (layer, j, 0)),
            pl.BlockSpec((1, ce, D_MODEL), lambda i, j: (layer, j, 0)),
            pl.BlockSpec((PEER_HEADS, 2, n_i1, tm), lambda i, j: (0, 0, j, i)),
            pl.BlockSpec((PEER_HEADS, 2, N_KEYS, tm), lambda i, j: (0, 0, 0, i)),
            pl.BlockSpec((tm, D_MODEL), lambda i, j: (i, 0)),
            pl.BlockSpec((1, 1, 6, D_MODEL), lambda i, j: (layer, _mod_row(i, tm), 0, 0)),
        ],
        out_specs=pl.BlockSpec((tm, D_MODEL), lambda i, j: (i, 0)),
        out_shape=jax.ShapeDtypeStruct((TT, D_MODEL), F32),
        scratch_shapes=[
            pltpu.VMEM((PEER_HEADS, 2, N_KEYS, tm), BF16),
            pltpu.VMEM((n_parts, ce // n_parts, tm), F32),
            pltpu.VMEM((n_parts, ce // n_parts, tm), BF16),
            pltpu.VMEM((tm, D_MODEL), F32),
        ],
        compiler_params=_cp(("parallel", "arbitrary"), 56),
        name="peer_experts",
    )(h2, u, v, r1, r2, x, mods4)


def kernel(x_prompt, x_sample, c, cache_diff_k, cache_diff_v, cache_gqa_k, cache_gqa_v, state_hgrn, c_ctx,
           mod_w, mod_b, norm_mix, norm_ffn, w_in, hgrn_lb, hgrn_norm, diff_qk_norm, diff_lambda, diff_subln,
           gqa_qk_norm, w_branch, w_out, peer_wq, peer_keys, peer_u, peer_v):
    x = jnp.concatenate([x_prompt.reshape(TP, D_MODEL), x_sample.reshape(TS, D_MODEL)], axis=0)
    cond8 = jnp.concatenate([c_ctx[None, :], c, jnp.zeros((8 - 1 - DEC_BATCH, D_MODEL), F32)], axis=0)
    mods4 = _modulation(cond8, mod_w, mod_b).reshape(DEPTH, 8, 6, D_MODEL)

    lb_all = jnp.cumsum(jax.nn.softmax(hgrn_lb.astype(F32), axis=1), axis=1)
    lb_all = lb_all - lb_all[:, :1]

    w_in_b, peer_wq_b = w_in, peer_wq
    w_branch_b = w_branch.astype(BF16)
    w_out_b = w_out.astype(BF16)
    peer_u_b = peer_u.astype(BF16)
    peer_v_b = peer_v.astype(BF16)

    cdk = cache_diff_k.reshape(DEC_BATCH, DEPTH, PAST_LEN, DIFF_HEADS * 2 * DIFF_DK)
    cdv = cache_diff_v.reshape(DEC_BATCH, DEPTH, PAST_LEN, DIFF_HEADS * 2 * DIFF_DK)
    cgk = cache_gqa_k.reshape(DEC_BATCH, DEPTH, PAST_LEN, GQA_KV_HEADS * GQA_DH)
    cgv = cache_gqa_v.reshape(DEC_BATCH, DEPTH, PAST_LEN, GQA_KV_HEADS * GQA_DH)

    new_state = None
    diff_kv = None
    gqa_kv = None
    for l in range(DEPTH):
        lam_init = 0.8 - 0.6 * math.exp(-0.3 * l)
        z = _norm_matmul(x, mods4, l, norm_mix[l][None, :], w_in_b, sh_idx=0, sc_idx=1,
                         tm=1024, tn=512, out_dtype=F32, emit_h=False, name="in_proj")[0]

        o_f, new_state = _hgrn(z, lb_all[:, l], state_hgrn, l, rev=False, prev=new_state)
        o_b, new_state = _hgrn(z, lb_all[:, l], state_hgrn, l, rev=True, prev=new_state)

        nq_d = jnp.tile(diff_qk_norm[l, 0], 2)[None, :]
        nk_d = jnp.tile(diff_qk_norm[l, 1], 2)[None, :]
        sub = diff_subln[l][None, :]
        diff_p, *diff_kv = _diff_attention(z, nq_d, nk_d, sub, diff_lambda[l], None, None, l,
                                           lam_init, latent=False, prev=diff_kv)
        diff_s = _diff_attention(z, nq_d, nk_d, sub, diff_lambda[l], cdk, cdv, l, lam_init, latent=True)[0]

        nq_g = gqa_qk_norm[l, 0][None, :]
        nk_g = gqa_qk_norm[l, 1][None, :]
        gqa_p, *gqa_kv = _gqa_attention(z, nq_g, nk_g, None, None, l, latent=False, prev=gqa_kv)
        gqa_s = _gqa_attention(z, nq_g, nk_g, cgk, cgv, l, latent=True)[0]

        m = _merge(o_f, o_b, z, hgrn_norm[l][None, :], diff_p, diff_s, gqa_p, gqa_s, w_branch_b, l)
        x = _out_proj(m, w_out_b, x, mods4, l, gate_idx=2)

        qp, h2 = _norm_matmul(x, mods4, l, norm_ffn[l][None, :], peer_wq_b, sh_idx=3, sc_idx=4,
                              tm=1024, tn=512, out_dtype=BF16, emit_h=True, name="peer_query")
        r1, r2 = _peer_route(qp, peer_keys, l)
        x = _peer_experts(h2, peer_u_b, peer_v_b, r1, r2, x, mods4, l, gate_idx=5)

    return (x[:TP].reshape(BATCH, SEQ, D_MODEL), x[TP:].reshape(DEC_BATCH, DEC_SEQ, D_MODEL),
            diff_kv[0].reshape(BATCH, DEPTH, SEQ, DIFF_HEADS, 2 * DIFF_DK),
            diff_kv[1].reshape(BATCH, DEPTH, SEQ, DIFF_HEADS, 2 * DIFF_DK),
            gqa_kv[0].reshape(BATCH, DEPTH, SEQ, GQA_KV_HEADS, GQA_DH),
            gqa_kv[1].reshape(BATCH, DEPTH, SEQ, GQA_KV_HEADS, GQA_DH),
            new_state)
```

```python
import functools
import math

import numpy as np
import jax
import jax.numpy as jnp
from jax import lax
from jax.experimental import pallas as pl
from jax.experimental.pallas import tpu as pltpu

F32 = jnp.float32
BF16 = jnp.bfloat16

D_MODEL = 2048
BATCH = 16
SEQ = 256
DEPTH = 2
DEC_BATCH = 2
DEC_SEQ = 1024
PAST_LEN = 256
GRID_W = 64
ROPE_THETA = 10000.0
NORM_EPS = 1e-6

HGRN_HEADS = 8
HGRN_DK = 128
HGRN_W = 1024
DIFF_HEADS = 8
DIFF_DK = 64
GQA_HEADS = 8
GQA_KV_HEADS = 2
GQA_DH = 128
GQA_REP = 4
N_BRANCH = 3
BRANCH_W = 1024
PEER_HEADS = 8
N_KEYS = 128
N_EXPERTS = N_KEYS * N_KEYS
PEER_TOPK = 16
IN_WIDTH = 15872

TP = BATCH * SEQ
TS = DEC_BATCH * DEC_SEQ
TT = TP + TS

C_HQ, C_HF0, C_HF1, C_HI, C_HG = 0, 1024, 2048, 3072, 4096
C_DQ, C_DK, C_DV = 5120, 6144, 7168
C_GQ, C_GK, C_GV = 8192, 9216, 9472
C_GATE = 9728

LANES = 128
CH = 128
NCH = TT // CH
NPC = TP // CH
CH_PER_PROMPT = SEQ // CH
CH_PER_SAMPLE = DEC_SEQ // CH
N_LEVELS = 7
NEG_INF = float("-inf")
POS_INF = float("inf")
MIN_NORMAL = 1.1754944e-38

_NT = (((1,), (1,)), ((), ()))
_TN = (((0,), (0,)), ((), ()))


def _cp(sem, vmem_mb):
    return pltpu.CompilerParams(dimension_semantics=sem, vmem_limit_bytes=vmem_mb * 1024 * 1024)


def _dot(a, b):
    return jnp.dot(a, b, preferred_element_type=F32)


def _dot_nt(a, b):
    return lax.dot_general(a, b, _NT, preferred_element_type=F32)


def _dot_tn(a, b):
    return lax.dot_general(a, b, _TN, preferred_element_type=F32)


def _sigmoid(x):
    return 0.5 * jnp.tanh(0.5 * x) + 0.5


def _split_bf16(x):
    hi = x.astype(BF16)
    lo = (x - hi.astype(F32)).astype(BF16)
    return hi, lo


def _seg_matrix(seg):
    r = lax.broadcasted_iota(jnp.int32, (LANES, LANES), 0) // seg
    c = lax.broadcasted_iota(jnp.int32, (LANES, LANES), 1) // seg
    return (r == c).astype(BF16)


def _rmsnorm_seg(x, gain_row, seg_mat, seg):
    hi, lo = _split_bf16(x * x)
    ss = _dot(hi, seg_mat) + _dot(lo, seg_mat)
    return x * lax.rsqrt(ss * (1.0 / seg) + NORM_EPS) * gain_row


def _mod_row(i, tm):
    n_p = TP // tm
    per = DEC_SEQ // tm
    return jnp.where(i < n_p, 0, 1 + (i - n_p) // per)


def _mod_kernel(cond_ref, w_ref, b_ref, out_ref):
    a = cond_ref[...]
    a = a * jax.nn.sigmoid(a)
    out_ref[0] = _dot(a.astype(BF16), w_ref[0].astype(BF16)) + b_ref[0]


def _modulation(cond8, mod_w, mod_b):
    tn = 1024
    n6 = 6 * D_MODEL
    return pl.pallas_call(
        _mod_kernel,
        grid=(DEPTH, n6 // tn),
        in_specs=[
            pl.BlockSpec((8, D_MODEL), lambda l, j: (0, 0)),
            pl.BlockSpec((1, D_MODEL, tn), lambda l, j: (l, 0, j)),
            pl.BlockSpec((1, 1, tn), lambda l, j: (l, 0, j)),
        ],
        out_specs=pl.BlockSpec((1, 8, tn), lambda l, j: (l, 0, j)),
        out_shape=jax.ShapeDtypeStruct((DEPTH, 8, n6), F32),
        compiler_params=_cp(("parallel", "parallel"), 40),
        name="modulation",
    )(cond8, mod_w, mod_b.reshape(DEPTH, 1, n6))


def _norm_mm_kernel(x_ref, mod_ref, gain_ref, w_ref, *rest, sh_idx, sc_idx, emit_h):
    if emit_h:
        z_ref, h_ref, h_scr = rest
    else:
        z_ref, h_scr = rest

    @pl.when(pl.program_id(1) == 0)
    def _():
        x = x_ref[...]
        ms = jnp.mean(x * x, axis=-1, keepdims=True)
        y = x * lax.rsqrt(ms + NORM_EPS) * gain_ref[...]
        h = y * (1.0 + mod_ref[0, 0, sc_idx:sc_idx + 1, :]) + mod_ref[0, 0, sh_idx:sh_idx + 1, :]
        hb = h.astype(BF16)
        h_scr[...] = hb
        if emit_h:
            h_ref[...] = hb

    z_ref[...] = _dot(h_scr[...], w_ref[0].astype(BF16)).astype(z_ref.dtype)


def _norm_matmul(x, mods4, layer, gain, w, *, sh_idx, sc_idx, tm, tn, out_dtype, emit_h, name):
    n = w.shape[2]
    out_shape = [jax.ShapeDtypeStruct((TT, n), out_dtype)]
    out_specs = [pl.BlockSpec((tm, tn), lambda i, j: (i, j))]
    if emit_h:
        out_shape.append(jax.ShapeDtypeStruct((TT, D_MODEL), BF16))
        out_specs.append(pl.BlockSpec((tm, D_MODEL), lambda i, j: (i, 0)))
    return pl.pallas_call(
        functools.partial(_norm_mm_kernel, sh_idx=sh_idx, sc_idx=sc_idx, emit_h=emit_h),
        grid=(TT // tm, n // tn),
        in_specs=[
            pl.BlockSpec((tm, D_MODEL), lambda i, j: (i, 0)),
            pl.BlockSpec((1, 1, 6, D_MODEL), lambda i, j: (layer, _mod_row(i, tm), 0, 0)),
            pl.BlockSpec((1, D_MODEL), lambda i, j: (0, 0)),
            pl.BlockSpec((1, D_MODEL, tn), lambda i, j: (layer, 0, j)),
        ],
        out_specs=out_specs,
        out_shape=out_shape,
        scratch_shapes=[pltpu.VMEM((tm, D_MODEL), BF16)],
        compiler_params=_cp(("parallel", "arbitrary"), 48),
        name=name,
    )(x, mods4, gain, w)


def _level_table(rev):
    t = np.arange(CH)[:, None]
    s = np.arange(CH)[None, :]
    x = t ^ s
    lev = np.full((CH, CH), -1, np.int32)
    nz = x > 0
    lev[nz] = np.floor(np.log2(x[nz])).astype(np.int32)
    valid = (t < s) if rev else (t > s)
    lev = np.where(valid, lev, -1)
    lev[np.arange(CH), np.arange(CH)] = N_LEVELS
    return lev.astype(np.int32)


def _bmid(b, b3, m, rev):
    off = m if rev else m - 1
    if m >= 8:
        pieces = []
        for j in range(CH // (2 * m)):
            idx = j * 2 * m + off
            pieces.append(jnp.broadcast_to(b[idx:idx + 1, :], (2 * m, LANES)))
        return pieces[0] if len(pieces) == 1 else jnp.concatenate(pieces, axis=0)
    sub = lax.broadcasted_iota(jnp.int32, (CH // 8, 8, LANES), 1)
    out = None
    for j in range(8 // (2 * m)):
        idx = j * 2 * m + off
        piece = jnp.broadcast_to(b3[:, idx:idx + 1, :], (CH // 8, 8, LANES))
        out = piece if out is None else jnp.where(sub >= j * 2 * m, piece, out)
    return out.reshape(CH, LANES)


def _level_operand(q, kk, b, b3, m, rev, row):
    if m < 8:
        e = jnp.exp(-jnp.abs(b - _bmid(b, b3, m, rev)))
        q_side = ((row // m) % 2) == (0 if rev else 1)
        return jnp.where(q_side, q, kk) * e
    pieces = []
    for j in range(CH // (2 * m)):
        lo = slice(j * 2 * m, j * 2 * m + m)
        hi = slice(j * 2 * m + m, (j + 1) * 2 * m)
        mid = j * 2 * m + (m if rev else m - 1)
        bm = b[mid:mid + 1, :]
        if rev:
            pieces += [q[lo] * jnp.exp(b[lo] - bm), kk[hi] * jnp.exp(bm - b[hi])]
        else:
            pieces += [kk[lo] * jnp.exp(bm - b[lo]), q[hi] * jnp.exp(b[hi] - bm)]
    return jnp.concatenate(pieces, axis=0)


def _hgrn_kernel(*refs, rev):
    q_ref, f_ref, v_ref, lb_ref, s0_ref, lev_ref = refs[:6]
    o_ref, sfin_ref, st_scr = refs[-3:]
    i = pl.program_id(0)
    c = (NCH - 1 - i) if rev else i
    is_prompt = c < NPC
    cp_first = (c % CH_PER_PROMPT) == 0
    cp_last = (c % CH_PER_PROMPT) == CH_PER_PROMPT - 1
    cs = jnp.maximum(c - NPC, 0)
    cs_first = (cs % CH_PER_SAMPLE) == 0
    cs_last = (cs % CH_PER_SAMPLE) == CH_PER_SAMPLE - 1
    if rev:
        start_p, end_p, start_s = cp_last, cp_first, cs_last
    else:
        start_p, end_p, start_s = cp_first, cp_last, cs_first

    @pl.when(jnp.logical_and(is_prompt, start_p))
    def _():
        st_scr[...] = jnp.zeros_like(st_scr)

    @pl.when(jnp.logical_and(jnp.logical_not(is_prompt), start_s))
    def _():
        for h in range(HGRN_HEADS):
            st_scr[h] = s0_ref[0, 0, 0, h].T

    row = lax.broadcasted_iota(jnp.int32, (CH, CH), 0)
    col = lax.broadcasted_iota(jnp.int32, (CH, CH), 1)
    tri = ((col >= row) if rev else (col <= row)).astype(BF16)
    lev = lev_ref[...]

    for h in range(HGRN_HEADS):
        sl = slice(h * HGRN_DK, (h + 1) * HGRN_DK)
        q = q_ref[:, sl]
        zf = f_ref[:, sl]
        v = v_ref[:, sl]
        lb = lb_ref[:, sl]
        f = lb + (1.0 - lb) * jax.nn.sigmoid(zf)
        kk = (1.0 - lb) * jax.nn.sigmoid(-zf)
        logf = jnp.log(jnp.maximum(f, MIN_NORMAL))
        hi, lo = _split_bf16(logf)
        b = _dot(tri, hi) + _dot(tri, lo)
        b3 = b.reshape(CH // 8, 8, LANES)
        vb = v.astype(BF16)

        a = jnp.where(lev == N_LEVELS, _dot_nt(q.astype(BF16), kk.astype(BF16)), 0.0)
        for lm in range(N_LEVELS):
            x = _level_operand(q, kk, b, b3, 1 << lm, rev, row).astype(BF16)
            a = jnp.where(lev == lm, _dot_nt(x, x), a)

        b_end = b[0:1, :] if rev else b[CH - 1:CH, :]
        qd = (q * jnp.exp(b)).astype(BF16)
        kd = (kk * jnp.exp(b_end - b)).astype(BF16)
        st = st_scr[h]
        o_ref[:, sl] = _dot(a.astype(BF16), vb) + _dot_nt(qd, st.astype(BF16))
        st_scr[h] = st * jnp.exp(b_end) + _dot_tn(vb, kd)

    @pl.when(jnp.logical_and(is_prompt, end_p))
    def _():
        for h in range(HGRN_HEADS):
            sfin_ref[0, 0, 0, h] = st_scr[h].T


def _hgrn(z, lb_l, state_hgrn, layer, rev, prev=None):
    d = 1 if rev else 0
    lev = jnp.asarray(_level_table(rev))

    def cidx(i):
        return (NCH - 1 - i) if rev else i

    def s0_map(i):
        b = jnp.clip((cidx(i) - NPC) // CH_PER_SAMPLE, 0, DEC_BATCH - 1)
        return (b, layer, d, 0, 0, 0)

    def sfin_map(i):
        return (jnp.minimum(cidx(i) // CH_PER_PROMPT, BATCH - 1), layer, d, 0, 0, 0)

    wblk = HGRN_W
    in_specs = [
        pl.BlockSpec((CH, wblk), lambda i: (cidx(i), C_HQ // wblk)),
        pl.BlockSpec((CH, wblk), lambda i: (cidx(i), (C_HF1 if rev else C_HF0) // wblk)),
        pl.BlockSpec((CH, wblk), lambda i: (cidx(i), C_HI // wblk)),
        pl.BlockSpec((1, wblk), lambda i: (0, 0)),
        pl.BlockSpec((1, 1, 1, HGRN_HEADS, HGRN_DK, HGRN_DK), s0_map),
        pl.BlockSpec((CH, CH), lambda i: (0, 0)),
    ]
    args = [z, z, z, lb_l[d:d + 1], state_hgrn, lev]
    aliases = {}
    if prev is not None:
        in_specs.append(pl.BlockSpec(memory_space=pl.ANY))
        args.append(prev)
        aliases = {6: 1}
    o, sfin = pl.pallas_call(
        functools.partial(_hgrn_kernel, rev=rev),
        grid=(NCH,),
        in_specs=in_specs,
        out_specs=[
            pl.BlockSpec((CH, wblk), lambda i: (cidx(i), 0)),
            pl.BlockSpec((1, 1, 1, HGRN_HEADS, HGRN_DK, HGRN_DK), sfin_map),
        ],
        out_shape=[
            jax.ShapeDtypeStruct((TT, HGRN_W), F32),
            jax.ShapeDtypeStruct((BATCH, DEPTH, 2, HGRN_HEADS, HGRN_DK, HGRN_DK), F32),
        ],
        input_output_aliases=aliases,
        scratch_shapes=[pltpu.VMEM((HGRN_HEADS, HGRN_DK, HGRN_DK), F32)],
        compiler_params=_cp(("arbitrary",), 32),
        name="hgrn_bwd" if rev else "hgrn_fwd",
    )(*args)
    return o, sfin


def _rope_tables(dim, copies):
    nfreq = dim // 4
    inv_freq = ROPE_THETA ** (-np.arange(nfreq, dtype=np.float64) / nfreq)
    t = np.arange(DEC_SEQ)
    pos_row = (t // GRID_W).astype(np.float64)
    pos_col = (t % GRID_W).astype(np.float64)
    lane = np.arange(dim)
    use_col = (lane // (dim // 2)) == 1
    fidx = lane % nfreq
    first = (lane % (dim // 2)) < nfreq
    pos = np.where(use_col[None, :], pos_col[:, None], pos_row[:, None])
    ang = pos * inv_freq[fidx][None, :]
    cos = np.cos(ang)
    sin = np.where(first[None, :], -np.sin(ang), np.sin(ang))
    cos = np.tile(cos, (1, copies)).astype(np.float32)
    sin = np.tile(sin, (1, copies)).astype(np.float32)
    return jnp.asarray(cos), jnp.asarray(sin)


def _rope(x, cos, sin, dim):
    nfreq = dim // 4
    lane = lax.broadcasted_iota(jnp.int32, x.shape, 1)
    first = (lane % (dim // 2)) < nfreq
    partner = jnp.where(first, pltpu.roll(x, LANES - nfreq, 1), pltpu.roll(x, nfreq, 1))
    return x * cos + partner * sin


def _diff_kernel(*refs, latent, lam_init, tq, hpb):
    if latent:
        (q_ref, k_ref, v_ref, nq_ref, nk_ref, sub_ref, lam_ref, cos_ref, sin_ref, ck_ref, cv_ref,
         out_ref, k_scr, v_scr) = refs
    else:
        q_ref, k_ref, v_ref, nq_ref, nk_ref, sub_ref, lam_ref = refs[:7]
        out_ref, ckout_ref, cvout_ref, k_scr, v_scr = refs[-5:]
    n = q_ref.shape[0]
    seg64 = _seg_matrix(DIFF_DK)
    seg128 = _seg_matrix(LANES)
    lp = lam_ref[...]
    lam = (jnp.exp(jnp.sum(lp[0:1] * lp[1:2], axis=-1, keepdims=True))
           - jnp.exp(jnp.sum(lp[2:3] * lp[3:4], axis=-1, keepdims=True)) + lam_init)
    lane = lax.broadcasted_iota(jnp.int32, (tq, LANES), 1)
    scale = DIFF_DK ** -0.5

    for hh in range(hpb):
        hs = slice(hh * LANES, (hh + 1) * LANES)
        qn = _rmsnorm_seg(q_ref[:, hs], nq_ref[...], seg64, DIFF_DK)
        kn = _rmsnorm_seg(k_ref[:, hs], nk_ref[...], seg64, DIFF_DK)
        v = v_ref[:, hs]
        if latent:
            qn = _rope(qn, cos_ref[...], sin_ref[...], DIFF_DK)
            kn = _rope(kn, cos_ref[...], sin_ref[...], DIFF_DK)
            k_scr[hh, n:, :] = ck_ref[0, 0, :, hs].astype(BF16)
            v_scr[hh, n:, :] = cv_ref[0, 0, :, hs].astype(BF16)
        else:
            ckout_ref[0, 0, :, hs] = kn
            cvout_ref[0, 0, :, hs] = v
        k_scr[hh, 0:n, :] = kn.astype(BF16)
        v_scr[hh, 0:n, :] = v.astype(BF16)
        kall = k_scr[hh]
        vall = v_scr[hh]
        for blk in range(n // tq):
            qb = qn[blk * tq:(blk + 1) * tq, :] * scale
            q1 = jnp.where(lane < DIFF_DK, qb, 0.0).astype(BF16)
            q2 = jnp.where(lane >= DIFF_DK, qb, 0.0).astype(BF16)
            s1 = _dot_nt(q1, kall)
            s2 = _dot_nt(q2, kall)
            e1 = jnp.exp(s1 - jnp.max(s1, axis=-1, keepdims=True))
            e2 = jnp.exp(s2 - jnp.max(s2, axis=-1, keepdims=True))
            r1 = 1.0 / jnp.sum(e1, axis=-1, keepdims=True)
            r2 = lam / jnp.sum(e2, axis=-1, keepdims=True)
            p = e1 * r1 - e2 * r2
            o = _dot(p.astype(BF16), vall)
            on = _rmsnorm_seg(o, sub_ref[...], seg128, LANES) * (1.0 - lam_init)
            out_ref[blk * tq:(blk + 1) * tq, hs] = on.astype(out_ref.dtype)


def _diff_attention(z, nq, nk, sub, lam_p, cache_k, cache_v, layer, lam_init, latent, prev=None):
    hw = 2 * DIFF_DK
    if latent:
        n, nseq, row0, skv, hpb = DEC_SEQ, DEC_BATCH, TP // DEC_SEQ, DEC_SEQ + PAST_LEN, 1
    else:
        n, nseq, row0, skv, hpb = SEQ, BATCH, 0, SEQ, 4
    bw = hpb * hw
    tq = 256
    zspec = lambda c0: pl.BlockSpec((n, bw), lambda s, h: (row0 + s, c0 // bw + h))
    vec = pl.BlockSpec((1, hw), lambda s, h: (0, 0))
    in_specs = [zspec(C_DQ), zspec(C_DK), zspec(C_DV), vec, vec, vec,
                pl.BlockSpec((4, DIFF_DK), lambda s, h: (0, 0))]
    args = [z, z, z, nq, nk, sub, lam_p]
    out_specs = [pl.BlockSpec((n, bw), lambda s, h: (s, h))]
    out_shape = [jax.ShapeDtypeStruct((nseq * n, DIFF_HEADS * hw), BF16)]
    aliases = {}
    if latent:
        cos, sin = _rope_tables(DIFF_DK, 2)
        tab = pl.BlockSpec((DEC_SEQ, hw), lambda s, h: (0, 0))
        cspec = pl.BlockSpec((1, 1, PAST_LEN, bw), lambda s, h: (s, layer, 0, h))
        in_specs += [tab, tab, cspec, cspec]
        args += [cos, sin, cache_k, cache_v]
    else:
        cblk = pl.BlockSpec((1, 1, n, bw), lambda s, h: (s, layer, 0, h))
        out_specs += [cblk, cblk]
        out_shape += [jax.ShapeDtypeStruct((BATCH, DEPTH, SEQ, DIFF_HEADS * hw), F32)] * 2
        if prev is not None:
            in_specs += [pl.BlockSpec(memory_space=pl.ANY)] * 2
            args += list(prev)
            aliases = {7: 1, 8: 2}
    return pl.pallas_call(
        functools.partial(_diff_kernel, latent=latent, lam_init=lam_init, tq=tq, hpb=hpb),
        grid=(nseq, DIFF_HEADS // hpb),
        in_specs=in_specs,
        out_specs=out_specs,
        out_shape=out_shape,
        input_output_aliases=aliases,
        scratch_shapes=[pltpu.VMEM((hpb, skv, hw), BF16), pltpu.VMEM((hpb, skv, hw), BF16)],
        compiler_params=_cp(("parallel", "parallel"), 48),
        name="diff_latent" if latent else "diff_context",
    )(*args)


def _gqa_kernel(*refs, latent, tq):
    if latent:
        (q_ref, k_ref, v_ref, nq_ref, nk_ref, cos_ref, sin_ref, ck_ref, cv_ref,
         out_ref, k_scr, v_scr) = refs
    else:
        q_ref, k_ref, v_ref, nq_ref, nk_ref = refs[:5]
        out_ref, ckout_ref, cvout_ref, k_scr, v_scr = refs[-5:]
    n = q_ref.shape[0]
    seg128 = _seg_matrix(LANES)
    kn = _rmsnorm_seg(k_ref[...], nk_ref[...], seg128, GQA_DH)
    v = v_ref[...]
    if latent:
        kn = _rope(kn, cos_ref[...], sin_ref[...], GQA_DH)
        k_scr[n:, :] = ck_ref[0, 0].astype(BF16)
        v_scr[n:, :] = cv_ref[0, 0].astype(BF16)
    else:
        ckout_ref[0, 0] = kn
        cvout_ref[0, 0] = v
    k_scr[0:n, :] = kn.astype(BF16)
    v_scr[0:n, :] = v.astype(BF16)
    kall = k_scr[...]
    vall = v_scr[...]
    scale = GQA_DH ** -0.5
    for r in range(GQA_REP):
        sl = slice(r * GQA_DH, (r + 1) * GQA_DH)
        qn = _rmsnorm_seg(q_ref[:, sl], nq_ref[...], seg128, GQA_DH)
        if latent:
            qn = _rope(qn, cos_ref[...], sin_ref[...], GQA_DH)
        qb16 = qn.astype(BF16)
        for blk in range(n // tq):
            s = _dot_nt(qb16[blk * tq:(blk + 1) * tq, :], kall) * scale
            e = jnp.exp(s - jnp.max(s, axis=-1, keepdims=True))
            p = e * (1.0 / jnp.sum(e, axis=-1, keepdims=True))
            out_ref[blk * tq:(blk + 1) * tq, sl] = _dot(p.astype(BF16), vall).astype(out_ref.dtype)


def _gqa_attention(z, nq, nk, cache_k, cache_v, layer, latent, prev=None):
    qw = GQA_REP * GQA_DH
    aliases = {}
    if latent:
        n, nseq, row0, skv = DEC_SEQ, DEC_BATCH, TP // DEC_SEQ, DEC_SEQ + PAST_LEN
    else:
        n, nseq, row0, skv = SEQ, BATCH, 0, SEQ
    tq = 256
    vec = pl.BlockSpec((1, GQA_DH), lambda s, g: (0, 0))
    in_specs = [
        pl.BlockSpec((n, qw), lambda s, g: (row0 + s, C_GQ // qw + g)),
        pl.BlockSpec((n, GQA_DH), lambda s, g: (row0 + s, C_GK // GQA_DH + g)),
        pl.BlockSpec((n, GQA_DH), lambda s, g: (row0 + s, C_GV // GQA_DH + g)),
        vec, vec]
    args = [z, z, z, nq, nk]
    out_specs = [pl.BlockSpec((n, qw), lambda s, g: (s, g))]
    out_shape = [jax.ShapeDtypeStruct((nseq * n, GQA_HEADS * GQA_DH), BF16)]
    if latent:
        cos, sin = _rope_tables(GQA_DH, 1)
        tab = pl.BlockSpec((DEC_SEQ, GQA_DH), lambda s, g: (0, 0))
        cspec = pl.BlockSpec((1, 1, PAST_LEN, GQA_DH), lambda s, g: (s, layer, 0, g))
        in_specs += [tab, tab, cspec, cspec]
        args += [cos, sin, cache_k, cache_v]
    else:
        cblk = pl.BlockSpec((1, 1, n, GQA_DH), lambda s, g: (s, layer, 0, g))
        out_specs += [cblk, cblk]
        out_shape += [jax.ShapeDtypeStruct((BATCH, DEPTH, SEQ, GQA_KV_HEADS * GQA_DH), F32)] * 2
        if prev is not None:
            in_specs += [pl.BlockSpec(memory_space=pl.ANY)] * 2
            args += list(prev)
            aliases = {5: 1, 6: 2}
    return pl.pallas_call(
        functools.partial(_gqa_kernel, latent=latent, tq=tq),
        grid=(nseq, GQA_KV_HEADS),
        in_specs=in_specs,
        out_specs=out_specs,
        out_shape=out_shape,
        input_output_aliases=aliases,
        scratch_shapes=[pltpu.VMEM((skv, GQA_DH), BF16), pltpu.VMEM((skv, GQA_DH), BF16)],
        compiler_params=_cp(("parallel", "parallel"), 48),
        name="gqa_latent" if latent else "gqa_context",
    )(*args)


GATE_BLK = 512


def _merge_kernel(of_ref, ob_ref, hg_ref, hn_ref, dp_ref, ds_ref, gp_ref, gs_ref, *rest, n_prompt_blocks):
    g_refs = rest[:-3]
    wb_ref, m_ref, br_scr = rest[-3:]
    i = pl.program_id(0)
    nq = D_MODEL // GATE_BLK

    seg128 = _seg_matrix(LANES)
    for h in range(HGRN_HEADS):
        sl = slice(h * HGRN_DK, (h + 1) * HGRN_DK)
        o = of_ref[:, sl] + ob_ref[:, sl]
        g = hg_ref[:, sl]
        y = _rmsnorm_seg(o, hn_ref[...], seg128, HGRN_DK) * (g * _sigmoid(g))
        br_scr[0, :, sl] = y.astype(BF16)

    @pl.when(i < n_prompt_blocks)
    def _():
        br_scr[1] = dp_ref[...]
        br_scr[2] = gp_ref[...]

    @pl.when(i >= n_prompt_blocks)
    def _():
        br_scr[1] = ds_ref[...]
        br_scr[2] = gs_ref[...]

    for q in range(nq):
        cols = slice(q * GATE_BLK, (q + 1) * GATE_BLK)
        acc = _sigmoid(g_refs[q][...]) * _dot(br_scr[0], wb_ref[0, 0, :, cols])
        acc += _sigmoid(g_refs[nq + q][...]) * _dot(br_scr[1], wb_ref[0, 1, :, cols])
        acc += _sigmoid(g_refs[2 * nq + q][...]) * _dot(br_scr[2], wb_ref[0, 2, :, cols])
        m_ref[:, cols] = acc.astype(m_ref.dtype)


def _merge(o_f, o_b, z, hgrn_norm_row, diff_p, diff_s, gqa_p, gqa_s, w_branch, layer):
    tm = 256
    npb = TP // tm
    nsb = TS // tm
    nq = D_MODEL // GATE_BLK
    row = lambda i: (i, 0)
    prow = lambda i: (jnp.minimum(i, npb - 1), 0)
    srow = lambda i: (jnp.clip(i - npb, 0, nsb - 1), 0)
    gates = [pl.BlockSpec((tm, GATE_BLK), lambda i, b=(C_GATE + c * D_MODEL) // GATE_BLK + q: (i, b))
             for c in range(N_BRANCH) for q in range(nq)]
    return pl.pallas_call(
        functools.partial(_merge_kernel, n_prompt_blocks=npb),
        grid=(TT // tm,),
        in_specs=[
            pl.BlockSpec((tm, HGRN_W), row),
            pl.BlockSpec((tm, HGRN_W), row),
            pl.BlockSpec((tm, HGRN_W), lambda i: (i, C_HG // HGRN_W)),
            pl.BlockSpec((1, HGRN_DK), lambda i: (0, 0)),
            pl.BlockSpec((tm, BRANCH_W), prow),
            pl.BlockSpec((tm, BRANCH_W), srow),
            pl.BlockSpec((tm, BRANCH_W), prow),
            pl.BlockSpec((tm, BRANCH_W), srow),
            *gates,
            pl.BlockSpec((1, N_BRANCH, BRANCH_W, D_MODEL), lambda i: (layer, 0, 0, 0)),
        ],
        out_specs=pl.BlockSpec((tm, D_MODEL), row),
        out_shape=jax.ShapeDtypeStruct((TT, D_MODEL), BF16),
        scratch_shapes=[pltpu.VMEM((N_BRANCH, tm, BRANCH_W), BF16)],
        compiler_params=_cp(("parallel",), 56),
        name="branch_merge",
    )(o_f, o_b, z, hgrn_norm_row, diff_p, diff_s, gqa_p, gqa_s, *([z] * (N_BRANCH * nq)), w_branch)


def _out_proj_kernel(m_ref, w_ref, x_ref, mod_ref, y_ref, *, gate_idx):
    y_ref[...] = x_ref[...] + mod_ref[0, 0, gate_idx:gate_idx + 1, :] * _dot(m_ref[...], w_ref[0])


def _out_proj(m, w_out, x, mods4, layer, gate_idx):
    tm, tn = 512, D_MODEL
    return pl.pallas_call(
        functools.partial(_out_proj_kernel, gate_idx=gate_idx),
        grid=(TT // tm, D_MODEL // tn),
        in_specs=[
            pl.BlockSpec((tm, D_MODEL), lambda i, j: (i, 0)),
            pl.BlockSpec((1, D_MODEL, tn), lambda i, j: (layer, 0, j)),
            pl.BlockSpec((tm, tn), lambda i, j: (i, j)),
            pl.BlockSpec((1, 1, 6, tn), lambda i, j: (layer, _mod_row(i, tm), 0, j)),
        ],
        out_specs=pl.BlockSpec((tm, tn), lambda i, j: (i, j)),
        out_shape=jax.ShapeDtypeStruct((TT, D_MODEL), F32),
        compiler_params=_cp(("parallel", "parallel"), 48),
        name="out_proj",
    )(m, w_out, x, mods4)


_STAIR = [PEER_TOPK // (r + 1) for r in range(8)]


def _extract_top(s, k, want_rank):
    vals = []
    rank = jnp.full(s.shape, float(k), F32) if want_rank else None
    for r in range(k):
        m = jnp.max(s, axis=0, keepdims=True)
        hit = s >= m
        vals.append(m)
        if want_rank:
            rank = jnp.where(hit, float(r), rank)
        s = jnp.where(hit, NEG_INF, s)
    return vals, rank


def _route_kernel(q_ref, keys_ref, r1_ref, r2_ref, *, tm):
    kb = keys_ref[0].astype(BF16)
    sub8 = lax.broadcasted_iota(jnp.int32, (8, LANES), 0)
    for g in range(tm // LANES):
        rows = slice(g * LANES, (g + 1) * LANES)
        cols = slice(g * LANES, (g + 1) * LANES)
        s1 = _dot_nt(kb[0], q_ref[rows, 0:N_KEYS])
        s2 = _dot_nt(kb[1], q_ref[rows, N_KEYS:2 * N_KEYS])
        v1, _ = _extract_top(s1, PEER_TOPK, False)
        v2, rank2 = _extract_top(s2, PEER_TOPK, True)
        sv2 = jnp.concatenate(v2, axis=0)
        sv1_hi = jnp.concatenate(v1[8:], axis=0)
        cand = [v1[0] + sv2, v1[1] + sv2[0:8]]
        for r in range(2, 8):
            cand.append(jnp.where(sub8 < _STAIR[r], v1[r] + sv2[0:8], NEG_INF))
        cand.append(sv1_hi + v2[0])
        cand = jnp.concatenate(cand, axis=0)
        tau = _extract_top(cand, PEER_TOPK, False)[0][-1]
        cmax = v1[0] + v2[0]
        zsum = jnp.sum(jnp.where(cand >= tau, jnp.exp(cand - cmax), 0.0), axis=0, keepdims=True)
        rz = 1.0 / zsum
        cnt = jnp.zeros((N_KEYS, LANES), F32)
        for r in range(PEER_TOPK):
            cnt_r = jnp.sum(jnp.where(v1[r] + sv2 >= tau, 1.0, 0.0), axis=0, keepdims=True)
            cnt = jnp.where(s1 == v1[r], cnt_r, cnt)
        r1_ref[0, 0, :, cols] = cnt
        r1_ref[0, 1, :, cols] = jnp.exp(s1 - v1[0]) * rz
        r2_ref[0, 0, :, cols] = rank2
        r2_ref[0, 1, :, cols] = jnp.exp(s2 - v2[0])


def _peer_route(qp, peer_keys, layer):
    tm = 512
    blk = pl.BlockSpec((1, 2, N_KEYS, tm), lambda i, h: (h, 0, 0, i))
    return pl.pallas_call(
        functools.partial(_route_kernel, tm=tm),
        grid=(TT // tm, PEER_HEADS),
        in_specs=[
            pl.BlockSpec((tm, 2 * N_KEYS), lambda i, h: (i, h)),
            pl.BlockSpec((1, 2, N_KEYS, N_KEYS), lambda i, h: (layer, 0, 0, 0)),
        ],
        out_specs=[blk, blk],
        out_shape=[jax.ShapeDtypeStruct((PEER_HEADS, 2, N_KEYS, TT), F32)] * 2,
        compiler_params=_cp(("parallel", "parallel"), 32),
        name="peer_route",
    )(qp, peer_keys)


def _expert_kernel(h_ref, u_ref, v_ref, r1_ref, r2_ref, x_ref, mod_ref, y_ref,
                   r2_scr, at_scr, g_scr, acc_scr, *, tm, ce, n_parts, gate_idx):
    j = pl.program_id(1)

    @pl.when(j == 0)
    def _():
        acc_scr[...] = jnp.zeros_like(acc_scr)
        r2_scr[...] = r2_ref[...].astype(BF16)

    hb = h_ref[...]
    pe = ce // n_parts
    packed = (N_KEYS // 16, 16, LANES)
    contrib = None
    for p in range(n_parts):
        at_scr[p] = _dot_nt(u_ref[0, p * pe:(p + 1) * pe, :], hb)
        for ib in range(pe // N_KEYS):
            il = p * (pe // N_KEYS) + ib
            rows = slice(ib * N_KEYS, (ib + 1) * N_KEYS)
            for g in range(tm // LANES):
                cols = slice(g * LANES, (g + 1) * LANES)
                w = jnp.zeros(packed, BF16)
                for h in range(PEER_HEADS):
                    rk = r2_scr[h, 0, :, cols].reshape(packed)
                    e2 = r2_scr[h, 1, :, cols].reshape(packed)
                    cnt = jnp.broadcast_to(r1_ref[h, 0, il:il + 1, cols], (16, LANES)).astype(BF16)[None]
                    e1 = jnp.broadcast_to(r1_ref[h, 1, il:il + 1, cols], (16, LANES)).astype(BF16)[None]
                    w = w + jnp.where(rk < cnt, e2, jnp.zeros_like(e2)) * e1
                a = at_scr[p, rows, cols]
                act = 0.5 * a * (1.0 + lax.erf(a * (2.0 ** -0.5)))
                gt = act.astype(BF16).reshape(packed) * w
                g_scr[p, rows, cols] = gt.reshape(N_KEYS, LANES)
        d = _dot_tn(g_scr[p], v_ref[0, p * pe:(p + 1) * pe, :])
        contrib = d if contrib is None else contrib + d
    acc_scr[...] += contrib

    @pl.when(j == pl.num_programs(1) - 1)
    def _():
        y_ref[...] = x_ref[...] + mod_ref[0, 0, gate_idx:gate_idx + 1, :] * acc_scr[...]


def _peer_experts(h2, u, v, r1, r2, x, mods4, layer, gate_idx):
    tm, ce, n_parts = 512, 1024, 2
    n_i1 = ce // N_KEYS
    return pl.pallas_call(
        functools.partial(_expert_kernel, tm=tm, ce=ce, n_parts=n_parts, gate_idx=gate_idx),
        grid=(TT // tm, N_EXPERTS // ce),
        in_specs=[
            pl.BlockSpec((tm, D_MODEL), lambda i, j: (i, 0)),
            pl.BlockSpec((1, ce, D_MODEL), lambda i, j: (layer, j, 0)),
            pl.BlockSpec((1, ce, D_MODEL), lambda i, j: (layer, j, 0)),
            pl.BlockSpec((PEER_HEADS, 2, n_i1, tm), lambda i, j: (0, 0, j, i)),
            pl.BlockSpec((PEER_HEADS, 2, N_KEYS, tm), lambda i, j: (0, 0, 0, i)),
            pl.BlockSpec((tm, D_MODEL), lambda i, j: (i, 0)),
            pl.BlockSpec((1, 1, 6, D_MODEL), lambda i, j: (layer, _mod_row(i, tm), 0, 0)),
        ],
        out_specs=pl.BlockSpec((tm, D_MODEL), lambda i, j: (i, 0)),
        out_shape=jax.ShapeDtypeStruct((TT, D_MODEL), F32),
        scratch_shapes=[
            pltpu.VMEM((PEER_HEADS, 2, N_KEYS, tm), BF16),
            pltpu.VMEM((n_parts, ce // n_parts, tm), F32),
            pltpu.VMEM((n_parts, ce // n_parts, tm), BF16),
            pltpu.VMEM((tm, D_MODEL), F32),
        ],
        compiler_params=_cp(("parallel", "arbitrary"), 56),
        name="peer_experts",
    )(h2, u, v, r1, r2, x, mods4)


def kernel(x_prompt, x_sample, c, cache_diff_k, cache_diff_v, cache_gqa_k, cache_gqa_v, state_hgrn, c_ctx,
           mod_w, mod_b, norm_mix, norm_ffn, w_in, hgrn_lb, hgrn_norm, diff_qk_norm, diff_lambda, diff_subln,
           gqa_qk_norm, w_branch, w_out, peer_wq, peer_keys, peer_u, peer_v):
    x = jnp.concatenate([x_prompt.reshape(TP, D_MODEL), x_sample.reshape(TS, D_MODEL)], axis=0)
    cond8 = jnp.concatenate([c_ctx[None, :], c, jnp.zeros((8 - 1 - DEC_BATCH, D_MODEL), F32)], axis=0)
    mods4 = _modulation(cond8, mod_w, mod_b).reshape(DEPTH, 8, 6, D_MODEL)

    lb_all = jnp.cumsum(jax.nn.softmax(hgrn_lb.astype(F32), axis=1), axis=1)
    lb_all = lb_all - lb_all[:, :1]

    w_in_b = w_in
    peer_wq_b = peer_wq.astype(BF16)
    w_branch_b = w_branch.astype(BF16)
    w_out_b = w_out.astype(BF16)
    peer_u_b = peer_u.astype(BF16)
    peer_v_b = peer_v.astype(BF16)

    cdk = cache_diff_k.reshape(DEC_BATCH, DEPTH, PAST_LEN, DIFF_HEADS * 2 * DIFF_DK)
    cdv = cache_diff_v.reshape(DEC_BATCH, DEPTH, PAST_LEN, DIFF_HEADS * 2 * DIFF_DK)
    cgk = cache_gqa_k.reshape(DEC_BATCH, DEPTH, PAST_LEN, GQA_KV_HEADS * GQA_DH)
    cgv = cache_gqa_v.reshape(DEC_BATCH, DEPTH, PAST_LEN, GQA_KV_HEADS * GQA_DH)

    new_state = None
    diff_kv = None
    gqa_kv = None
    for l in range(DEPTH):
        lam_init = 0.8 - 0.6 * math.exp(-0.3 * l)
        z = _norm_matmul(x, mods4, l, norm_mix[l][None, :], w_in_b, sh_idx=0, sc_idx=1,
                         tm=1024, tn=512, out_dtype=F32, emit_h=False, name="in_proj")[0]

        o_f, new_state = _hgrn(z, lb_all[:, l], state_hgrn, l, rev=False, prev=new_state)
        o_b, new_state = _hgrn(z, lb_all[:, l], state_hgrn, l, rev=True, prev=new_state)

        nq_d = jnp.tile(diff_qk_norm[l, 0], 2)[None, :]
        nk_d = jnp.tile(diff_qk_norm[l, 1], 2)[None, :]
        sub = diff_subln[l][None, :]
        diff_p, *diff_kv = _diff_attention(z, nq_d, nk_d, sub, diff_lambda[l], None, None, l,
                                           lam_init, latent=False, prev=diff_kv)
        diff_s = _diff_attention(z, nq_d, nk_d, sub, diff_lambda[l], cdk, cdv, l, lam_init, latent=True)[0]

        nq_g = gqa_qk_norm[l, 0][None, :]
        nk_g = gqa_qk_norm[l, 1][None, :]
        gqa_p, *gqa_kv = _gqa_attention(z, nq_g, nk_g, None, None, l, latent=False, prev=gqa_kv)
        gqa_s = _gqa_attention(z, nq_g, nk_g, cgk, cgv, l, latent=True)[0]

        m = _merge(o_f, o_b, z, hgrn_norm[l][None, :], diff_p, diff_s, gqa_p, gqa_s, w_branch_b, l)
        x = _out_proj(m, w_out_b, x, mods4, l, gate_idx=2)

        qp, h2 = _norm_matmul(x, mods4, l, norm_ffn[l][None, :], peer_wq_b, sh_idx=3, sc_idx=4,
                              tm=512, tn=D_MODEL, out_dtype=BF16, emit_h=True, name="peer_query")
        r1, r2 = _peer_route(qp, peer_keys, l)
        x = _peer_experts(h2, peer_u_b, peer_v_b, r1, r2, x, mods4, l, gate_idx=5)

    return (x[:TP].reshape(BATCH, SEQ, D_MODEL), x[TP:].reshape(DEC_BATCH, DEC_SEQ, D_MODEL),
            diff_kv[0].reshape(BATCH, DEPTH, SEQ, DIFF_HEADS, 2 * DIFF_DK),
            diff_kv[1].reshape(BATCH, DEPTH, SEQ, DIFF_HEADS, 2 * DIFF_DK),
            gqa_kv[0].reshape(BATCH, DEPTH, SEQ, GQA_KV_HEADS, GQA_DH),
            gqa_kv[1].reshape(BATCH, DEPTH, SEQ, GQA_KV_HEADS, GQA_DH),
            new_state)
```

```python
import functools
import math

import numpy as np
import jax
import jax.numpy as jnp
from jax import lax
from jax.experimental import pallas as pl
from jax.experimental.pallas import tpu as pltpu

F32 = jnp.float32
BF16 = jnp.bfloat16

D_MODEL = 2048
BATCH = 16
SEQ = 256
DEPTH = 2
DEC_BATCH = 2
DEC_SEQ = 1024
PAST_LEN = 256
GRID_W = 64
ROPE_THETA = 10000.0
NORM_EPS = 1e-6

HGRN_HEADS = 8
HGRN_DK = 128
HGRN_W = 1024
DIFF_HEADS = 8
DIFF_DK = 64
GQA_HEADS = 8
GQA_KV_HEADS = 2
GQA_DH = 128
GQA_REP = 4
N_BRANCH = 3
BRANCH_W = 1024
PEER_HEADS = 8
N_KEYS = 128
N_EXPERTS = N_KEYS * N_KEYS
PEER_TOPK = 16

TP = BATCH * SEQ
TS = DEC_BATCH * DEC_SEQ
TT = TP + TS

C_HQ, C_HF0, C_HF1, C_HI, C_HG = 0, 1024, 2048, 3072, 4096
C_DQ, C_DK, C_DV = 5120, 6144, 7168
C_GQ, C_GK, C_GV = 8192, 9216, 9472
C_GATE = 9728

LANES = 128
CH = 128
NCH = TT // CH
NPC = TP // CH
CH_PER_PROMPT = SEQ // CH
CH_PER_SAMPLE = DEC_SEQ // CH
N_LEVELS = 7
NEG_INF = float("-inf")
MIN_NORMAL = 1.1754944e-38

_NT = (((1,), (1,)), ((), ()))
_TN = (((0,), (0,)), ((), ()))


def _cp(sem, vmem_mb):
    return pltpu.CompilerParams(dimension_semantics=sem, vmem_limit_bytes=vmem_mb * 1024 * 1024)


def _dot(a, b):
    return jnp.dot(a, b, preferred_element_type=F32)


def _dot_nt(a, b):
    return lax.dot_general(a, b, _NT, preferred_element_type=F32)


def _dot_tn(a, b):
    return lax.dot_general(a, b, _TN, preferred_element_type=F32)


def _sigmoid(x):
    return 0.5 * jnp.tanh(0.5 * x) + 0.5


def _split_bf16(x):
    hi = x.astype(BF16)
    lo = (x - hi.astype(F32)).astype(BF16)
    return hi, lo


def _seg_matrix(seg):
    r = lax.broadcasted_iota(jnp.int32, (LANES, LANES), 0) // seg
    c = lax.broadcasted_iota(jnp.int32, (LANES, LANES), 1) // seg
    return (r == c).astype(BF16)


def _rmsnorm_seg(x, gain_row, seg_mat, seg):
    hi, lo = _split_bf16(x * x)
    ss = _dot(hi, seg_mat) + _dot(lo, seg_mat)
    return x * lax.rsqrt(ss * (1.0 / seg) + NORM_EPS) * gain_row


def _mod_row(i, tm):
    n_p = TP // tm
    per = DEC_SEQ // tm
    return jnp.where(i < n_p, 0, 1 + (i - n_p) // per)


def _mod_kernel(cond_ref, w_ref, b_ref, out_ref):
    a = cond_ref[...]
    a = a * jax.nn.sigmoid(a)
    out_ref[0] = _dot(a.astype(BF16), w_ref[0].astype(BF16)) + b_ref[0]


def _modulation(cond8, mod_w, mod_b):
    tn = 1024
    n6 = 6 * D_MODEL
    return pl.pallas_call(
        _mod_kernel,
        grid=(DEPTH, n6 // tn),
        in_specs=[
            pl.BlockSpec((8, D_MODEL), lambda l, j: (0, 0)),
            pl.BlockSpec((1, D_MODEL, tn), lambda l, j: (l, 0, j)),
            pl.BlockSpec((1, 1, tn), lambda l, j: (l, 0, j)),
        ],
        out_specs=pl.BlockSpec((1, 8, tn), lambda l, j: (l, 0, j)),
        out_shape=jax.ShapeDtypeStruct((DEPTH, 8, n6), F32),
        compiler_params=_cp(("parallel", "parallel"), 40),
        name="modulation",
    )(cond8, mod_w, mod_b.reshape(DEPTH, 1, n6))


def _norm_mm_kernel(x_ref, mod_ref, gain_ref, w_ref, *rest, sh_idx, sc_idx, emit_h):
    if emit_h:
        z_ref, h_ref, h_scr = rest
    else:
        z_ref, h_scr = rest

    @pl.when(pl.program_id(1) == 0)
    def _():
        x = x_ref[...]
        ms = jnp.mean(x * x, axis=-1, keepdims=True)
        y = x * lax.rsqrt(ms + NORM_EPS) * gain_ref[...]
        h = y * (1.0 + mod_ref[0, 0, sc_idx:sc_idx + 1, :]) + mod_ref[0, 0, sh_idx:sh_idx + 1, :]
        hb = h.astype(BF16)
        h_scr[...] = hb
        if emit_h:
            h_ref[...] = hb

    z_ref[...] = _dot(h_scr[...], w_ref[0].astype(BF16)).astype(z_ref.dtype)


def _norm_matmul(x, mods4, layer, gain, w, *, sh_idx, sc_idx, tm, tn, out_dtype, emit_h, name):
    n = w.shape[2]
    out_shape = [jax.ShapeDtypeStruct((TT, n), out_dtype)]
    out_specs = [pl.BlockSpec((tm, tn), lambda i, j: (i, j))]
    if emit_h:
        out_shape.append(jax.ShapeDtypeStruct((TT, D_MODEL), BF16))
        out_specs.append(pl.BlockSpec((tm, D_MODEL), lambda i, j: (i, 0)))
    return pl.pallas_call(
        functools.partial(_norm_mm_kernel, sh_idx=sh_idx, sc_idx=sc_idx, emit_h=emit_h),
        grid=(TT // tm, n // tn),
        in_specs=[
            pl.BlockSpec((tm, D_MODEL), lambda i, j: (i, 0)),
            pl.BlockSpec((1, 1, 6, D_MODEL), lambda i, j: (layer, _mod_row(i, tm), 0, 0)),
            pl.BlockSpec((1, D_MODEL), lambda i, j: (0, 0)),
            pl.BlockSpec((1, D_MODEL, tn), lambda i, j: (layer, 0, j)),
        ],
        out_specs=out_specs,
        out_shape=out_shape,
        scratch_shapes=[pltpu.VMEM((tm, D_MODEL), BF16)],
        compiler_params=_cp(("parallel", "arbitrary"), 48),
        name=name,
    )(x, mods4, gain, w)


def _level_table(rev):
    t = np.arange(CH)[:, None]
    s = np.arange(CH)[None, :]
    x = t ^ s
    lev = np.full((CH, CH), -1, np.int32)
    nz = x > 0
    lev[nz] = np.floor(np.log2(x[nz])).astype(np.int32)
    valid = (t < s) if rev else (t > s)
    lev = np.where(valid, lev, -1)
    lev[np.arange(CH), np.arange(CH)] = N_LEVELS
    return lev.astype(np.int32)


def _bmid(b, b3, m, rev):
    off = m if rev else m - 1
    if m >= 8:
        pieces = []
        for j in range(CH // (2 * m)):
            idx = j * 2 * m + off
            pieces.append(jnp.broadcast_to(b[idx:idx + 1, :], (2 * m, LANES)))
        return pieces[0] if len(pieces) == 1 else jnp.concatenate(pieces, axis=0)
    sub = lax.broadcasted_iota(jnp.int32, (CH // 8, 8, LANES), 1)
    out = None
    for j in range(8 // (2 * m)):
        idx = j * 2 * m + off
        piece = jnp.broadcast_to(b3[:, idx:idx + 1, :], (CH // 8, 8, LANES))
        out = piece if out is None else jnp.where(sub >= j * 2 * m, piece, out)
    return out.reshape(CH, LANES)


def _level_operand(q, kk, b, b3, m, rev, row):
    if m < 8:
        e = jnp.exp(-jnp.abs(b - _bmid(b, b3, m, rev)))
        q_side = ((row // m) % 2) == (0 if rev else 1)
        return jnp.where(q_side, q, kk) * e
    pieces = []
    for j in range(CH // (2 * m)):
        lo = slice(j * 2 * m, j * 2 * m + m)
        hi = slice(j * 2 * m + m, (j + 1) * 2 * m)
        mid = j * 2 * m + (m if rev else m - 1)
        bm = b[mid:mid + 1, :]
        if rev:
            pieces += [q[lo] * jnp.exp(b[lo] - bm), kk[hi] * jnp.exp(bm - b[hi])]
        else:
            pieces += [kk[lo] * jnp.exp(bm - b[lo]), q[hi] * jnp.exp(b[hi] - bm)]
    return jnp.concatenate(pieces, axis=0)


def _hgrn_kernel(*refs, rev):
    q_ref, f_ref, v_ref, lb_ref, s0_ref, lev_ref = refs[:6]
    o_ref, sfin_ref, st_scr = refs[-3:]
    i = pl.program_id(0)
    c = (NCH - 1 - i) if rev else i
    is_prompt = c < NPC
    cp_first = (c % CH_PER_PROMPT) == 0
    cp_last = (c % CH_PER_PROMPT) == CH_PER_PROMPT - 1
    cs = jnp.maximum(c - NPC, 0)
    cs_first = (cs % CH_PER_SAMPLE) == 0
    cs_last = (cs % CH_PER_SAMPLE) == CH_PER_SAMPLE - 1
    if rev:
        start_p, end_p, start_s = cp_last, cp_first, cs_last
    else:
        start_p, end_p, start_s = cp_first, cp_last, cs_first

    @pl.when(jnp.logical_and(is_prompt, start_p))
    def _():
        st_scr[...] = jnp.zeros_like(st_scr)

    @pl.when(jnp.logical_and(jnp.logical_not(is_prompt), start_s))
    def _():
        for h in range(HGRN_HEADS):
            st_scr[h] = s0_ref[0, 0, 0, h].T

    row = lax.broadcasted_iota(jnp.int32, (CH, CH), 0)
    col = lax.broadcasted_iota(jnp.int32, (CH, CH), 1)
    tri = ((col >= row) if rev else (col <= row)).astype(BF16)
    lev = lev_ref[...]

    for h in range(HGRN_HEADS):
        sl = slice(h * HGRN_DK, (h + 1) * HGRN_DK)
        q = q_ref[:, sl].astype(F32)
        zf = f_ref[:, sl].astype(F32)
        v = v_ref[:, sl].astype(F32)
        lb = lb_ref[:, sl]
        f = lb + (1.0 - lb) * jax.nn.sigmoid(zf)
        kk = (1.0 - lb) * jax.nn.sigmoid(-zf)
        logf = jnp.log(jnp.maximum(f, MIN_NORMAL))
        hi, lo = _split_bf16(logf)
        b = _dot(tri, hi) + _dot(tri, lo)
        b3 = b.reshape(CH // 8, 8, LANES)
        vb = v.astype(BF16)

        a = jnp.where(lev == N_LEVELS, _dot_nt(q.astype(BF16), kk.astype(BF16)), 0.0)
        for lm in range(N_LEVELS):
            x = _level_operand(q, kk, b, b3, 1 << lm, rev, row).astype(BF16)
            a = jnp.where(lev == lm, _dot_nt(x, x), a)

        b_end = b[0:1, :] if rev else b[CH - 1:CH, :]
        qd = (q * jnp.exp(b)).astype(BF16)
        kd = (kk * jnp.exp(b_end - b)).astype(BF16)
        st = st_scr[h]
        o_ref[:, sl] = _dot(a.astype(BF16), vb) + _dot_nt(qd, st.astype(BF16))
        st_scr[h] = st * jnp.exp(b_end) + _dot_tn(vb, kd)

    @pl.when(jnp.logical_and(is_prompt, end_p))
    def _():
        for h in range(HGRN_HEADS):
            sfin_ref[0, 0, 0, h] = st_scr[h].T


def _hgrn(z, lb_l, state_hgrn, layer, rev, prev=None):
    d = 1 if rev else 0
    lev = jnp.asarray(_level_table(rev))

    def cidx(i):
        return (NCH - 1 - i) if rev else i

    def s0_map(i):
        b = jnp.clip((cidx(i) - NPC) // CH_PER_SAMPLE, 0, DEC_BATCH - 1)
        return (b, layer, d, 0, 0, 0)

    def sfin_map(i):
        return (jnp.minimum(cidx(i) // CH_PER_PROMPT, BATCH - 1), layer, d, 0, 0, 0)

    wblk = HGRN_W
    in_specs = [
        pl.BlockSpec((CH, wblk), lambda i: (cidx(i), C_HQ // wblk)),
        pl.BlockSpec((CH, wblk), lambda i: (cidx(i), (C_HF1 if rev else C_HF0) // wblk)),
        pl.BlockSpec((CH, wblk), lambda i: (cidx(i), C_HI // wblk)),
        pl.BlockSpec((1, wblk), lambda i: (0, 0)),
        pl.BlockSpec((1, 1, 1, HGRN_HEADS, HGRN_DK, HGRN_DK), s0_map),
        pl.BlockSpec((CH, CH), lambda i: (0, 0)),
    ]
    args = [z, z, z, lb_l[d:d + 1], state_hgrn, lev]
    aliases = {}
    if prev is not None:
        in_specs.append(pl.BlockSpec(memory_space=pl.ANY))
        args.append(prev)
        aliases = {6: 1}
    o, sfin = pl.pallas_call(
        functools.partial(_hgrn_kernel, rev=rev),
        grid=(NCH,),
        in_specs=in_specs,
        out_specs=[
            pl.BlockSpec((CH, wblk), lambda i: (cidx(i), 0)),
            pl.BlockSpec((1, 1, 1, HGRN_HEADS, HGRN_DK, HGRN_DK), sfin_map),
        ],
        out_shape=[
            jax.ShapeDtypeStruct((TT, HGRN_W), F32),
            jax.ShapeDtypeStruct((BATCH, DEPTH, 2, HGRN_HEADS, HGRN_DK, HGRN_DK), F32),
        ],
        input_output_aliases=aliases,
        scratch_shapes=[pltpu.VMEM((HGRN_HEADS, HGRN_DK, HGRN_DK), F32)],
        compiler_params=_cp(("arbitrary",), 32),
        name="hgrn_bwd" if rev else "hgrn_fwd",
    )(*args)
    return o, sfin


def _rope_tables(dim, copies):
    nfreq = dim // 4
    inv_freq = ROPE_THETA ** (-np.arange(nfreq, dtype=np.float64) / nfreq)
    t = np.arange(DEC_SEQ)
    pos_row = (t // GRID_W).astype(np.float64)
    pos_col = (t % GRID_W).astype(np.float64)
    lane = np.arange(dim)
    use_col = (lane // (dim // 2)) == 1
    fidx = lane % nfreq
    first = (lane % (dim // 2)) < nfreq
    pos = np.where(use_col[None, :], pos_col[:, None], pos_row[:, None])
    ang = pos * inv_freq[fidx][None, :]
    cos = np.cos(ang)
    sin = np.where(first[None, :], -np.sin(ang), np.sin(ang))
    cos = np.tile(cos, (1, copies)).astype(np.float32)
    sin = np.tile(sin, (1, copies)).astype(np.float32)
    return jnp.asarray(cos), jnp.asarray(sin)


def _rope(x, cos, sin, dim):
    nfreq = dim // 4
    lane = lax.broadcasted_iota(jnp.int32, x.shape, 1)
    first = (lane % (dim // 2)) < nfreq
    partner = jnp.where(first, pltpu.roll(x, LANES - nfreq, 1), pltpu.roll(x, nfreq, 1))
    return x * cos + partner * sin


def _diff_kernel(*refs, latent, lam_init, tq, hpb):
    if latent:
        (q_ref, k_ref, v_ref, nq_ref, nk_ref, sub_ref, lam_ref, cos_ref, sin_ref, ck_ref, cv_ref,
         out_ref, k_scr, v_scr) = refs
    else:
        q_ref, k_ref, v_ref, nq_ref, nk_ref, sub_ref, lam_ref = refs[:7]
        out_ref, ckout_ref, cvout_ref, k_scr, v_scr = refs[-5:]
    n = q_ref.shape[0]
    seg64 = _seg_matrix(DIFF_DK)
    seg128 = _seg_matrix(LANES)
    lp = lam_ref[...]
    lam = (jnp.exp(jnp.sum(lp[0:1] * lp[1:2], axis=-1, keepdims=True))
           - jnp.exp(jnp.sum(lp[2:3] * lp[3:4], axis=-1, keepdims=True)) + lam_init)
    lane = lax.broadcasted_iota(jnp.int32, (tq, LANES), 1)
    scale = DIFF_DK ** -0.5

    for hh in range(hpb):
        hs = slice(hh * LANES, (hh + 1) * LANES)
        qn = _rmsnorm_seg(q_ref[:, hs].astype(F32), nq_ref[...], seg64, DIFF_DK)
        kn = _rmsnorm_seg(k_ref[:, hs].astype(F32), nk_ref[...], seg64, DIFF_DK)
        v = v_ref[:, hs].astype(F32)
        if latent:
            qn = _rope(qn, cos_ref[...], sin_ref[...], DIFF_DK)
            kn = _rope(kn, cos_ref[...], sin_ref[...], DIFF_DK)
            k_scr[hh, n:, :] = ck_ref[0, 0, :, hs].astype(BF16)
            v_scr[hh, n:, :] = cv_ref[0, 0, :, hs].astype(BF16)
        else:
            ckout_ref[0, 0, :, hs] = kn
            cvout_ref[0, 0, :, hs] = v
        k_scr[hh, 0:n, :] = kn.astype(BF16)
        v_scr[hh, 0:n, :] = v.astype(BF16)
        kall = k_scr[hh]
        vall = v_scr[hh]
        for blk in range(n // tq):
            qb = qn[blk * tq:(blk + 1) * tq, :] * scale
            q1 = jnp.where(lane < DIFF_DK, qb, 0.0).astype(BF16)
            q2 = jnp.where(lane >= DIFF_DK, qb, 0.0).astype(BF16)
            s1 = _dot_nt(q1, kall)
            s2 = _dot_nt(q2, kall)
            e1 = jnp.exp(s1 - jnp.max(s1, axis=-1, keepdims=True))
            e2 = jnp.exp(s2 - jnp.max(s2, axis=-1, keepdims=True))
            r1 = 1.0 / jnp.sum(e1, axis=-1, keepdims=True)
            r2 = lam / jnp.sum(e2, axis=-1, keepdims=True)
            p = e1 * r1 - e2 * r2
            o = _dot(p.astype(BF16), vall)
            on = _rmsnorm_seg(o, sub_ref[...], seg128, LANES) * (1.0 - lam_init)
            out_ref[blk * tq:(blk + 1) * tq, hs] = on.astype(out_ref.dtype)


def _diff_attention(z, nq, nk, sub, lam_p, cache_k, cache_v, layer, lam_init, latent, prev=None):
    hw = 2 * DIFF_DK
    if latent:
        n, nseq, row0, skv, hpb = DEC_SEQ, DEC_BATCH, TP // DEC_SEQ, DEC_SEQ + PAST_LEN, 1
    else:
        n, nseq, row0, skv, hpb = SEQ, BATCH, 0, SEQ, 4
    bw = hpb * hw
    tq = 256
    zspec = lambda c0: pl.BlockSpec((n, bw), lambda s, h: (row0 + s, c0 // bw + h))
    vec = pl.BlockSpec((1, hw), lambda s, h: (0, 0))
    in_specs = [zspec(C_DQ), zspec(C_DK), zspec(C_DV), vec, vec, vec,
                pl.BlockSpec((4, DIFF_DK), lambda s, h: (0, 0))]
    args = [z, z, z, nq, nk, sub, lam_p]
    out_specs = [pl.BlockSpec((n, bw), lambda s, h: (s, h))]
    out_shape = [jax.ShapeDtypeStruct((nseq * n, DIFF_HEADS * hw), BF16)]
    aliases = {}
    if latent:
        cos, sin = _rope_tables(DIFF_DK, 2)
        tab = pl.BlockSpec((DEC_SEQ, hw), lambda s, h: (0, 0))
        cspec = pl.BlockSpec((1, 1, PAST_LEN, bw), lambda s, h: (s, layer, 0, h))
        in_specs += [tab, tab, cspec, cspec]
        args += [cos, sin, cache_k, cache_v]
    else:
        cblk = pl.BlockSpec((1, 1, n, bw), lambda s, h: (s, layer, 0, h))
        out_specs += [cblk, cblk]
        out_shape += [jax.ShapeDtypeStruct((BATCH, DEPTH, SEQ, DIFF_HEADS * hw), F32)] * 2
        if prev is not None:
            in_specs += [pl.BlockSpec(memory_space=pl.ANY)] * 2
            args += list(prev)
            aliases = {7: 1, 8: 2}
    return pl.pallas_call(
        functools.partial(_diff_kernel, latent=latent, lam_init=lam_init, tq=tq, hpb=hpb),
        grid=(nseq, DIFF_HEADS // hpb),
        in_specs=in_specs,
        out_specs=out_specs,
        out_shape=out_shape,
        input_output_aliases=aliases,
        scratch_shapes=[pltpu.VMEM((hpb, skv, hw), BF16), pltpu.VMEM((hpb, skv, hw), BF16)],
        compiler_params=_cp(("parallel", "parallel"), 48),
        name="diff_latent" if latent else "diff_context",
    )(*args)


def _gqa_kernel(*refs, latent, tq):
    if latent:
        (q_ref, k_ref, v_ref, nq_ref, nk_ref, cos_ref, sin_ref, ck_ref, cv_ref,
         out_ref, k_scr, v_scr) = refs
    else:
        q_ref, k_ref, v_ref, nq_ref, nk_ref = refs[:5]
        out_ref, ckout_ref, cvout_ref, k_scr, v_scr = refs[-5:]
    n = q_ref.shape[0]
    seg128 = _seg_matrix(LANES)
    kn = _rmsnorm_seg(k_ref[...].astype(F32), nk_ref[...], seg128, GQA_DH)
    v = v_ref[...].astype(F32)
    if latent:
        kn = _rope(kn, cos_ref[...], sin_ref[...], GQA_DH)
        k_scr[n:, :] = ck_ref[0, 0].astype(BF16)
        v_scr[n:, :] = cv_ref[0, 0].astype(BF16)
    else:
        ckout_ref[0, 0] = kn
        cvout_ref[0, 0] = v
    k_scr[0:n, :] = kn.astype(BF16)
    v_scr[0:n, :] = v.astype(BF16)
    kall = k_scr[...]
    vall = v_scr[...]
    scale = GQA_DH ** -0.5
    for r in range(GQA_REP):
        sl = slice(r * GQA_DH, (r + 1) * GQA_DH)
        qn = _rmsnorm_seg(q_ref[:, sl].astype(F32), nq_ref[...], seg128, GQA_DH)
        if latent:
            qn = _rope(qn, cos_ref[...], sin_ref[...], GQA_DH)
        qb16 = qn.astype(BF16)
        for blk in range(n // tq):
            s = _dot_nt(qb16[blk * tq:(blk + 1) * tq, :], kall) * scale
            e = jnp.exp(s - jnp.max(s, axis=-1, keepdims=True))
            p = e * (1.0 / jnp.sum(e, axis=-1, keepdims=True))
            out_ref[blk * tq:(blk + 1) * tq, sl] = _dot(p.astype(BF16), vall).astype(out_ref.dtype)


def _gqa_attention(z, nq, nk, cache_k, cache_v, layer, latent, prev=None):
    qw = GQA_REP * GQA_DH
    aliases = {}
    if latent:
        n, nseq, row0, skv = DEC_SEQ, DEC_BATCH, TP // DEC_SEQ, DEC_SEQ + PAST_LEN
    else:
        n, nseq, row0, skv = SEQ, BATCH, 0, SEQ
    tq = 256
    vec = pl.BlockSpec((1, GQA_DH), lambda s, g: (0, 0))
    in_specs = [
        pl.BlockSpec((n, qw), lambda s, g: (row0 + s, C_GQ // qw + g)),
        pl.BlockSpec((n, GQA_DH), lambda s, g: (row0 + s, C_GK // GQA_DH + g)),
        pl.BlockSpec((n, GQA_DH), lambda s, g: (row0 + s, C_GV // GQA_DH + g)),
        vec, vec]
    args = [z, z, z, nq, nk]
    out_specs = [pl.BlockSpec((n, qw), lambda s, g: (s, g))]
    out_shape = [jax.ShapeDtypeStruct((nseq * n, GQA_HEADS * GQA_DH), BF16)]
    if latent:
        cos, sin = _rope_tables(GQA_DH, 1)
        tab = pl.BlockSpec((DEC_SEQ, GQA_DH), lambda s, g: (0, 0))
        cspec = pl.BlockSpec((1, 1, PAST_LEN, GQA_DH), lambda s, g: (s, layer, 0, g))
        in_specs += [tab, tab, cspec, cspec]
        args += [cos, sin, cache_k, cache_v]
    else:
        cblk = pl.BlockSpec((1, 1, n, GQA_DH), lambda s, g: (s, layer, 0, g))
        out_specs += [cblk, cblk]
        out_shape += [jax.ShapeDtypeStruct((BATCH, DEPTH, SEQ, GQA_KV_HEADS * GQA_DH), F32)] * 2
        if prev is not None:
            in_specs += [pl.BlockSpec(memory_space=pl.ANY)] * 2
            args += list(prev)
            aliases = {5: 1, 6: 2}
    return pl.pallas_call(
        functools.partial(_gqa_kernel, latent=latent, tq=tq),
        grid=(nseq, GQA_KV_HEADS),
        in_specs=in_specs,
        out_specs=out_specs,
        out_shape=out_shape,
        input_output_aliases=aliases,
        scratch_shapes=[pltpu.VMEM((skv, GQA_DH), BF16), pltpu.VMEM((skv, GQA_DH), BF16)],
        compiler_params=_cp(("parallel", "parallel"), 48),
        name="gqa_latent" if latent else "gqa_context",
    )(*args)


GATE_BLK = 512


def _merge_kernel(of_ref, ob_ref, hg_ref, hn_ref, dp_ref, ds_ref, gp_ref, gs_ref, *rest, n_prompt_blocks):
    g_refs = rest[:-3]
    wb_ref, m_ref, br_scr = rest[-3:]
    i = pl.program_id(0)
    nq = D_MODEL // GATE_BLK

    seg128 = _seg_matrix(LANES)
    for h in range(HGRN_HEADS):
        sl = slice(h * HGRN_DK, (h + 1) * HGRN_DK)
        o = of_ref[:, sl] + ob_ref[:, sl]
        g = hg_ref[:, sl].astype(F32)
        y = _rmsnorm_seg(o, hn_ref[...], seg128, HGRN_DK) * (g * _sigmoid(g))
        br_scr[0, :, sl] = y.astype(BF16)

    @pl.when(i < n_prompt_blocks)
    def _():
        br_scr[1] = dp_ref[...]
        br_scr[2] = gp_ref[...]

    @pl.when(i >= n_prompt_blocks)
    def _():
        br_scr[1] = ds_ref[...]
        br_scr[2] = gs_ref[...]

    for q in range(nq):
        cols = slice(q * GATE_BLK, (q + 1) * GATE_BLK)
        acc = _sigmoid(g_refs[q][...].astype(F32)) * _dot(br_scr[0], wb_ref[0, 0, :, cols])
        acc += _sigmoid(g_refs[nq + q][...].astype(F32)) * _dot(br_scr[1], wb_ref[0, 1, :, cols])
        acc += _sigmoid(g_refs[2 * nq + q][...].astype(F32)) * _dot(br_scr[2], wb_ref[0, 2, :, cols])
        m_ref[:, cols] = acc.astype(m_ref.dtype)


def _merge(o_f, o_b, z, hgrn_norm_row, diff_p, diff_s, gqa_p, gqa_s, w_branch, layer):
    tm = 256
    npb = TP // tm
    nsb = TS // tm
    nq = D_MODEL // GATE_BLK
    row = lambda i: (i, 0)
    prow = lambda i: (jnp.minimum(i, npb - 1), 0)
    srow = lambda i: (jnp.clip(i - npb, 0, nsb - 1), 0)
    gates = [pl.BlockSpec((tm, GATE_BLK), lambda i, b=(C_GATE + c * D_MODEL) // GATE_BLK + q: (i, b))
             for c in range(N_BRANCH) for q in range(nq)]
    return pl.pallas_call(
        functools.partial(_merge_kernel, n_prompt_blocks=npb),
        grid=(TT // tm,),
        in_specs=[
            pl.BlockSpec((tm, HGRN_W), row),
            pl.BlockSpec((tm, HGRN_W), row),
            pl.BlockSpec((tm, HGRN_W), lambda i: (i, C_HG // HGRN_W)),
            pl.BlockSpec((1, HGRN_DK), lambda i: (0, 0)),
            pl.BlockSpec((tm, BRANCH_W), prow),
            pl.BlockSpec((tm, BRANCH_W), srow),
            pl.BlockSpec((tm, BRANCH_W), prow),
            pl.BlockSpec((tm, BRANCH_W), srow),
            *gates,
            pl.BlockSpec((1, N_BRANCH, BRANCH_W, D_MODEL), lambda i: (layer, 0, 0, 0)),
        ],
        out_specs=pl.BlockSpec((tm, D_MODEL), row),
        out_shape=jax.ShapeDtypeStruct((TT, D_MODEL), BF16),
        scratch_shapes=[pltpu.VMEM((N_BRANCH, tm, BRANCH_W), BF16)],
        compiler_params=_cp(("parallel",), 56),
        name="branch_merge",
    )(o_f, o_b, z, hgrn_norm_row, diff_p, diff_s, gqa_p, gqa_s, *([z] * (N_BRANCH * nq)), w_branch)


def _out_proj_kernel(m_ref, w_ref, x_ref, mod_ref, y_ref, *, gate_idx):
    y_ref[...] = x_ref[...] + mod_ref[0, 0, gate_idx:gate_idx + 1, :] * _dot(m_ref[...], w_ref[0])


def _out_proj(m, w_out, x, mods4, layer, gate_idx):
    tm, tn = 512, D_MODEL
    return pl.pallas_call(
        functools.partial(_out_proj_kernel, gate_idx=gate_idx),
        grid=(TT // tm, D_MODEL // tn),
        in_specs=[
            pl.BlockSpec((tm, D_MODEL), lambda i, j: (i, 0)),
            pl.BlockSpec((1, D_MODEL, tn), lambda i, j: (layer, 0, j)),
            pl.BlockSpec((tm, tn), lambda i, j: (i, j)),
            pl.BlockSpec((1, 1, 6, tn), lambda i, j: (layer, _mod_row(i, tm), 0, j)),
        ],
        out_specs=pl.BlockSpec((tm, tn), lambda i, j: (i, j)),
        out_shape=jax.ShapeDtypeStruct((TT, D_MODEL), F32),
        compiler_params=_cp(("parallel", "parallel"), 48),
        name="out_proj",
    )(m, w_out, x, mods4)


_STAIR = [PEER_TOPK // (r + 1) for r in range(8)]


def _extract_top(s, k, want_rank):
    vals = []
    rank = jnp.full(s.shape, float(k), F32) if want_rank else None
    for r in range(k):
        m = jnp.max(s, axis=0, keepdims=True)
        hit = s >= m
        vals.append(m)
        if want_rank:
            rank = jnp.where(hit, float(r), rank)
        s = jnp.where(hit, NEG_INF, s)
    return vals, rank


def _route_kernel(q_ref, keys_ref, r1_ref, r2_ref, *, tm):
    kb = keys_ref[0].astype(BF16)
    sub8 = lax.broadcasted_iota(jnp.int32, (8, LANES), 0)
    for g in range(tm // LANES):
        rows = slice(g * LANES, (g + 1) * LANES)
        cols = slice(g * LANES, (g + 1) * LANES)
        s1 = _dot_nt(kb[0], q_ref[rows, 0:N_KEYS])
        s2 = _dot_nt(kb[1], q_ref[rows, N_KEYS:2 * N_KEYS])
        v1, _ = _extract_top(s1, PEER_TOPK, False)
        v2, rank2 = _extract_top(s2, PEER_TOPK, True)
        sv2 = jnp.concatenate(v2, axis=0)
        sv1_hi = jnp.concatenate(v1[8:], axis=0)
        cand = [v1[0] + sv2, v1[1] + sv2[0:8]]
        for r in range(2, 8):
            cand.append(jnp.where(sub8 < _STAIR[r], v1[r] + sv2[0:8], NEG_INF))
        cand.append(sv1_hi + v2[0])
        cand = jnp.concatenate(cand, axis=0)
        tau = _extract_top(cand, PEER_TOPK, False)[0][-1]
        cmax = v1[0] + v2[0]
        zsum = jnp.sum(jnp.where(cand >= tau, jnp.exp(cand - cmax), 0.0), axis=0, keepdims=True)
        rz = 1.0 / zsum
        cnt = jnp.zeros((N_KEYS, LANES), F32)
        for r in range(PEER_TOPK):
            cnt_r = jnp.sum(jnp.where(v1[r] + sv2 >= tau, 1.0, 0.0), axis=0, keepdims=True)
            cnt = jnp.where(s1 == v1[r], cnt_r, cnt)
        r1_ref[0, 0, :, cols] = cnt
        r1_ref[0, 1, :, cols] = jnp.exp(s1 - v1[0]) * rz
        r2_ref[0, 0, :, cols] = rank2.astype(BF16)
        r2_ref[0, 1, :, cols] = jnp.exp(s2 - v2[0]).astype(BF16)


def _peer_route(qp, peer_keys, layer):
    tm = 512
    blk = pl.BlockSpec((1, 2, N_KEYS, tm), lambda i, h: (h, 0, 0, i))
    return pl.pallas_call(
        functools.partial(_route_kernel, tm=tm),
        grid=(TT // tm, PEER_HEADS),
        in_specs=[
            pl.BlockSpec((tm, 2 * N_KEYS), lambda i, h: (i, h)),
            pl.BlockSpec((1, 2, N_KEYS, N_KEYS), lambda i, h: (layer, 0, 0, 0)),
        ],
        out_specs=[blk, blk],
        out_shape=[jax.ShapeDtypeStruct((PEER_HEADS, 2, N_KEYS, TT), F32),
                   jax.ShapeDtypeStruct((PEER_HEADS, 2, N_KEYS, TT), BF16)],
        compiler_params=_cp(("parallel", "parallel"), 32),
        name="peer_route",
    )(qp, peer_keys)


def _expert_kernel(h_ref, u_ref, v_ref, r1_ref, r2_ref, x_ref, mod_ref, y_ref,
                   r2_scr, h_scr, at_scr, g_scr, acc_scr, *, tm, ce, n_parts, gate_idx):
    j = pl.program_id(1)

    @pl.when(j == 0)
    def _():
        acc_scr[...] = jnp.zeros_like(acc_scr)
        r2_scr[...] = r2_ref[...]
        h_scr[...] = h_ref[...]

    hb = h_scr[...]
    pe = ce // n_parts
    packed = (N_KEYS // 16, 16, LANES)
    contrib = None
    for p in range(n_parts):
        at_scr[p] = _dot_nt(u_ref[0, p * pe:(p + 1) * pe, :], hb)
        for ib in range(pe // N_KEYS):
            il = p * (pe // N_KEYS) + ib
            rows = slice(ib * N_KEYS, (ib + 1) * N_KEYS)
            for g in range(tm // LANES):
                cols = slice(g * LANES, (g + 1) * LANES)
                w = jnp.zeros(packed, BF16)
                for h in range(PEER_HEADS):
                    rk = r2_scr[h, 0, :, cols].reshape(packed)
                    e2 = r2_scr[h, 1, :, cols].reshape(packed)
                    cnt = jnp.broadcast_to(r1_ref[h, 0, il:il + 1, cols], (16, LANES)).astype(BF16)[None]
                    e1 = jnp.broadcast_to(r1_ref[h, 1, il:il + 1, cols], (16, LANES)).astype(BF16)[None]
                    w = w + jnp.where(rk < cnt, e2, jnp.zeros_like(e2)) * e1
                a = at_scr[p, rows, cols]
                act = 0.5 * a * (1.0 + lax.erf(a * (2.0 ** -0.5)))
                gt = act.astype(BF16).reshape(packed) * w
                g_scr[p, rows, cols] = gt.reshape(N_KEYS, LANES)
        d = _dot_tn(g_scr[p], v_ref[0, p * pe:(p + 1) * pe, :])
        contrib = d if contrib is None else contrib + d
    acc_scr[...] += contrib

    @pl.when(j == pl.num_programs(1) - 1)
    def _():
        y_ref[...] = x_ref[...] + mod_ref[0, 0, gate_idx:gate_idx + 1, :] * acc_scr[...]


def _peer_experts(h2, u, v, r1, r2, x, mods4, layer, gate_idx):
    tm, ce, n_parts = 512, 1024, 2
    n_i1 = ce // N_KEYS
    return pl.pallas_call(
        functools.partial(_expert_kernel, tm=tm, ce=ce, n_parts=n_parts, gate_idx=gate_idx),
        grid=(TT // tm, N_EXPERTS // ce),
        in_specs=[
            pl.BlockSpec((tm, D_MODEL), lambda i, j: (i, 0)),
            pl.BlockSpec((1, ce, D_MODEL), lambda i, j: (layer, j, 0)),
            pl.BlockSpec((1, ce, D_MODEL), lambda i, j: (layer, j, 0)),
            pl.BlockSpec((PEER_HEADS, 2, n_i1, tm), lambda i, j: (0, 0, j, i)),
            pl.BlockSpec((PEER_HEADS, 2, N_KEYS, tm), lambda i, j: (0, 0, 0, i)),
            pl.BlockSpec((tm, D_MODEL), lambda i, j: (i, 0)),
            pl.BlockSpec((1, 1, 6, D_MODEL), lambda i, j: (layer, _mod_row(i, tm), 0, 0)),
        ],
        out_specs=pl.BlockSpec((tm, D_MODEL), lambda i, j: (i, 0)),
        out_shape=jax.ShapeDtypeStruct((TT, D_MODEL), F32),
        scratch_shapes=[
            pltpu.VMEM((PEER_HEADS, 2, N_KEYS, tm), BF16),
            pltpu.VMEM((tm, D_MODEL), BF16),
            pltpu.VMEM((n_parts, ce // n_parts, tm), F32),
            pltpu.VMEM((n_parts, ce // n_parts, tm), BF16),
            pltpu.VMEM((tm, D_MODEL), F32),
        ],
        compiler_params=_cp(("parallel", "arbitrary"), 56),
        name="peer_experts",
    )(h2, u, v, r1, r2, x, mods4)


def kernel(x_prompt, x_sample, c, cache_diff_k, cache_diff_v, cache_gqa_k, cache_gqa_v, state_hgrn, c_ctx,
           mod_w, mod_b, norm_mix, norm_ffn, w_in, hgrn_lb, hgrn_norm, diff_qk_norm, diff_lambda, diff_subln,
           gqa_qk_norm, w_branch, w_out, peer_wq, peer_keys, peer_u, peer_v):
    x = jnp.concatenate([x_prompt.reshape(TP, D_MODEL), x_sample.reshape(TS, D_MODEL)], axis=0)
    cond8 = jnp.concatenate([c_ctx[None, :], c, jnp.zeros((8 - 1 - DEC_BATCH, D_MODEL), F32)], axis=0)
    mods4 = _modulation(cond8, mod_w, mod_b).reshape(DEPTH, 8, 6, D_MODEL)

    lb_all = jnp.cumsum(jax.nn.softmax(hgrn_lb.astype(F32), axis=1), axis=1)
    lb_all = lb_all - lb_all[:, :1]

    w_in_b = w_in
    peer_wq_b = peer_wq.astype(BF16)
    w_branch_b = w_branch.astype(BF16)
    w_out_b = w_out.astype(BF16)
    peer_u_b = peer_u.astype(BF16)
    peer_v_b = peer_v.astype(BF16)

    cdk = cache_diff_k.reshape(DEC_BATCH, DEPTH, PAST_LEN, DIFF_HEADS * 2 * DIFF_DK)
    cdv = cache_diff_v.reshape(DEC_BATCH, DEPTH, PAST_LEN, DIFF_HEADS * 2 * DIFF_DK)
    cgk = cache_gqa_k.reshape(DEC_BATCH, DEPTH, PAST_LEN, GQA_KV_HEADS * GQA_DH)
    cgv = cache_gqa_v.reshape(DEC_BATCH, DEPTH, PAST_LEN, GQA_KV_HEADS * GQA_DH)

    new_state = None
    diff_kv = None
    gqa_kv = None
    for l in range(DEPTH):
        lam_init = 0.8 - 0.6 * math.exp(-0.3 * l)
        z = _norm_matmul(x, mods4, l, norm_mix[l][None, :], w_in_b, sh_idx=0, sc_idx=1,
                         tm=1024, tn=512, out_dtype=BF16, emit_h=False, name="in_proj")[0]

        o_f, new_state = _hgrn(z, lb_all[:, l], state_hgrn, l, rev=False, prev=new_state)
        o_b, new_state = _hgrn(z, lb_all[:, l], state_hgrn, l, rev=True, prev=new_state)

        nq_d = jnp.tile(diff_qk_norm[l, 0], 2)[None, :]
        nk_d = jnp.tile(diff_qk_norm[l, 1], 2)[None, :]
        sub = diff_subln[l][None, :]
        diff_p, *diff_kv = _diff_attention(z, nq_d, nk_d, sub, diff_lambda[l], None, None, l,
                                           lam_init, latent=False, prev=diff_kv)
        diff_s = _diff_attention(z, nq_d, nk_d, sub, diff_lambda[l], cdk, cdv, l, lam_init, latent=True)[0]

        nq_g = gqa_qk_norm[l, 0][None, :]
        nk_g = gqa_qk_norm[l, 1][None, :]
        gqa_p, *gqa_kv = _gqa_attention(z, nq_g, nk_g, None, None, l, latent=False, prev=gqa_kv)
        gqa_s = _gqa_attention(z, nq_g, nk_g, cgk, cgv, l, latent=True)[0]

        m = _merge(o_f, o_b, z, hgrn_norm[l][None, :], diff_p, diff_s, gqa_p, gqa_s, w_branch_b, l)
        x = _out_proj(m, w_out_b, x, mods4, l, gate_idx=2)

        qp, h2 = _norm_matmul(x, mods4, l, norm_ffn[l][None, :], peer_wq_b, sh_idx=3, sc_idx=4,
                              tm=512, tn=D_MODEL, out_dtype=BF16, emit_h=True, name="peer_query")
        r1, r2 = _peer_route(qp, peer_keys, l)
        x = _peer_experts(h2, peer_u_b, peer_v_b, r1, r2, x, mods4, l, gate_idx=5)

    return (x[:TP].reshape(BATCH, SEQ, D_MODEL), x[TP:].reshape(DEC_BATCH, DEC_SEQ, D_MODEL),
            diff_kv[0].reshape(BATCH, DEPTH, SEQ, DIFF_HEADS, 2 * DIFF_DK),
            diff_kv[1].reshape(BATCH, DEPTH, SEQ, DIFF_HEADS, 2 * DIFF_DK),
            gqa_kv[0].reshape(BATCH, DEPTH, SEQ, GQA_KV_HEADS, GQA_DH),
            gqa_kv[1].reshape(BATCH, DEPTH, SEQ, GQA_KV_HEADS, GQA_DH),
            new_state)
```

```python
import functools
import math

import numpy as np
import jax
import jax.numpy as jnp
from jax import lax
from jax.experimental import pallas as pl
from jax.experimental.pallas import tpu as pltpu

F32 = jnp.float32
BF16 = jnp.bfloat16

D_MODEL = 2048
BATCH = 16
SEQ = 256
DEPTH = 2
DEC_BATCH = 2
DEC_SEQ = 1024
PAST_LEN = 256
GRID_W = 64
ROPE_THETA = 10000.0
NORM_EPS = 1e-6

HGRN_HEADS = 8
HGRN_DK = 128
HGRN_W = 1024
DIFF_HEADS = 8
DIFF_DK = 64
GQA_HEADS = 8
GQA_KV_HEADS = 2
GQA_DH = 128
GQA_REP = 4
N_BRANCH = 3
BRANCH_W = 1024
PEER_HEADS = 8
N_KEYS = 128
N_EXPERTS = N_KEYS * N_KEYS
PEER_TOPK = 16

TP = BATCH * SEQ
TS = DEC_BATCH * DEC_SEQ
TT = TP + TS

C_HQ, C_HF0, C_HF1, C_HI, C_HG = 0, 1024, 2048, 3072, 4096
C_DQ, C_DK, C_DV = 5120, 6144, 7168
C_GQ, C_GK, C_GV = 8192, 9216, 9472
C_GATE = 9728

LANES = 128
CH = 128
NCH = TT // CH
NPC = TP // CH
CH_PER_PROMPT = SEQ // CH
CH_PER_SAMPLE = DEC_SEQ // CH
N_LEVELS = 7
NEG_INF = float("-inf")
MIN_NORMAL = 1.1754944e-38

_NT = (((1,), (1,)), ((), ()))
_TN = (((0,), (0,)), ((), ()))


def _cp(sem, vmem_mb):
    return pltpu.CompilerParams(dimension_semantics=sem, vmem_limit_bytes=vmem_mb * 1024 * 1024)


def _dot(a, b):
    return jnp.dot(a, b, preferred_element_type=F32)


def _dot_nt(a, b):
    return lax.dot_general(a, b, _NT, preferred_element_type=F32)


def _dot_tn(a, b):
    return lax.dot_general(a, b, _TN, preferred_element_type=F32)


def _sigmoid(x):
    return 0.5 * jnp.tanh(0.5 * x) + 0.5


def _split_bf16(x):
    hi = x.astype(BF16)
    lo = (x - hi.astype(F32)).astype(BF16)
    return hi, lo


def _seg_matrix(seg):
    r = lax.broadcasted_iota(jnp.int32, (LANES, LANES), 0) // seg
    c = lax.broadcasted_iota(jnp.int32, (LANES, LANES), 1) // seg
    return (r == c).astype(BF16)


def _rmsnorm_seg(x, gain_row, seg_mat, seg):
    hi, lo = _split_bf16(x * x)
    ss = _dot(hi, seg_mat) + _dot(lo, seg_mat)
    return x * lax.rsqrt(ss * (1.0 / seg) + NORM_EPS) * gain_row


def _mod_row(i, tm):
    n_p = TP // tm
    per = DEC_SEQ // tm
    return jnp.where(i < n_p, 0, 1 + (i - n_p) // per)


def _mod_kernel(cond_ref, w_ref, b_ref, out_ref):
    a = cond_ref[...]
    a = a * jax.nn.sigmoid(a)
    out_ref[0] = _dot(a.astype(BF16), w_ref[0].astype(BF16)) + b_ref[0]


def _modulation(cond8, mod_w, mod_b):
    tn = 1024
    n6 = 6 * D_MODEL
    return pl.pallas_call(
        _mod_kernel,
        grid=(DEPTH, n6 // tn),
        in_specs=[
            pl.BlockSpec((8, D_MODEL), lambda l, j: (0, 0)),
            pl.BlockSpec((1, D_MODEL, tn), lambda l, j: (l, 0, j)),
            pl.BlockSpec((1, 1, tn), lambda l, j: (l, 0, j)),
        ],
        out_specs=pl.BlockSpec((1, 8, tn), lambda l, j: (l, 0, j)),
        out_shape=jax.ShapeDtypeStruct((DEPTH, 8, n6), F32),
        compiler_params=_cp(("parallel", "parallel"), 40),
        name="modulation",
    )(cond8, mod_w, mod_b.reshape(DEPTH, 1, n6))


def _norm_mm_kernel(x_ref, mod_ref, gain_ref, w_ref, *rest, sh_idx, sc_idx, emit_h):
    if emit_h:
        z_ref, h_ref, h_scr = rest
    else:
        z_ref, h_scr = rest

    @pl.when(pl.program_id(1) == 0)
    def _():
        x = x_ref[...]
        ms = jnp.mean(x * x, axis=-1, keepdims=True)
        y = x * lax.rsqrt(ms + NORM_EPS) * gain_ref[...]
        h = y * (1.0 + mod_ref[0, 0, sc_idx:sc_idx + 1, :]) + mod_ref[0, 0, sh_idx:sh_idx + 1, :]
        hb = h.astype(BF16)
        h_scr[...] = hb
        if emit_h:
            h_ref[...] = hb

    z_ref[...] = _dot(h_scr[...], w_ref[0].astype(BF16)).astype(z_ref.dtype)


def _norm_matmul(x, mods4, layer, gain, w, *, sh_idx, sc_idx, tm, tn, out_dtype, emit_h, name):
    n = w.shape[2]
    out_shape = [jax.ShapeDtypeStruct((TT, n), out_dtype)]
    out_specs = [pl.BlockSpec((tm, tn), lambda i, j: (i, j))]
    if emit_h:
        out_shape.append(jax.ShapeDtypeStruct((TT, D_MODEL), BF16))
        out_specs.append(pl.BlockSpec((tm, D_MODEL), lambda i, j: (i, 0)))
    return pl.pallas_call(
        functools.partial(_norm_mm_kernel, sh_idx=sh_idx, sc_idx=sc_idx, emit_h=emit_h),
        grid=(TT // tm, n // tn),
        in_specs=[
            pl.BlockSpec((tm, D_MODEL), lambda i, j: (i, 0)),
            pl.BlockSpec((1, 1, 6, D_MODEL), lambda i, j: (layer, _mod_row(i, tm), 0, 0)),
            pl.BlockSpec((1, D_MODEL), lambda i, j: (0, 0)),
            pl.BlockSpec((1, D_MODEL, tn), lambda i, j: (layer, 0, j)),
        ],
        out_specs=out_specs,
        out_shape=out_shape,
        scratch_shapes=[pltpu.VMEM((tm, D_MODEL), BF16)],
        compiler_params=_cp(("parallel", "arbitrary"), 48),
        name=name,
    )(x, mods4, gain, w)


def _level_table(rev):
    t = np.arange(CH)[:, None]
    s = np.arange(CH)[None, :]
    x = t ^ s
    lev = np.full((CH, CH), -1, np.int32)
    nz = x > 0
    lev[nz] = np.floor(np.log2(x[nz])).astype(np.int32)
    valid = (t < s) if rev else (t > s)
    lev = np.where(valid, lev, -1)
    lev[np.arange(CH), np.arange(CH)] = N_LEVELS
    return lev.astype(np.int32)


def _bmid(b, b3, m, rev):
    off = m if rev else m - 1
    width = b.shape[1]
    if m >= 8:
        pieces = []
        for j in range(CH // (2 * m)):
            idx = j * 2 * m + off
            pieces.append(jnp.broadcast_to(b[idx:idx + 1, :], (2 * m, width)))
        return pieces[0] if len(pieces) == 1 else jnp.concatenate(pieces, axis=0)
    sub = lax.broadcasted_iota(jnp.int32, (CH // 8, 8, width), 1)
    out = None
    for j in range(8 // (2 * m)):
        idx = j * 2 * m + off
        piece = jnp.broadcast_to(b3[:, idx:idx + 1, :], (CH // 8, 8, width))
        out = piece if out is None else jnp.where(sub >= j * 2 * m, piece, out)
    return out.reshape(CH, width)


def _level_operand(q, kk, b, b3, m, rev, row):
    if m < 8:
        e = jnp.exp(-jnp.abs(b - _bmid(b, b3, m, rev)))
        q_side = ((row // m) % 2) == (0 if rev else 1)
        return jnp.where(q_side, q, kk) * e
    pieces = []
    for j in range(CH // (2 * m)):
        lo = slice(j * 2 * m, j * 2 * m + m)
        hi = slice(j * 2 * m + m, (j + 1) * 2 * m)
        mid = j * 2 * m + (m if rev else m - 1)
        bm = b[mid:mid + 1, :]
        if rev:
            pieces += [q[lo] * jnp.exp(b[lo] - bm), kk[hi] * jnp.exp(bm - b[hi])]
        else:
            pieces += [kk[lo] * jnp.exp(bm - b[lo]), q[hi] * jnp.exp(b[hi] - bm)]
    return jnp.concatenate(pieces, axis=0)


def _hgrn_kernel(*refs, rev):
    q_ref, f_ref, v_ref, lb_ref, s0_ref, lev_ref = refs[:6]
    o_ref, sfin_ref, st_scr = refs[-3:]
    i = pl.program_id(0)
    c = (NCH - 1 - i) if rev else i
    is_prompt = c < NPC
    cp_first = (c % CH_PER_PROMPT) == 0
    cp_last = (c % CH_PER_PROMPT) == CH_PER_PROMPT - 1
    cs = jnp.maximum(c - NPC, 0)
    cs_first = (cs % CH_PER_SAMPLE) == 0
    cs_last = (cs % CH_PER_SAMPLE) == CH_PER_SAMPLE - 1
    if rev:
        start_p, end_p, start_s = cp_last, cp_first, cs_last
    else:
        start_p, end_p, start_s = cp_first, cp_last, cs_first

    @pl.when(jnp.logical_and(is_prompt, start_p))
    def _():
        st_scr[...] = jnp.zeros_like(st_scr)

    @pl.when(jnp.logical_and(jnp.logical_not(is_prompt), start_s))
    def _():
        for h in range(HGRN_HEADS):
            st_scr[h] = s0_ref[0, 0, 0, h].T

    row = lax.broadcasted_iota(jnp.int32, (CH, CH), 0)
    col = lax.broadcasted_iota(jnp.int32, (CH, CH), 1)
    tri = ((col >= row) if rev else (col <= row)).astype(BF16)
    lev = lev_ref[...]

    q = q_ref[...].astype(F32)
    zf = f_ref[...].astype(F32)
    lb = lb_ref[...]
    f = lb + (1.0 - lb) * jax.nn.sigmoid(zf)
    kk = (1.0 - lb) * jax.nn.sigmoid(-zf)
    hi, lo = _split_bf16(jnp.log(jnp.maximum(f, MIN_NORMAL)))
    b = _dot(tri, hi) + _dot(tri, lo)
    b3 = b.reshape(CH // 8, 8, HGRN_W)
    row_w = lax.broadcasted_iota(jnp.int32, (CH, HGRN_W), 0)
    xs = [_level_operand(q, kk, b, b3, 1 << lm, rev, row_w).astype(BF16) for lm in range(N_LEVELS)]
    qb = q.astype(BF16)
    kb = kk.astype(BF16)
    vb = v_ref[...].astype(BF16)
    b_end = b[0:1, :] if rev else b[CH - 1:CH, :]
    qd = (q * jnp.exp(b)).astype(BF16)
    kd = (kk * jnp.exp(b_end - b)).astype(BF16)
    dec = jnp.exp(b_end)
    blocks = [slice(r * 8, (r + 1) * 8) for r in range(CH // 8)]

    for h in range(HGRN_HEADS):
        sl = slice(h * HGRN_DK, (h + 1) * HGRN_DK)
        p = _dot_nt(qb[:, sl], kb[:, sl])
        a_rows = [jnp.where(lev[rs] == N_LEVELS, p[rs], 0.0) for rs in blocks]
        for lm in range(N_LEVELS):
            p = _dot_nt(xs[lm][:, sl], xs[lm][:, sl])
            for r, rs in enumerate(blocks):
                if lm >= 3 and ((r >> (lm - 3)) & 1) == (1 if rev else 0):
                    continue
                a_rows[r] = jnp.where(lev[rs] == lm, p[rs], a_rows[r])
        a = jnp.concatenate(a_rows, axis=0)

        st = st_scr[h]
        o_ref[:, sl] = _dot(a.astype(BF16), vb[:, sl]) + _dot_nt(qd[:, sl], st.astype(BF16))
        st_scr[h] = st * dec[:, sl] + _dot_tn(vb[:, sl], kd[:, sl])

    @pl.when(jnp.logical_and(is_prompt, end_p))
    def _():
        for h in range(HGRN_HEADS):
            sfin_ref[0, 0, 0, h] = st_scr[h].T


def _hgrn(z, lb_l, state_hgrn, layer, rev, prev=None):
    d = 1 if rev else 0
    lev = jnp.asarray(_level_table(rev))

    def cidx(i):
        return (NCH - 1 - i) if rev else i

    def s0_map(i):
        b = jnp.clip((cidx(i) - NPC) // CH_PER_SAMPLE, 0, DEC_BATCH - 1)
        return (b, layer, d, 0, 0, 0)

    def sfin_map(i):
        return (jnp.minimum(cidx(i) // CH_PER_PROMPT, BATCH - 1), layer, d, 0, 0, 0)

    wblk = HGRN_W
    in_specs = [
        pl.BlockSpec((CH, wblk), lambda i: (cidx(i), C_HQ // wblk)),
        pl.BlockSpec((CH, wblk), lambda i: (cidx(i), (C_HF1 if rev else C_HF0) // wblk)),
        pl.BlockSpec((CH, wblk), lambda i: (cidx(i), C_HI // wblk)),
        pl.BlockSpec((1, wblk), lambda i: (0, 0)),
        pl.BlockSpec((1, 1, 1, HGRN_HEADS, HGRN_DK, HGRN_DK), s0_map),
        pl.BlockSpec((CH, CH), lambda i: (0, 0)),
    ]
    args = [z, z, z, lb_l[d:d + 1], state_hgrn, lev]
    aliases = {}
    if prev is not None:
        in_specs.append(pl.BlockSpec(memory_space=pl.ANY))
        args.append(prev)
        aliases = {6: 1}
    o, sfin = pl.pallas_call(
        functools.partial(_hgrn_kernel, rev=rev),
        grid=(NCH,),
        in_specs=in_specs,
        out_specs=[
            pl.BlockSpec((CH, wblk), lambda i: (cidx(i), 0)),
            pl.BlockSpec((1, 1, 1, HGRN_HEADS, HGRN_DK, HGRN_DK), sfin_map),
        ],
        out_shape=[
            jax.ShapeDtypeStruct((TT, HGRN_W), F32),
            jax.ShapeDtypeStruct((BATCH, DEPTH, 2, HGRN_HEADS, HGRN_DK, HGRN_DK), F32),
        ],
        input_output_aliases=aliases,
        scratch_shapes=[pltpu.VMEM((HGRN_HEADS, HGRN_DK, HGRN_DK), F32)],
        compiler_params=_cp(("arbitrary",), 32),
        name="hgrn_bwd" if rev else "hgrn_fwd",
    )(*args)
    return o, sfin


def _rope_tables(dim, copies):
    nfreq = dim // 4
    inv_freq = ROPE_THETA ** (-np.arange(nfreq, dtype=np.float64) / nfreq)
    t = np.arange(DEC_SEQ)
    pos_row = (t // GRID_W).astype(np.float64)
    pos_col = (t % GRID_W).astype(np.float64)
    lane = np.arange(dim)
    use_col = (lane // (dim // 2)) == 1
    fidx = lane % nfreq
    first = (lane % (dim // 2)) < nfreq
    pos = np.where(use_col[None, :], pos_col[:, None], pos_row[:, None])
    ang = pos * inv_freq[fidx][None, :]
    cos = np.cos(ang)
    sin = np.where(first[None, :], -np.sin(ang), np.sin(ang))
    cos = np.tile(cos, (1, copies)).astype(np.float32)
    sin = np.tile(sin, (1, copies)).astype(np.float32)
    return jnp.asarray(cos), jnp.asarray(sin)


def _rope(x, cos, sin, dim):
    nfreq = dim // 4
    lane = lax.broadcasted_iota(jnp.int32, x.shape, 1)
    first = (lane % (dim // 2)) < nfreq
    partner = jnp.where(first, pltpu.roll(x, LANES - nfreq, 1), pltpu.roll(x, nfreq, 1))
    return x * cos + partner * sin


def _diff_kernel(*refs, latent, lam_init, tq, hpb):
    if latent:
        (q_ref, k_ref, v_ref, nq_ref, nk_ref, sub_ref, lam_ref, cos_ref, sin_ref, ck_ref, cv_ref,
         out_ref, k_scr, v_scr) = refs
    else:
        q_ref, k_ref, v_ref, nq_ref, nk_ref, sub_ref, lam_ref = refs[:7]
        out_ref, ckout_ref, cvout_ref, k_scr, v_scr = refs[-5:]
    n = q_ref.shape[0]
    seg64 = _seg_matrix(DIFF_DK)
    seg128 = _seg_matrix(LANES)
    lp = lam_ref[...]
    lam = (jnp.exp(jnp.sum(lp[0:1] * lp[1:2], axis=-1, keepdims=True))
           - jnp.exp(jnp.sum(lp[2:3] * lp[3:4], axis=-1, keepdims=True)) + lam_init)
    lane = lax.broadcasted_iota(jnp.int32, (tq, LANES), 1)
    scale = DIFF_DK ** -0.5

    for hh in range(hpb):
        hs = slice(hh * LANES, (hh + 1) * LANES)
        qn = _rmsnorm_seg(q_ref[:, hs].astype(F32), nq_ref[...], seg64, DIFF_DK)
        kn = _rmsnorm_seg(k_ref[:, hs].astype(F32), nk_ref[...], seg64, DIFF_DK)
        v = v_ref[:, hs].astype(F32)
        if latent:
            qn = _rope(qn, cos_ref[...], sin_ref[...], DIFF_DK)
            kn = _rope(kn, cos_ref[...], sin_ref[...], DIFF_DK)
            k_scr[hh, n:, :] = ck_ref[0, 0, :, hs].astype(BF16)
            v_scr[hh, n:, :] = cv_ref[0, 0, :, hs].astype(BF16)
        else:
            ckout_ref[0, 0, :, hs] = kn
            cvout_ref[0, 0, :, hs] = v
        k_scr[hh, 0:n, :] = kn.astype(BF16)
        v_scr[hh, 0:n, :] = v.astype(BF16)
        kall = k_scr[hh]
        vall = v_scr[hh]
        for blk in range(n // tq):
            qb = qn[blk * tq:(blk + 1) * tq, :] * scale
            q1 = jnp.where(lane < DIFF_DK, qb, 0.0).astype(BF16)
            q2 = jnp.where(lane >= DIFF_DK, qb, 0.0).astype(BF16)
            s1 = _dot_nt(q1, kall)
            s2 = _dot_nt(q2, kall)
            e1 = jnp.exp(s1 - jnp.max(s1, axis=-1, keepdims=True))
            e2 = jnp.exp(s2 - jnp.max(s2, axis=-1, keepdims=True))
            r1 = 1.0 / jnp.sum(e1, axis=-1, keepdims=True)
            r2 = lam / jnp.sum(e2, axis=-1, keepdims=True)
            p = e1 * r1 - e2 * r2
            o = _dot(p.astype(BF16), vall)
            on = _rmsnorm_seg(o, sub_ref[...], seg128, LANES) * (1.0 - lam_init)
            out_ref[blk * tq:(blk + 1) * tq, hs] = on.astype(out_ref.dtype)


def _diff_attention(z, nq, nk, sub, lam_p, cache_k, cache_v, layer, lam_init, latent, prev=None):
    hw = 2 * DIFF_DK
    if latent:
        n, nseq, row0, skv, hpb = DEC_SEQ, DEC_BATCH, TP // DEC_SEQ, DEC_SEQ + PAST_LEN, 1
    else:
        n, nseq, row0, skv, hpb = SEQ, BATCH, 0, SEQ, 4
    bw = hpb * hw
    tq = 256
    zspec = lambda c0: pl.BlockSpec((n, bw), lambda s, h: (row0 + s, c0 // bw + h))
    vec = pl.BlockSpec((1, hw), lambda s, h: (0, 0))
    in_specs = [zspec(C_DQ), zspec(C_DK), zspec(C_DV), vec, vec, vec,
                pl.BlockSpec((4, DIFF_DK), lambda s, h: (0, 0))]
    args = [z, z, z, nq, nk, sub, lam_p]
    out_specs = [pl.BlockSpec((n, bw), lambda s, h: (s, h))]
    out_shape = [jax.ShapeDtypeStruct((nseq * n, DIFF_HEADS * hw), BF16)]
    aliases = {}
    if latent:
        cos, sin = _rope_tables(DIFF_DK, 2)
        tab = pl.BlockSpec((DEC_SEQ, hw), lambda s, h: (0, 0))
        cspec = pl.BlockSpec((1, 1, PAST_LEN, bw), lambda s, h: (s, layer, 0, h))
        in_specs += [tab, tab, cspec, cspec]
        args += [cos, sin, cache_k, cache_v]
    else:
        cblk = pl.BlockSpec((1, 1, n, bw), lambda s, h: (s, layer, 0, h))
        out_specs += [cblk, cblk]
        out_shape += [jax.ShapeDtypeStruct((BATCH, DEPTH, SEQ, DIFF_HEADS * hw), F32)] * 2
        if prev is not None:
            in_specs += [pl.BlockSpec(memory_space=pl.ANY)] * 2
            args += list(prev)
            aliases = {7: 1, 8: 2}
    return pl.pallas_call(
        functools.partial(_diff_kernel, latent=latent, lam_init=lam_init, tq=tq, hpb=hpb),
        grid=(nseq, DIFF_HEADS // hpb),
        in_specs=in_specs,
        out_specs=out_specs,
        out_shape=out_shape,
        input_output_aliases=aliases,
        scratch_shapes=[pltpu.VMEM((hpb, skv, hw), BF16), pltpu.VMEM((hpb, skv, hw), BF16)],
        compiler_params=_cp(("parallel", "parallel"), 48),
        name="diff_latent" if latent else "diff_context",
    )(*args)


def _gqa_kernel(*refs, latent, tq):
    if latent:
        (q_ref, k_ref, v_ref, nq_ref, nk_ref, cos_ref, sin_ref, ck_ref, cv_ref,
         out_ref, k_scr, v_scr) = refs
    else:
        q_ref, k_ref, v_ref, nq_ref, nk_ref = refs[:5]
        out_ref, ckout_ref, cvout_ref, k_scr, v_scr = refs[-5:]
    n = q_ref.shape[0]
    seg128 = _seg_matrix(LANES)
    kn = _rmsnorm_seg(k_ref[...].astype(F32), nk_ref[...], seg128, GQA_DH)
    v = v_ref[...].astype(F32)
    if latent:
        kn = _rope(kn, cos_ref[...], sin_ref[...], GQA_DH)
        k_scr[n:, :] = ck_ref[0, 0].astype(BF16)
        v_scr[n:, :] = cv_ref[0, 0].astype(BF16)
    else:
        ckout_ref[0, 0] = kn
        cvout_ref[0, 0] = v
    k_scr[0:n, :] = kn.astype(BF16)
    v_scr[0:n, :] = v.astype(BF16)
    kall = k_scr[...]
    vall = v_scr[...]
    scale = GQA_DH ** -0.5
    for r in range(GQA_REP):
        sl = slice(r * GQA_DH, (r + 1) * GQA_DH)
        qn = _rmsnorm_seg(q_ref[:, sl].astype(F32), nq_ref[...], seg128, GQA_DH)
        if latent:
            qn = _rope(qn, cos_ref[...], sin_ref[...], GQA_DH)
        qb16 = qn.astype(BF16)
        for blk in range(n // tq):
            s = _dot_nt(qb16[blk * tq:(blk + 1) * tq, :], kall) * scale
            e = jnp.exp(s - jnp.max(s, axis=-1, keepdims=True))
            p = e * (1.0 / jnp.sum(e, axis=-1, keepdims=True))
            out_ref[blk * tq:(blk + 1) * tq, sl] = _dot(p.astype(BF16), vall).astype(out_ref.dtype)


def _gqa_attention(z, nq, nk, cache_k, cache_v, layer, latent, prev=None):
    qw = GQA_REP * GQA_DH
    aliases = {}
    if latent:
        n, nseq, row0, skv = DEC_SEQ, DEC_BATCH, TP // DEC_SEQ, DEC_SEQ + PAST_LEN
    else:
        n, nseq, row0, skv = SEQ, BATCH, 0, SEQ
    tq = 256
    vec = pl.BlockSpec((1, GQA_DH), lambda s, g: (0, 0))
    in_specs = [
        pl.BlockSpec((n, qw), lambda s, g: (row0 + s, C_GQ // qw + g)),
        pl.BlockSpec((n, GQA_DH), lambda s, g: (row0 + s, C_GK // GQA_DH + g)),
        pl.BlockSpec((n, GQA_DH), lambda s, g: (row0 + s, C_GV // GQA_DH + g)),
        vec, vec]
    args = [z, z, z, nq, nk]
    out_specs = [pl.BlockSpec((n, qw), lambda s, g: (s, g))]
    out_shape = [jax.ShapeDtypeStruct((nseq * n, GQA_HEADS * GQA_DH), BF16)]
    if latent:
        cos, sin = _rope_tables(GQA_DH, 1)
        tab = pl.BlockSpec((DEC_SEQ, GQA_DH), lambda s, g: (0, 0))
        cspec = pl.BlockSpec((1, 1, PAST_LEN, GQA_DH), lambda s, g: (s, layer, 0, g))
        in_specs += [tab, tab, cspec, cspec]
        args += [cos, sin, cache_k, cache_v]
    else:
        cblk = pl.BlockSpec((1, 1, n, GQA_DH), lambda s, g: (s, layer, 0, g))
        out_specs += [cblk, cblk]
        out_shape += [jax.ShapeDtypeStruct((BATCH, DEPTH, SEQ, GQA_KV_HEADS * GQA_DH), F32)] * 2
        if prev is not None:
            in_specs += [pl.BlockSpec(memory_space=pl.ANY)] * 2
            args += list(prev)
            aliases = {5: 1, 6: 2}
    return pl.pallas_call(
        functools.partial(_gqa_kernel, latent=latent, tq=tq),
        grid=(nseq, GQA_KV_HEADS),
        in_specs=in_specs,
        out_specs=out_specs,
        out_shape=out_shape,
        input_output_aliases=aliases,
        scratch_shapes=[pltpu.VMEM((skv, GQA_DH), BF16), pltpu.VMEM((skv, GQA_DH), BF16)],
        compiler_params=_cp(("parallel", "parallel"), 48),
        name="gqa_latent" if latent else "gqa_context",
    )(*args)


GATE_BLK = 512


def _merge_kernel(of_ref, ob_ref, hg_ref, hn_ref, dp_ref, ds_ref, gp_ref, gs_ref, *rest, n_prompt_blocks):
    g_refs = rest[:-3]
    wb_ref, m_ref, br_scr = rest[-3:]
    i = pl.program_id(0)
    nq = D_MODEL // GATE_BLK

    seg128 = _seg_matrix(LANES)
    for h in range(HGRN_HEADS):
        sl = slice(h * HGRN_DK, (h + 1) * HGRN_DK)
        o = of_ref[:, sl] + ob_ref[:, sl]
        g = hg_ref[:, sl].astype(F32)
        y = _rmsnorm_seg(o, hn_ref[...], seg128, HGRN_DK) * (g * _sigmoid(g))
        br_scr[0, :, sl] = y.astype(BF16)

    @pl.when(i < n_prompt_blocks)
    def _():
        br_scr[1] = dp_ref[...]
        br_scr[2] = gp_ref[...]

    @pl.when(i >= n_prompt_blocks)
    def _():
        br_scr[1] = ds_ref[...]
        br_scr[2] = gs_ref[...]

    for q in range(nq):
        cols = slice(q * GATE_BLK, (q + 1) * GATE_BLK)
        acc = _sigmoid(g_refs[q][...].astype(F32)) * _dot(br_scr[0], wb_ref[0, 0, :, cols])
        acc += _sigmoid(g_refs[nq + q][...].astype(F32)) * _dot(br_scr[1], wb_ref[0, 1, :, cols])
        acc += _sigmoid(g_refs[2 * nq + q][...].astype(F32)) * _dot(br_scr[2], wb_ref[0, 2, :, cols])
        m_ref[:, cols] = acc.astype(m_ref.dtype)


def _merge(o_f, o_b, z, hgrn_norm_row, diff_p, diff_s, gqa_p, gqa_s, w_branch, layer):
    tm = 256
    npb = TP // tm
    nsb = TS // tm
    nq = D_MODEL // GATE_BLK
    row = lambda i: (i, 0)
    prow = lambda i: (jnp.minimum(i, npb - 1), 0)
    srow = lambda i: (jnp.clip(i - npb, 0, nsb - 1), 0)
    gates = [pl.BlockSpec((tm, GATE_BLK), lambda i, b=(C_GATE + c * D_MODEL) // GATE_BLK + q: (i, b))
             for c in range(N_BRANCH) for q in range(nq)]
    return pl.pallas_call(
        functools.partial(_merge_kernel, n_prompt_blocks=npb),
        grid=(TT // tm,),
        in_specs=[
            pl.BlockSpec((tm, HGRN_W), row),
            pl.BlockSpec((tm, HGRN_W), row),
            pl.BlockSpec((tm, HGRN_W), lambda i: (i, C_HG // HGRN_W)),
            pl.BlockSpec((1, HGRN_DK), lambda i: (0, 0)),
            pl.BlockSpec((tm, BRANCH_W), prow),
            pl.BlockSpec((tm, BRANCH_W), srow),
            pl.BlockSpec((tm, BRANCH_W), prow),
            pl.BlockSpec((tm, BRANCH_W), srow),
            *gates,
            pl.BlockSpec((1, N_BRANCH, BRANCH_W, D_MODEL), lambda i: (layer, 0, 0, 0)),
        ],
        out_specs=pl.BlockSpec((tm, D_MODEL), row),
        out_shape=jax.ShapeDtypeStruct((TT, D_MODEL), BF16),
        scratch_shapes=[pltpu.VMEM((N_BRANCH, tm, BRANCH_W), BF16)],
        compiler_params=_cp(("parallel",), 56),
        name="branch_merge",
    )(o_f, o_b, z, hgrn_norm_row, diff_p, diff_s, gqa_p, gqa_s, *([z] * (N_BRANCH * nq)), w_branch)


def _out_proj_kernel(m_ref, w_ref, x_ref, mod_ref, y_ref, *, gate_idx):
    y_ref[...] = x_ref[...] + mod_ref[0, 0, gate_idx:gate_idx + 1, :] * _dot(m_ref[...], w_ref[0])


def _out_proj(m, w_out, x, mods4, layer, gate_idx):
    tm, tn = 512, D_MODEL
    return pl.pallas_call(
        functools.partial(_out_proj_kernel, gate_idx=gate_idx),
        grid=(TT // tm, D_MODEL // tn),
        in_specs=[
            pl.BlockSpec((tm, D_MODEL), lambda i, j: (i, 0)),
            pl.BlockSpec((1, D_MODEL, tn), lambda i, j: (layer, 0, j)),
            pl.BlockSpec((tm, tn), lambda i, j: (i, j)),
            pl.BlockSpec((1, 1, 6, tn), lambda i, j: (layer, _mod_row(i, tm), 0, j)),
        ],
        out_specs=pl.BlockSpec((tm, tn), lambda i, j: (i, j)),
        out_shape=jax.ShapeDtypeStruct((TT, D_MODEL), F32),
        compiler_params=_cp(("parallel", "parallel"), 48),
        name="out_proj",
    )(m, w_out, x, mods4)


_STAIR = [PEER_TOPK // (r + 1) for r in range(8)]


def _extract_top(s, k, want_rank):
    vals = []
    rank = jnp.full(s.shape, float(k), F32) if want_rank else None
    for r in range(k):
        m = jnp.max(s, axis=0, keepdims=True)
        hit = s >= m
        vals.append(m)
        if want_rank:
            rank = jnp.where(hit, float(r), rank)
        s = jnp.where(hit, NEG_INF, s)
    return vals, rank


def _route_kernel(q_ref, keys_ref, r1_ref, r2_ref, *, tm):
    kb = keys_ref[0].astype(BF16)
    sub8 = lax.broadcasted_iota(jnp.int32, (8, LANES), 0)
    for g in range(tm // LANES):
        rows = slice(g * LANES, (g + 1) * LANES)
        cols = slice(g * LANES, (g + 1) * LANES)
        s1 = _dot_nt(kb[0], q_ref[rows, 0:N_KEYS])
        s2 = _dot_nt(kb[1], q_ref[rows, N_KEYS:2 * N_KEYS])
        v1, _ = _extract_top(s1, PEER_TOPK, False)
        v2, rank2 = _extract_top(s2, PEER_TOPK, True)
        sv2 = jnp.concatenate(v2, axis=0)
        sv1_hi = jnp.concatenate(v1[8:], axis=0)
        cand = [v1[0] + sv2, v1[1] + sv2[0:8]]
        for r in range(2, 8):
            cand.append(jnp.where(sub8 < _STAIR[r], v1[r] + sv2[0:8], NEG_INF))
        cand.append(sv1_hi + v2[0])
        cand = jnp.concatenate(cand, axis=0)
        tau = _extract_top(cand, PEER_TOPK, False)[0][-1]
        cmax = v1[0] + v2[0]
        zsum = jnp.sum(jnp.where(cand >= tau, jnp.exp(cand - cmax), 0.0), axis=0, keepdims=True)
        rz = 1.0 / zsum
        cnt = jnp.zeros((N_KEYS, LANES), F32)
        for r in range(PEER_TOPK):
            cnt_r = jnp.sum(jnp.where(v1[r] + sv2 >= tau, 1.0, 0.0), axis=0, keepdims=True)
            cnt = jnp.where(s1 == v1[r], cnt_r, cnt)
        r1_ref[0, 0, :, cols] = cnt
        r1_ref[0, 1, :, cols] = jnp.exp(s1 - v1[0]) * rz
        r2_ref[0, 0, :, cols] = rank2.astype(BF16)
        r2_ref[0, 1, :, cols] = jnp.exp(s2 - v2[0]).astype(BF16)


def _peer_route(qp, peer_keys, layer):
    tm = 512
    blk = pl.BlockSpec((1, 2, N_KEYS, tm), lambda i, h: (h, 0, 0, i))
    return pl.pallas_call(
        functools.partial(_route_kernel, tm=tm),
        grid=(TT // tm, PEER_HEADS),
        in_specs=[
            pl.BlockSpec((tm, 2 * N_KEYS), lambda i, h: (i, h)),
            pl.BlockSpec((1, 2, N_KEYS, N_KEYS), lambda i, h: (layer, 0, 0, 0)),
        ],
        out_specs=[blk, blk],
        out_shape=[jax.ShapeDtypeStruct((PEER_HEADS, 2, N_KEYS, TT), F32),
                   jax.ShapeDtypeStruct((PEER_HEADS, 2, N_KEYS, TT), BF16)],
        compiler_params=_cp(("parallel", "parallel"), 32),
        name="peer_route",
    )(qp, peer_keys)


def _expert_kernel(h_ref, u_ref, v_ref, r1_ref, r2_ref, x_ref, mod_ref, y_ref,
                   r2_scr, h_scr, at_scr, g_scr, acc_scr, *, tm, ce, n_parts, gate_idx):
    j = pl.program_id(1)

    @pl.when(j == 0)
    def _():
        acc_scr[...] = jnp.zeros_like(acc_scr)
        r2_scr[...] = r2_ref[...]
        h_scr[...] = h_ref[...]

    hb = h_scr[...]
    pe = ce // n_parts
    packed = (N_KEYS // 16, 16, LANES)
    contrib = None
    for p in range(n_parts):
        at_scr[p] = _dot_nt(u_ref[0, p * pe:(p + 1) * pe, :], hb)
        for ib in range(pe // N_KEYS):
            il = p * (pe // N_KEYS) + ib
            rows = slice(ib * N_KEYS, (ib + 1) * N_KEYS)
            for g in range(tm // LANES):
                cols = slice(g * LANES, (g + 1) * LANES)
                w = None
                for h in range(PEER_HEADS):
                    rk = r2_scr[h, 0, :, cols].reshape(packed)
                    e2 = r2_scr[h, 1, :, cols].reshape(packed)
                    cnt = jnp.broadcast_to(r1_ref[h, 0, il:il + 1, cols], (16, LANES)).astype(BF16)[None]
                    e1 = jnp.broadcast_to(r1_ref[h, 1, il:il + 1, cols], (16, LANES)).astype(BF16)[None]
                    wh = jnp.where(rk < cnt, e2, jnp.zeros_like(e2)) * e1
                    w = wh if w is None else w + wh
                a = at_scr[p, rows, cols]
                act = 0.5 * a * (1.0 + lax.erf(a * (2.0 ** -0.5)))
                gt = act.astype(BF16).reshape(packed) * w
                g_scr[p, rows, cols] = gt.reshape(N_KEYS, LANES)
        d = _dot_tn(g_scr[p], v_ref[0, p * pe:(p + 1) * pe, :])
        contrib = d if contrib is None else contrib + d
    acc_scr[...] += contrib

    @pl.when(j == pl.num_programs(1) - 1)
    def _():
        y_ref[...] = x_ref[...] + mod_ref[0, 0, gate_idx:gate_idx + 1, :] * acc_scr[...]


def _peer_experts(h2, u, v, r1, r2, x, mods4, layer, gate_idx):
    tm, ce, n_parts = 512, 1024, 2
    n_i1 = ce // N_KEYS
    return pl.pallas_call(
        functools.partial(_expert_kernel, tm=tm, ce=ce, n_parts=n_parts, gate_idx=gate_idx),
        grid=(TT // tm, N_EXPERTS // ce),
        in_specs=[
            pl.BlockSpec((tm, D_MODEL), lambda i, j: (i, 0)),
            pl.BlockSpec((1, ce, D_MODEL), lambda i, j: (layer, j, 0)),
            pl.BlockSpec((1, ce, D_MODEL), lambda i, j: (layer, j, 0)),
            pl.BlockSpec((PEER_HEADS, 2, n_i1, tm), lambda i, j: (0, 0, j, i)),
            pl.BlockSpec((PEER_HEADS, 2, N_KEYS, tm), lambda i, j: (0, 0, 0, i)),
            pl.BlockSpec((tm, D_MODEL), lambda i, j: (i, 0)),
            pl.BlockSpec((1, 1, 6, D_MODEL), lambda i, j: (layer, _mod_row(i, tm), 0, 0)),
        ],
        out_specs=pl.BlockSpec((tm, D_MODEL), lambda i, j: (i, 0)),
        out_shape=jax.ShapeDtypeStruct((TT, D_MODEL), F32),
        scratch_shapes=[
            pltpu.VMEM((PEER_HEADS, 2, N_KEYS, tm), BF16),
            pltpu.VMEM((tm, D_MODEL), BF16),
            pltpu.VMEM((n_parts, ce // n_parts, tm), F32),
            pltpu.VMEM((n_parts, ce // n_parts, tm), BF16),
            pltpu.VMEM((tm, D_MODEL), F32),
        ],
        compiler_params=_cp(("parallel", "arbitrary"), 56),
        name="peer_experts",
    )(h2, u, v, r1, r2, x, mods4)


def kernel(x_prompt, x_sample, c, cache_diff_k, cache_diff_v, cache_gqa_k, cache_gqa_v, state_hgrn, c_ctx,
           mod_w, mod_b, norm_mix, norm_ffn, w_in, hgrn_lb, hgrn_norm, diff_qk_norm, diff_lambda, diff_subln,
           gqa_qk_norm, w_branch, w_out, peer_wq, peer_keys, peer_u, peer_v):
    x = jnp.concatenate([x_prompt.reshape(TP, D_MODEL), x_sample.reshape(TS, D_MODEL)], axis=0)
    cond8 = jnp.concatenate([c_ctx[None, :], c, jnp.zeros((8 - 1 - DEC_BATCH, D_MODEL), F32)], axis=0)
    mods4 = _modulation(cond8, mod_w, mod_b).reshape(DEPTH, 8, 6, D_MODEL)

    lb_all = jnp.cumsum(jax.nn.softmax(hgrn_lb.astype(F32), axis=1), axis=1)
    lb_all = lb_all - lb_all[:, :1]

    w_in_b = w_in
    peer_wq_b = peer_wq.astype(BF16)
    w_branch_b = w_branch.astype(BF16)
    w_out_b = w_out.astype(BF16)
    peer_u_b = peer_u.astype(BF16)
    peer_v_b = peer_v.astype(BF16)

    cdk = cache_diff_k.reshape(DEC_BATCH, DEPTH, PAST_LEN, DIFF_HEADS * 2 * DIFF_DK)
    cdv = cache_diff_v.reshape(DEC_BATCH, DEPTH, PAST_LEN, DIFF_HEADS * 2 * DIFF_DK)
    cgk = cache_gqa_k.reshape(DEC_BATCH, DEPTH, PAST_LEN, GQA_KV_HEADS * GQA_DH)
    cgv = cache_gqa_v.reshape(DEC_BATCH, DEPTH, PAST_LEN, GQA_KV_HEADS * GQA_DH)

    new_state = None
    diff_kv = None
    gqa_kv = None
    for l in range(DEPTH):
        lam_init = 0.8 - 0.6 * math.exp(-0.3 * l)
        z = _norm_matmul(x, mods4, l, norm_mix[l][None, :], w_in_b, sh_idx=0, sc_idx=1,
                         tm=1024, tn=512, out_dtype=BF16, emit_h=False, name="in_proj")[0]

        o_f, new_state = _hgrn(z, lb_all[:, l], state_hgrn, l, rev=False, prev=new_state)
        o_b, new_state = _hgrn(z, lb_all[:, l], state_hgrn, l, rev=True, prev=new_state)

        nq_d = jnp.tile(diff_qk_norm[l, 0], 2)[None, :]
        nk_d = jnp.tile(diff_qk_norm[l, 1], 2)[None, :]
        sub = diff_subln[l][None, :]
        diff_p, *diff_kv = _diff_attention(z, nq_d, nk_d, sub, diff_lambda[l], None, None, l,
                                           lam_init, latent=False, prev=diff_kv)
        diff_s = _diff_attention(z, nq_d, nk_d, sub, diff_lambda[l], cdk, cdv, l, lam_init, latent=True)[0]

        nq_g = gqa_qk_norm[l, 0][None, :]
        nk_g = gqa_qk_norm[l, 1][None, :]
        gqa_p, *gqa_kv = _gqa_attention(z, nq_g, nk_g, None, None, l, latent=False, prev=gqa_kv)
        gqa_s = _gqa_attention(z, nq_g, nk_g, cgk, cgv, l, latent=True)[0]

        m = _merge(o_f, o_b, z, hgrn_norm[l][None, :], diff_p, diff_s, gqa_p, gqa_s, w_branch_b, l)
        x = _out_proj(m, w_out_b, x, mods4, l, gate_idx=2)

        qp, h2 = _norm_matmul(x, mods4, l, norm_ffn[l][None, :], peer_wq_b, sh_idx=3, sc_idx=4,
                              tm=512, tn=D_MODEL, out_dtype=BF16, emit_h=True, name="peer_query")
        r1, r2 = _peer_route(qp, peer_keys, l)
        x = _peer_experts(h2, peer_u_b, peer_v_b, r1, r2, x, mods4, l, gate_idx=5)

    return (x[:TP].reshape(BATCH, SEQ, D_MODEL), x[TP:].reshape(DEC_BATCH, DEC_SEQ, D_MODEL),
            diff_kv[0].reshape(BATCH, DEPTH, SEQ, DIFF_HEADS, 2 * DIFF_DK),
            diff_kv[1].reshape(BATCH, DEPTH, SEQ, DIFF_HEADS, 2 * DIFF_DK),
            gqa_kv[0].reshape(BATCH, DEPTH, SEQ, GQA_KV_HEADS, GQA_DH),
            gqa_kv[1].reshape(BATCH, DEPTH, SEQ, GQA_KV_HEADS, GQA_DH),
            new_state)
```

```python
import functools
import math

import numpy as np
import jax
import jax.numpy as jnp
from jax import lax
from jax.experimental import pallas as pl
from jax.experimental.pallas import tpu as pltpu

F32 = jnp.float32
BF16 = jnp.bfloat16

D_MODEL = 2048
BATCH = 16
SEQ = 256
DEPTH = 2
DEC_BATCH = 2
DEC_SEQ = 1024
PAST_LEN = 256
GRID_W = 64
ROPE_THETA = 10000.0
NORM_EPS = 1e-6

HGRN_HEADS = 8
HGRN_DK = 128
HGRN_W = 1024
DIFF_HEADS = 8
DIFF_DK = 64
GQA_HEADS = 8
GQA_KV_HEADS = 2
GQA_DH = 128
GQA_REP = 4
N_BRANCH = 3
BRANCH_W = 1024
PEER_HEADS = 8
N_KEYS = 128
N_EXPERTS = N_KEYS * N_KEYS
PEER_TOPK = 16

TP = BATCH * SEQ
TS = DEC_BATCH * DEC_SEQ
TT = TP + TS

C_HQ, C_HF0, C_HF1, C_HI, C_HG = 0, 1024, 2048, 3072, 4096
C_DQ, C_DK, C_DV = 5120, 6144, 7168
C_GQ, C_GK, C_GV = 8192, 9216, 9472
C_GATE = 9728

LANES = 128
CH = 128
NCH = TT // CH
NPC = TP // CH
CH_PER_PROMPT = SEQ // CH
HGRN_GROUP = CH_PER_PROMPT
CH_PER_SAMPLE = DEC_SEQ // CH
N_LEVELS = 7
NEG_INF = float("-inf")
MIN_NORMAL = 1.1754944e-38

_NT = (((1,), (1,)), ((), ()))
_TN = (((0,), (0,)), ((), ()))


def _cp(sem, vmem_mb):
    return pltpu.CompilerParams(dimension_semantics=sem, vmem_limit_bytes=vmem_mb * 1024 * 1024)


def _dot(a, b):
    return jnp.dot(a, b, preferred_element_type=F32)


def _dot_nt(a, b):
    return lax.dot_general(a, b, _NT, preferred_element_type=F32)


def _dot_tn(a, b):
    return lax.dot_general(a, b, _TN, preferred_element_type=F32)


def _sigmoid(x):
    return 0.5 * jnp.tanh(0.5 * x) + 0.5


def _split_bf16(x):
    hi = x.astype(BF16)
    lo = (x - hi.astype(F32)).astype(BF16)
    return hi, lo


def _seg_matrix(seg):
    r = lax.broadcasted_iota(jnp.int32, (LANES, LANES), 0) // seg
    c = lax.broadcasted_iota(jnp.int32, (LANES, LANES), 1) // seg
    return (r == c).astype(BF16)


def _rmsnorm_seg(x, gain_row, seg_mat, seg):
    hi, lo = _split_bf16(x * x)
    ss = _dot(hi, seg_mat) + _dot(lo, seg_mat)
    return x * lax.rsqrt(ss * (1.0 / seg) + NORM_EPS) * gain_row


def _mod_row(i, tm):
    n_p = TP // tm
    per = DEC_SEQ // tm
    return jnp.where(i < n_p, 0, 1 + (i - n_p) // per)


def _mod_kernel(cond_ref, w_ref, b_ref, out_ref):
    a = cond_ref[...]
    a = a * jax.nn.sigmoid(a)
    out_ref[0] = _dot(a.astype(BF16), w_ref[0].astype(BF16)) + b_ref[0]


def _modulation(cond8, mod_w, mod_b):
    tn = 1024
    n6 = 6 * D_MODEL
    return pl.pallas_call(
        _mod_kernel,
        grid=(DEPTH, n6 // tn),
        in_specs=[
            pl.BlockSpec((8, D_MODEL), lambda l, j: (0, 0)),
            pl.BlockSpec((1, D_MODEL, tn), lambda l, j: (l, 0, j)),
            pl.BlockSpec((1, 1, tn), lambda l, j: (l, 0, j)),
        ],
        out_specs=pl.BlockSpec((1, 8, tn), lambda l, j: (l, 0, j)),
        out_shape=jax.ShapeDtypeStruct((DEPTH, 8, n6), F32),
        compiler_params=_cp(("parallel", "parallel"), 40),
        name="modulation",
    )(cond8, mod_w, mod_b.reshape(DEPTH, 1, n6))


def _norm_mm_kernel(x_ref, mod_ref, gain_ref, w_ref, *rest, sh_idx, sc_idx, emit_h):
    if emit_h:
        z_ref, h_ref, h_scr = rest
    else:
        z_ref, h_scr = rest

    @pl.when(pl.program_id(1) == 0)
    def _():
        x = x_ref[...]
        ms = jnp.mean(x * x, axis=-1, keepdims=True)
        y = x * lax.rsqrt(ms + NORM_EPS) * gain_ref[...]
        h = y * (1.0 + mod_ref[0, 0, sc_idx:sc_idx + 1, :]) + mod_ref[0, 0, sh_idx:sh_idx + 1, :]
        hb = h.astype(BF16)
        h_scr[...] = hb
        if emit_h:
            h_ref[...] = hb

    z_ref[...] = _dot(h_scr[...], w_ref[0].astype(BF16)).astype(z_ref.dtype)


def _norm_matmul(x, mods4, layer, gain, w, *, sh_idx, sc_idx, tm, tn, out_dtype, emit_h, name):
    n = w.shape[2]
    out_shape = [jax.ShapeDtypeStruct((TT, n), out_dtype)]
    out_specs = [pl.BlockSpec((tm, tn), lambda i, j: (i, j))]
    if emit_h:
        out_shape.append(jax.ShapeDtypeStruct((TT, D_MODEL), BF16))
        out_specs.append(pl.BlockSpec((tm, D_MODEL), lambda i, j: (i, 0)))
    return pl.pallas_call(
        functools.partial(_norm_mm_kernel, sh_idx=sh_idx, sc_idx=sc_idx, emit_h=emit_h),
        grid=(TT // tm, n // tn),
        in_specs=[
            pl.BlockSpec((tm, D_MODEL), lambda i, j: (i, 0)),
            pl.BlockSpec((1, 1, 6, D_MODEL), lambda i, j: (layer, _mod_row(i, tm), 0, 0)),
            pl.BlockSpec((1, D_MODEL), lambda i, j: (0, 0)),
            pl.BlockSpec((1, D_MODEL, tn), lambda i, j: (layer, 0, j)),
        ],
        out_specs=out_specs,
        out_shape=out_shape,
        scratch_shapes=[pltpu.VMEM((tm, D_MODEL), BF16)],
        compiler_params=_cp(("parallel", "arbitrary"), 48),
        name=name,
    )(x, mods4, gain, w)


def _level_table(rev):
    t = np.arange(CH)[:, None]
    s = np.arange(CH)[None, :]
    x = t ^ s
    lev = np.full((CH, CH), -1, np.int32)
    nz = x > 0
    lev[nz] = np.floor(np.log2(x[nz])).astype(np.int32)
    valid = (t < s) if rev else (t > s)
    lev = np.where(valid, lev, -1)
    lev[np.arange(CH), np.arange(CH)] = N_LEVELS
    return lev.astype(np.int32)


def _bmid(b, b3, m, rev):
    off = m if rev else m - 1
    width = b.shape[1]
    if m >= 8:
        pieces = []
        for j in range(CH // (2 * m)):
            idx = j * 2 * m + off
            pieces.append(jnp.broadcast_to(b[idx:idx + 1, :], (2 * m, width)))
        return pieces[0] if len(pieces) == 1 else jnp.concatenate(pieces, axis=0)
    sub = lax.broadcasted_iota(jnp.int32, (CH // 8, 8, width), 1)
    out = None
    for j in range(8 // (2 * m)):
        idx = j * 2 * m + off
        piece = jnp.broadcast_to(b3[:, idx:idx + 1, :], (CH // 8, 8, width))
        out = piece if out is None else jnp.where(sub >= j * 2 * m, piece, out)
    return out.reshape(CH, width)


def _level_operand(q, kk, b, b3, m, rev, row):
    if m < 8:
        e = jnp.exp(-jnp.abs(b - _bmid(b, b3, m, rev)))
        q_side = ((row // m) % 2) == (0 if rev else 1)
        return jnp.where(q_side, q, kk) * e
    pieces = []
    for j in range(CH // (2 * m)):
        lo = slice(j * 2 * m, j * 2 * m + m)
        hi = slice(j * 2 * m + m, (j + 1) * 2 * m)
        mid = j * 2 * m + (m if rev else m - 1)
        bm = b[mid:mid + 1, :]
        if rev:
            pieces += [q[lo] * jnp.exp(b[lo] - bm), kk[hi] * jnp.exp(bm - b[hi])]
        else:
            pieces += [kk[lo] * jnp.exp(bm - b[lo]), q[hi] * jnp.exp(b[hi] - bm)]
    return jnp.concatenate(pieces, axis=0)


def _hgrn_kernel(*refs, rev):
    q_ref, f_ref, v_ref, lb_ref, s0_ref, lev_ref = refs[:6]
    o_ref, sfin_ref, st_scr = refs[-3:]
    i = pl.program_id(0)
    c = (NCH // HGRN_GROUP - 1 - i) if rev else i
    n_prompt_steps = NPC // HGRN_GROUP
    steps_per_sample = CH_PER_SAMPLE // HGRN_GROUP
    is_prompt = c < n_prompt_steps
    cs = jnp.maximum(c - n_prompt_steps, 0) % steps_per_sample
    start_s = cs == (steps_per_sample - 1 if rev else 0)

    @pl.when(is_prompt)
    def _():
        st_scr[...] = jnp.zeros_like(st_scr)

    @pl.when(jnp.logical_and(jnp.logical_not(is_prompt), start_s))
    def _():
        for h in range(HGRN_HEADS):
            st_scr[h] = s0_ref[0, 0, 0, h].T

    row = lax.broadcasted_iota(jnp.int32, (CH, CH), 0)
    col = lax.broadcasted_iota(jnp.int32, (CH, CH), 1)
    tri = ((col >= row) if rev else (col <= row)).astype(BF16)
    lev = lev_ref[...]
    lb = lb_ref[...]
    row_w = lax.broadcasted_iota(jnp.int32, (CH, HGRN_W), 0)
    blocks = [slice(r * 8, (r + 1) * 8) for r in range(CH // 8)]

    for sub in (reversed(range(HGRN_GROUP)) if rev else range(HGRN_GROUP)):
        rows = slice(sub * CH, (sub + 1) * CH)
        q = q_ref[rows, :].astype(F32)
        zf = f_ref[rows, :].astype(F32)
        f = lb + (1.0 - lb) * jax.nn.sigmoid(zf)
        kk = (1.0 - lb) * jax.nn.sigmoid(-zf)
        hi, lo = _split_bf16(jnp.log(jnp.maximum(f, MIN_NORMAL)))
        b = _dot(tri, hi) + _dot(tri, lo)
        b3 = b.reshape(CH // 8, 8, HGRN_W)
        xs = [_level_operand(q, kk, b, b3, 1 << lm, rev, row_w).astype(BF16) for lm in range(N_LEVELS)]
        qb = q.astype(BF16)
        kb = kk.astype(BF16)
        vb = v_ref[rows, :].astype(BF16)
        b_end = b[0:1, :] if rev else b[CH - 1:CH, :]
        qd = (q * jnp.exp(b)).astype(BF16)
        kd = (kk * jnp.exp(b_end - b)).astype(BF16)
        dec = jnp.exp(b_end)

        for h in range(HGRN_HEADS):
            sl = slice(h * HGRN_DK, (h + 1) * HGRN_DK)
            p = _dot_nt(qb[:, sl], kb[:, sl])
            a_rows = [jnp.where(lev[rs] == N_LEVELS, p[rs], 0.0) for rs in blocks]
            for lm in range(N_LEVELS):
                p = _dot_nt(xs[lm][:, sl], xs[lm][:, sl])
                for r, rs in enumerate(blocks):
                    if lm >= 3 and ((r >> (lm - 3)) & 1) == (1 if rev else 0):
                        continue
                    a_rows[r] = jnp.where(lev[rs] == lm, p[rs], a_rows[r])
            a = jnp.concatenate(a_rows, axis=0)

            st = st_scr[h]
            o_ref[rows, sl] = _dot(a.astype(BF16), vb[:, sl]) + _dot_nt(qd[:, sl], st.astype(BF16))
            st_scr[h] = st * dec[:, sl] + _dot_tn(vb[:, sl], kd[:, sl])

    @pl.when(is_prompt)
    def _():
        for h in range(HGRN_HEADS):
            sfin_ref[0, 0, 0, h] = st_scr[h].T


def _hgrn(z, lb_l, state_hgrn, layer, rev, prev=None):
    d = 1 if rev else 0
    lev = jnp.asarray(_level_table(rev))
    n_steps = NCH // HGRN_GROUP
    n_prompt_steps = NPC // HGRN_GROUP
    rows = HGRN_GROUP * CH

    def cidx(i):
        return (n_steps - 1 - i) if rev else i

    def s0_map(i):
        b = jnp.clip((cidx(i) - n_prompt_steps) // (CH_PER_SAMPLE // HGRN_GROUP), 0, DEC_BATCH - 1)
        return (b, layer, d, 0, 0, 0)

    def sfin_map(i):
        return (jnp.minimum(cidx(i), BATCH - 1), layer, d, 0, 0, 0)

    wblk = HGRN_W
    in_specs = [
        pl.BlockSpec((rows, wblk), lambda i: (cidx(i), C_HQ // wblk)),
        pl.BlockSpec((rows, wblk), lambda i: (cidx(i), (C_HF1 if rev else C_HF0) // wblk)),
        pl.BlockSpec((rows, wblk), lambda i: (cidx(i), C_HI // wblk)),
        pl.BlockSpec((1, wblk), lambda i: (0, 0)),
        pl.BlockSpec((1, 1, 1, HGRN_HEADS, HGRN_DK, HGRN_DK), s0_map),
        pl.BlockSpec((CH, CH), lambda i: (0, 0)),
    ]
    args = [z, z, z, lb_l[d:d + 1], state_hgrn, lev]
    aliases = {}
    if prev is not None:
        in_specs.append(pl.BlockSpec(memory_space=pl.ANY))
        args.append(prev)
        aliases = {6: 1}
    o, sfin = pl.pallas_call(
        functools.partial(_hgrn_kernel, rev=rev),
        grid=(n_steps,),
        in_specs=in_specs,
        out_specs=[
            pl.BlockSpec((rows, wblk), lambda i: (cidx(i), 0)),
            pl.BlockSpec((1, 1, 1, HGRN_HEADS, HGRN_DK, HGRN_DK), sfin_map),
        ],
        out_shape=[
            jax.ShapeDtypeStruct((TT, HGRN_W), F32),
            jax.ShapeDtypeStruct((BATCH, DEPTH, 2, HGRN_HEADS, HGRN_DK, HGRN_DK), F32),
        ],
        input_output_aliases=aliases,
        scratch_shapes=[pltpu.VMEM((HGRN_HEADS, HGRN_DK, HGRN_DK), F32)],
        compiler_params=_cp(("arbitrary",), 32),
        name="hgrn_bwd" if rev else "hgrn_fwd",
    )(*args)
    return o, sfin


def _rope_tables(dim, copies):
    nfreq = dim // 4
    inv_freq = ROPE_THETA ** (-np.arange(nfreq, dtype=np.float64) / nfreq)
    t = np.arange(DEC_SEQ)
    pos_row = (t // GRID_W).astype(np.float64)
    pos_col = (t % GRID_W).astype(np.float64)
    lane = np.arange(dim)
    use_col = (lane // (dim // 2)) == 1
    fidx = lane % nfreq
    first = (lane % (dim // 2)) < nfreq
    pos = np.where(use_col[None, :], pos_col[:, None], pos_row[:, None])
    ang = pos * inv_freq[fidx][None, :]
    cos = np.cos(ang)
    sin = np.where(first[None, :], -np.sin(ang), np.sin(ang))
    cos = np.tile(cos, (1, copies)).astype(np.float32)
    sin = np.tile(sin, (1, copies)).astype(np.float32)
    return jnp.asarray(cos), jnp.asarray(sin)


def _rope(x, cos, sin, dim):
    nfreq = dim // 4
    lane = lax.broadcasted_iota(jnp.int32, x.shape, 1)
    first = (lane % (dim // 2)) < nfreq
    partner = jnp.where(first, pltpu.roll(x, LANES - nfreq, 1), pltpu.roll(x, nfreq, 1))
    return x * cos + partner * sin


def _diff_kernel(*refs, latent, lam_init, tq, hpb):
    if latent:
        (q_ref, k_ref, v_ref, nq_ref, nk_ref, sub_ref, lam_ref, cos_ref, sin_ref, ck_ref, cv_ref,
         out_ref, k_scr, v_scr) = refs
    else:
        q_ref, k_ref, v_ref, nq_ref, nk_ref, sub_ref, lam_ref = refs[:7]
        out_ref, ckout_ref, cvout_ref, k_scr, v_scr = refs[-5:]
    n = q_ref.shape[0]
    seg64 = _seg_matrix(DIFF_DK)
    seg128 = _seg_matrix(LANES)
    lp = lam_ref[...]
    lam = (jnp.exp(jnp.sum(lp[0:1] * lp[1:2], axis=-1, keepdims=True))
           - jnp.exp(jnp.sum(lp[2:3] * lp[3:4], axis=-1, keepdims=True)) + lam_init)
    lane = lax.broadcasted_iota(jnp.int32, (tq, LANES), 1)
    scale = DIFF_DK ** -0.5

    for hh in range(hpb):
        hs = slice(hh * LANES, (hh + 1) * LANES)
        qn = _rmsnorm_seg(q_ref[:, hs].astype(F32), nq_ref[...], seg64, DIFF_DK)
        kn = _rmsnorm_seg(k_ref[:, hs].astype(F32), nk_ref[...], seg64, DIFF_DK)
        v = v_ref[:, hs].astype(F32)
        if latent:
            qn = _rope(qn, cos_ref[...], sin_ref[...], DIFF_DK)
            kn = _rope(kn, cos_ref[...], sin_ref[...], DIFF_DK)
            k_scr[hh, n:, :] = ck_ref[0, 0, :, hs].astype(BF16)
            v_scr[hh, n:, :] = cv_ref[0, 0, :, hs].astype(BF16)
        else:
            ckout_ref[0, 0, :, hs] = kn
            cvout_ref[0, 0, :, hs] = v
        k_scr[hh, 0:n, :] = kn.astype(BF16)
        v_scr[hh, 0:n, :] = v.astype(BF16)
        kall = k_scr[hh]
        vall = v_scr[hh]
        for blk in range(n // tq):
            qb = qn[blk * tq:(blk + 1) * tq, :] * scale
            q1 = jnp.where(lane < DIFF_DK, qb, 0.0).astype(BF16)
            q2 = jnp.where(lane >= DIFF_DK, qb, 0.0).astype(BF16)
            s1 = _dot_nt(q1, kall)
            s2 = _dot_nt(q2, kall)
            e1 = jnp.exp(s1 - jnp.max(s1, axis=-1, keepdims=True))
            e2 = jnp.exp(s2 - jnp.max(s2, axis=-1, keepdims=True))
            r1 = 1.0 / jnp.sum(e1, axis=-1, keepdims=True)
            r2 = lam / jnp.sum(e2, axis=-1, keepdims=True)
            p = e1 * r1 - e2 * r2
            o = _dot(p.astype(BF16), vall)
            on = _rmsnorm_seg(o, sub_ref[...], seg128, LANES) * (1.0 - lam_init)
            out_ref[blk * tq:(blk + 1) * tq, hs] = on.astype(out_ref.dtype)


def _diff_attention(z, nq, nk, sub, lam_p, cache_k, cache_v, layer, lam_init, latent, prev=None):
    hw = 2 * DIFF_DK
    if latent:
        n, nseq, row0, skv, hpb = DEC_SEQ, DEC_BATCH, TP // DEC_SEQ, DEC_SEQ + PAST_LEN, 1
    else:
        n, nseq, row0, skv, hpb = SEQ, BATCH, 0, SEQ, 4
    bw = hpb * hw
    tq = 256
    zspec = lambda c0: pl.BlockSpec((n, bw), lambda s, h: (row0 + s, c0 // bw + h))
    vec = pl.BlockSpec((1, hw), lambda s, h: (0, 0))
    in_specs = [zspec(C_DQ), zspec(C_DK), zspec(C_DV), vec, vec, vec,
                pl.BlockSpec((4, DIFF_DK), lambda s, h: (0, 0))]
    args = [z, z, z, nq, nk, sub, lam_p]
    out_specs = [pl.BlockSpec((n, bw), lambda s, h: (s, h))]
    out_shape = [jax.ShapeDtypeStruct((nseq * n, DIFF_HEADS * hw), BF16)]
    aliases = {}
    if latent:
        cos, sin = _rope_tables(DIFF_DK, 2)
        tab = pl.BlockSpec((DEC_SEQ, hw), lambda s, h: (0, 0))
        cspec = pl.BlockSpec((1, 1, PAST_LEN, bw), lambda s, h: (s, layer, 0, h))
        in_specs += [tab, tab, cspec, cspec]
        args += [cos, sin, cache_k, cache_v]
    else:
        cblk = pl.BlockSpec((1, 1, n, bw), lambda s, h: (s, layer, 0, h))
        out_specs += [cblk, cblk]
        out_shape += [jax.ShapeDtypeStruct((BATCH, DEPTH, SEQ, DIFF_HEADS * hw), F32)] * 2
        if prev is not None:
            in_specs += [pl.BlockSpec(memory_space=pl.ANY)] * 2
            args += list(prev)
            aliases = {7: 1, 8: 2}
    return pl.pallas_call(
        functools.partial(_diff_kernel, latent=latent, lam_init=lam_init, tq=tq, hpb=hpb),
        grid=(nseq, DIFF_HEADS // hpb),
        in_specs=in_specs,
        out_specs=out_specs,
        out_shape=out_shape,
        input_output_aliases=aliases,
        scratch_shapes=[pltpu.VMEM((hpb, skv, hw), BF16), pltpu.VMEM((hpb, skv, hw), BF16)],
        compiler_params=_cp(("parallel", "parallel"), 48),
        name="diff_latent" if latent else "diff_context",
    )(*args)


def _gqa_kernel(*refs, latent, tq):
    if latent:
        (q_ref, k_ref, v_ref, nq_ref, nk_ref, cos_ref, sin_ref, ck_ref, cv_ref,
         out_ref, k_scr, v_scr) = refs
    else:
        q_ref, k_ref, v_ref, nq_ref, nk_ref = refs[:5]
        out_ref, ckout_ref, cvout_ref, k_scr, v_scr = refs[-5:]
    n = q_ref.shape[0]
    seg128 = _seg_matrix(LANES)
    kn = _rmsnorm_seg(k_ref[...].astype(F32), nk_ref[...], seg128, GQA_DH)
    v = v_ref[...].astype(F32)
    if latent:
        kn = _rope(kn, cos_ref[...], sin_ref[...], GQA_DH)
        k_scr[n:, :] = ck_ref[0, 0].astype(BF16)
        v_scr[n:, :] = cv_ref[0, 0].astype(BF16)
    else:
        ckout_ref[0, 0] = kn
        cvout_ref[0, 0] = v
    k_scr[0:n, :] = kn.astype(BF16)
    v_scr[0:n, :] = v.astype(BF16)
    kall = k_scr[...]
    vall = v_scr[...]
    scale = GQA_DH ** -0.5
    for r in range(GQA_REP):
        sl = slice(r * GQA_DH, (r + 1) * GQA_DH)
        qn = _rmsnorm_seg(q_ref[:, sl].astype(F32), nq_ref[...], seg128, GQA_DH)
        if latent:
            qn = _rope(qn, cos_ref[...], sin_ref[...], GQA_DH)
        qb16 = qn.astype(BF16)
        for blk in range(n // tq):
            s = _dot_nt(qb16[blk * tq:(blk + 1) * tq, :], kall) * scale
            e = jnp.exp(s - jnp.max(s, axis=-1, keepdims=True))
            p = e * (1.0 / jnp.sum(e, axis=-1, keepdims=True))
            out_ref[blk * tq:(blk + 1) * tq, sl] = _dot(p.astype(BF16), vall).astype(out_ref.dtype)


def _gqa_attention(z, nq, nk, cache_k, cache_v, layer, latent, prev=None):
    qw = GQA_REP * GQA_DH
    aliases = {}
    if latent:
        n, nseq, row0, skv = DEC_SEQ, DEC_BATCH, TP // DEC_SEQ, DEC_SEQ + PAST_LEN
    else:
        n, nseq, row0, skv = SEQ, BATCH, 0, SEQ
    tq = 256
    vec = pl.BlockSpec((1, GQA_DH), lambda s, g: (0, 0))
    in_specs = [
        pl.BlockSpec((n, qw), lambda s, g: (row0 + s, C_GQ // qw + g)),
        pl.BlockSpec((n, GQA_DH), lambda s, g: (row0 + s, C_GK // GQA_DH + g)),
        pl.BlockSpec((n, GQA_DH), lambda s, g: (row0 + s, C_GV // GQA_DH + g)),
        vec, vec]
    args = [z, z, z, nq, nk]
    out_specs = [pl.BlockSpec((n, qw), lambda s, g: (s, g))]
    out_shape = [jax.ShapeDtypeStruct((nseq * n, GQA_HEADS * GQA_DH), BF16)]
    if latent:
        cos, sin = _rope_tables(GQA_DH, 1)
        tab = pl.BlockSpec((DEC_SEQ, GQA_DH), lambda s, g: (0, 0))
        cspec = pl.BlockSpec((1, 1, PAST_LEN, GQA_DH), lambda s, g: (s, layer, 0, g))
        in_specs += [tab, tab, cspec, cspec]
        args += [cos, sin, cache_k, cache_v]
    else:
        cblk = pl.BlockSpec((1, 1, n, GQA_DH), lambda s, g: (s, layer, 0, g))
        out_specs += [cblk, cblk]
        out_shape += [jax.ShapeDtypeStruct((BATCH, DEPTH, SEQ, GQA_KV_HEADS * GQA_DH), F32)] * 2
        if prev is not None:
            in_specs += [pl.BlockSpec(memory_space=pl.ANY)] * 2
            args += list(prev)
            aliases = {5: 1, 6: 2}
    return pl.pallas_call(
        functools.partial(_gqa_kernel, latent=latent, tq=tq),
        grid=(nseq, GQA_KV_HEADS),
        in_specs=in_specs,
        out_specs=out_specs,
        out_shape=out_shape,
        input_output_aliases=aliases,
        scratch_shapes=[pltpu.VMEM((skv, GQA_DH), BF16), pltpu.VMEM((skv, GQA_DH), BF16)],
        compiler_params=_cp(("parallel", "parallel"), 48),
        name="gqa_latent" if latent else "gqa_context",
    )(*args)


GATE_BLK = 512


def _merge_kernel(of_ref, ob_ref, hg_ref, hn_ref, dp_ref, ds_ref, gp_ref, gs_ref, *rest, n_prompt_blocks):
    g_refs = rest[:-3]
    wb_ref, m_ref, br_scr = rest[-3:]
    i = pl.program_id(0)
    nq = D_MODEL // GATE_BLK

    seg128 = _seg_matrix(LANES)
    for h in range(HGRN_HEADS):
        sl = slice(h * HGRN_DK, (h + 1) * HGRN_DK)
        o = of_ref[:, sl] + ob_ref[:, sl]
        g = hg_ref[:, sl].astype(F32)
        y = _rmsnorm_seg(o, hn_ref[...], seg128, HGRN_DK) * (g * _sigmoid(g))
        br_scr[0, :, sl] = y.astype(BF16)

    @pl.when(i < n_prompt_blocks)
    def _():
        br_scr[1] = dp_ref[...]
        br_scr[2] = gp_ref[...]

    @pl.when(i >= n_prompt_blocks)
    def _():
        br_scr[1] = ds_ref[...]
        br_scr[2] = gs_ref[...]

    for q in range(nq):
        cols = slice(q * GATE_BLK, (q + 1) * GATE_BLK)
        acc = _sigmoid(g_refs[q][...].astype(F32)) * _dot(br_scr[0], wb_ref[0, 0, :, cols])
        acc += _sigmoid(g_refs[nq + q][...].astype(F32)) * _dot(br_scr[1], wb_ref[0, 1, :, cols])
        acc += _sigmoid(g_refs[2 * nq + q][...].astype(F32)) * _dot(br_scr[2], wb_ref[0, 2, :, cols])
        m_ref[:, cols] = acc.astype(m_ref.dtype)


def _merge(o_f, o_b, z, hgrn_norm_row, diff_p, diff_s, gqa_p, gqa_s, w_branch, layer):
    tm = 256
    npb = TP // tm
    nsb = TS // tm
    nq = D_MODEL // GATE_BLK
    row = lambda i: (i, 0)
    prow = lambda i: (jnp.minimum(i, npb - 1), 0)
    srow = lambda i: (jnp.clip(i - npb, 0, nsb - 1), 0)
    gates = [pl.BlockSpec((tm, GATE_BLK), lambda i, b=(C_GATE + c * D_MODEL) // GATE_BLK + q: (i, b))
             for c in range(N_BRANCH) for q in range(nq)]
    return pl.pallas_call(
        functools.partial(_merge_kernel, n_prompt_blocks=npb),
        grid=(TT // tm,),
        in_specs=[
            pl.BlockSpec((tm, HGRN_W), row),
            pl.BlockSpec((tm, HGRN_W), row),
            pl.BlockSpec((tm, HGRN_W), lambda i: (i, C_HG // HGRN_W)),
            pl.BlockSpec((1, HGRN_DK), lambda i: (0, 0)),
            pl.BlockSpec((tm, BRANCH_W), prow),
            pl.BlockSpec((tm, BRANCH_W), srow),
            pl.BlockSpec((tm, BRANCH_W), prow),
            pl.BlockSpec((tm, BRANCH_W), srow),
            *gates,
            pl.BlockSpec((1, N_BRANCH, BRANCH_W, D_MODEL), lambda i: (layer, 0, 0, 0)),
        ],
        out_specs=pl.BlockSpec((tm, D_MODEL), row),
        out_shape=jax.ShapeDtypeStruct((TT, D_MODEL), BF16),
        scratch_shapes=[pltpu.VMEM((N_BRANCH, tm, BRANCH_W), BF16)],
        compiler_params=_cp(("parallel",), 56),
        name="branch_merge",
    )(o_f, o_b, z, hgrn_norm_row, diff_p, diff_s, gqa_p, gqa_s, *([z] * (N_BRANCH * nq)), w_branch)


def _out_proj_kernel(m_ref, w_ref, x_ref, mod_ref, y_ref, *, gate_idx):
    y_ref[...] = x_ref[...] + mod_ref[0, 0, gate_idx:gate_idx + 1, :] * _dot(m_ref[...], w_ref[0])


def _out_proj(m, w_out, x, mods4, layer, gate_idx):
    tm, tn = 512, D_MODEL
    return pl.pallas_call(
        functools.partial(_out_proj_kernel, gate_idx=gate_idx),
        grid=(TT // tm, D_MODEL // tn),
        in_specs=[
            pl.BlockSpec((tm, D_MODEL), lambda i, j: (i, 0)),
            pl.BlockSpec((1, D_MODEL, tn), lambda i, j: (layer, 0, j)),
            pl.BlockSpec((tm, tn), lambda i, j: (i, j)),
            pl.BlockSpec((1, 1, 6, tn), lambda i, j: (layer, _mod_row(i, tm), 0, j)),
        ],
        out_specs=pl.BlockSpec((tm, tn), lambda i, j: (i, j)),
        out_shape=jax.ShapeDtypeStruct((TT, D_MODEL), F32),
        compiler_params=_cp(("parallel", "parallel"), 48),
        name="out_proj",
    )(m, w_out, x, mods4)


_STAIR = [PEER_TOPK // (r + 1) for r in range(8)]


def _extract_top(s, k, want_rank):
    vals = []
    rank = jnp.full(s.shape, float(k), F32) if want_rank else None
    for r in range(k):
        m = jnp.max(s, axis=0, keepdims=True)
        hit = s >= m
        vals.append(m)
        if want_rank:
            rank = jnp.where(hit, float(r), rank)
        s = jnp.where(hit, NEG_INF, s)
    return vals, rank


def _route_kernel(q_ref, keys_ref, r1_ref, r2_ref, *, tm):
    kb = keys_ref[0].astype(BF16)
    sub8 = lax.broadcasted_iota(jnp.int32, (8, LANES), 0)
    for g in range(tm // LANES):
        rows = slice(g * LANES, (g + 1) * LANES)
        cols = slice(g * LANES, (g + 1) * LANES)
        s1 = _dot_nt(kb[0], q_ref[rows, 0:N_KEYS])
        s2 = _dot_nt(kb[1], q_ref[rows, N_KEYS:2 * N_KEYS])
        v1, _ = _extract_top(s1, PEER_TOPK, False)
        v2, rank2 = _extract_top(s2, PEER_TOPK, True)
        sv2 = jnp.concatenate(v2, axis=0)
        sv1_hi = jnp.concatenate(v1[8:], axis=0)
        cand = [v1[0] + sv2, v1[1] + sv2[0:8]]
        for r in range(2, 8):
            cand.append(jnp.where(sub8 < _STAIR[r], v1[r] + sv2[0:8], NEG_INF))
        cand.append(sv1_hi + v2[0])
        cand = jnp.concatenate(cand, axis=0)
        tau = _extract_top(cand, PEER_TOPK, False)[0][-1]
        cmax = v1[0] + v2[0]
        zsum = jnp.sum(jnp.where(cand >= tau, jnp.exp(cand - cmax), 0.0), axis=0, keepdims=True)
        rz = 1.0 / zsum
        cnt = jnp.zeros((N_KEYS, LANES), F32)
        for r in range(PEER_TOPK):
            cnt_r = jnp.sum(jnp.where(v1[r] + sv2 >= tau, 1.0, 0.0), axis=0, keepdims=True)
            cnt = jnp.where(s1 == v1[r], cnt_r, cnt)
        r1_ref[0, 0, :, cols] = cnt
        r1_ref[0, 1, :, cols] = jnp.exp(s1 - v1[0]) * rz
        r2_ref[0, 0, :, cols] = rank2.astype(BF16)
        r2_ref[0, 1, :, cols] = jnp.exp(s2 - v2[0]).astype(BF16)


def _peer_route(qp, peer_keys, layer):
    tm = 512
    blk = pl.BlockSpec((1, 2, N_KEYS, tm), lambda i, h: (h, 0, 0, i))
    return pl.pallas_call(
        functools.partial(_route_kernel, tm=tm),
        grid=(TT // tm, PEER_HEADS),
        in_specs=[
            pl.BlockSpec((tm, 2 * N_KEYS), lambda i, h: (i, h)),
            pl.BlockSpec((1, 2, N_KEYS, N_KEYS), lambda i, h: (layer, 0, 0, 0)),
        ],
        out_specs=[blk, blk],
        out_shape=[jax.ShapeDtypeStruct((PEER_HEADS, 2, N_KEYS, TT), F32),
                   jax.ShapeDtypeStruct((PEER_HEADS, 2, N_KEYS, TT), BF16)],
        compiler_params=_cp(("parallel", "parallel"), 32),
        name="peer_route",
    )(qp, peer_keys)


def _expert_kernel(h_ref, u_ref, v_ref, r1_ref, r2_ref, x_ref, mod_ref, y_ref,
                   r2_scr, h_scr, at_scr, g_scr, acc_scr, *, tm, ce, n_parts, gate_idx):
    j = pl.program_id(1)

    @pl.when(j == 0)
    def _():
        acc_scr[...] = jnp.zeros_like(acc_scr)
        r2_scr[...] = r2_ref[...]
        h_scr[...] = h_ref[...]

    hb = h_scr[...]
    pe = ce // n_parts
    packed = (N_KEYS // 16, 16, LANES)
    contrib = None
    for p in range(n_parts):
        at_scr[p] = _dot_nt(u_ref[0, p * pe:(p + 1) * pe, :], hb)
        for ib in range(pe // N_KEYS):
            il = p * (pe // N_KEYS) + ib
            rows = slice(ib * N_KEYS, (ib + 1) * N_KEYS)
            for g in range(tm // LANES):
                cols = slice(g * LANES, (g + 1) * LANES)
                w = None
                for h in range(PEER_HEADS):
                    rk = r2_scr[h, 0, :, cols].reshape(packed)
                    e2 = r2_scr[h, 1, :, cols].reshape(packed)
                    cnt = jnp.broadcast_to(r1_ref[h, 0, il:il + 1, cols], (16, LANES)).astype(BF16)[None]
                    e1 = jnp.broadcast_to(r1_ref[h, 1, il:il + 1, cols], (16, LANES)).astype(BF16)[None]
                    wh = jnp.where(rk < cnt, e2, jnp.zeros_like(e2)) * e1
                    w = wh if w is None else w + wh
                a = at_scr[p, rows, cols]
                act = 0.5 * a * (1.0 + lax.erf(a * (2.0 ** -0.5)))
                gt = act.astype(BF16).reshape(packed) * w
                g_scr[p, rows, cols] = gt.reshape(N_KEYS, LANES)
        d = _dot_tn(g_scr[p], v_ref[0, p * pe:(p + 1) * pe, :])
        contrib = d if contrib is None else contrib + d
    acc_scr[...] += contrib

    @pl.when(j == pl.num_programs(1) - 1)
    def _():
        y_ref[...] = x_ref[...] + mod_ref[0, 0, gate_idx:gate_idx + 1, :] * acc_scr[...]


def _peer_experts(h2, u, v, r1, r2, x, mods4, layer, gate_idx):
    tm, ce, n_parts = 512, 1024, 2
    n_i1 = ce // N_KEYS
    return pl.pallas_call(
        functools.partial(_expert_kernel, tm=tm, ce=ce, n_parts=n_parts, gate_idx=gate_idx),
        grid=(TT // tm, N_EXPERTS // ce),
        in_specs=[
            pl.BlockSpec((tm, D_MODEL), lambda i, j: (i, 0)),
            pl.BlockSpec((1, ce, D_MODEL), lambda i, j: (layer, j, 0)),
            pl.BlockSpec((1, ce, D_MODEL), lambda i, j: (layer, j, 0)),
            pl.BlockSpec((PEER_HEADS, 2, n_i1, tm), lambda i, j: (0, 0, j, i)),
            pl.BlockSpec((PEER_HEADS, 2, N_KEYS, tm), lambda i, j: (0, 0, 0, i)),
            pl.BlockSpec((tm, D_MODEL), lambda i, j: (i, 0)),
            pl.BlockSpec((1, 1, 6, D_MODEL), lambda i, j: (layer, _mod_row(i, tm), 0, 0)),
        ],
        out_specs=pl.BlockSpec((tm, D_MODEL), lambda i, j: (i, 0)),
        out_shape=jax.ShapeDtypeStruct((TT, D_MODEL), F32),
        scratch_shapes=[
            pltpu.VMEM((PEER_HEADS, 2, N_KEYS, tm), BF16),
            pltpu.VMEM((tm, D_MODEL), BF16),
            pltpu.VMEM((n_parts, ce // n_parts, tm), F32),
            pltpu.VMEM((n_parts, ce // n_parts, tm), BF16),
            pltpu.VMEM((tm, D_MODEL), F32),
        ],
        compiler_params=_cp(("parallel", "arbitrary"), 56),
        name="peer_experts",
    )(h2, u, v, r1, r2, x, mods4)


def kernel(x_prompt, x_sample, c, cache_diff_k, cache_diff_v, cache_gqa_k, cache_gqa_v, state_hgrn, c_ctx,
           mod_w, mod_b, norm_mix, norm_ffn, w_in, hgrn_lb, hgrn_norm, diff_qk_norm, diff_lambda, diff_subln,
           gqa_qk_norm, w_branch, w_out, peer_wq, peer_keys, peer_u, peer_v):
    x = jnp.concatenate([x_prompt.reshape(TP, D_MODEL), x_sample.reshape(TS, D_MODEL)], axis=0)
    cond8 = jnp.concatenate([c_ctx[None, :], c, jnp.zeros((8 - 1 - DEC_BATCH, D_MODEL), F32)], axis=0)
    mods4 = _modulation(cond8, mod_w, mod_b).reshape(DEPTH, 8, 6, D_MODEL)

    lb_all = jnp.cumsum(jax.nn.softmax(hgrn_lb.astype(F32), axis=1), axis=1)
    lb_all = lb_all - lb_all[:, :1]

    w_in_b = w_in
    peer_wq_b = peer_wq.astype(BF16)
    w_branch_b = w_branch.astype(BF16)
    w_out_b = w_out.astype(BF16)
    peer_u_b = peer_u.astype(BF16)
    peer_v_b = peer_v.astype(BF16)

    cdk = cache_diff_k.reshape(DEC_BATCH, DEPTH, PAST_LEN, DIFF_HEADS * 2 * DIFF_DK)
    cdv = cache_diff_v.reshape(DEC_BATCH, DEPTH, PAST_LEN, DIFF_HEADS * 2 * DIFF_DK)
    cgk = cache_gqa_k.reshape(DEC_BATCH, DEPTH, PAST_LEN, GQA_KV_HEADS * GQA_DH)
    cgv = cache_gqa_v.reshape(DEC_BATCH, DEPTH, PAST_LEN, GQA_KV_HEADS * GQA_DH)

    new_state = None
    diff_kv = None
    gqa_kv = None
    for l in range(DEPTH):
        lam_init = 0.8 - 0.6 * math.exp(-0.3 * l)
        z = _norm_matmul(x, mods4, l, norm_mix[l][None, :], w_in_b, sh_idx=0, sc_idx=1,
                         tm=1024, tn=512, out_dtype=BF16, emit_h=False, name="in_proj")[0]

        o_f, new_state = _hgrn(z, lb_all[:, l], state_hgrn, l, rev=False, prev=new_state)
        o_b, new_state = _hgrn(z, lb_all[:, l], state_hgrn, l, rev=True, prev=new_state)

        nq_d = jnp.tile(diff_qk_norm[l, 0], 2)[None, :]
        nk_d = jnp.tile(diff_qk_norm[l, 1], 2)[None, :]
        sub = diff_subln[l][None, :]
        diff_p, *diff_kv = _diff_attention(z, nq_d, nk_d, sub, diff_lambda[l], None, None, l,
                                           lam_init, latent=False, prev=diff_kv)
        diff_s = _diff_attention(z, nq_d, nk_d, sub, diff_lambda[l], cdk, cdv, l, lam_init, latent=True)[0]

        nq_g = gqa_qk_norm[l, 0][None, :]
        nk_g = gqa_qk_norm[l, 1][None, :]
        gqa_p, *gqa_kv = _gqa_attention(z, nq_g, nk_g, None, None, l, latent=False, prev=gqa_kv)
        gqa_s = _gqa_attention(z, nq_g, nk_g, cgk, cgv, l, latent=True)[0]

        m = _merge(o_f, o_b, z, hgrn_norm[l][None, :], diff_p, diff_s, gqa_p, gqa_s, w_branch_b, l)
        x = _out_proj(m, w_out_b, x, mods4, l, gate_idx=2)

        qp, h2 = _norm_matmul(x, mods4, l, norm_ffn[l][None, :], peer_wq_b, sh_idx=3, sc_idx=4,
                              tm=512, tn=D_MODEL, out_dtype=BF16, emit_h=True, name="peer_query")
        r1, r2 = _peer_route(qp, peer_keys, l)
        x = _peer_experts(h2, peer_u_b, peer_v_b, r1, r2, x, mods4, l, gate_idx=5)

    return (x[:TP].reshape(BATCH, SEQ, D_MODEL), x[TP:].reshape(DEC_BATCH, DEC_SEQ, D_MODEL),
            diff_kv[0].reshape(BATCH, DEPTH, SEQ, DIFF_HEADS, 2 * DIFF_DK),
            diff_kv[1].reshape(BATCH, DEPTH, SEQ, DIFF_HEADS, 2 * DIFF_DK),
            gqa_kv[0].reshape(BATCH, DEPTH, SEQ, GQA_KV_HEADS, GQA_DH),
            gqa_kv[1].reshape(BATCH, DEPTH, SEQ, GQA_KV_HEADS, GQA_DH),
            new_state)
```

```python
import functools
import math

import numpy as np
import jax
import jax.numpy as jnp
from jax import lax
from jax.experimental import pallas as pl
from jax.experimental.pallas import tpu as pltpu

F32 = jnp.float32
BF16 = jnp.bfloat16

D_MODEL = 2048
BATCH = 16
SEQ = 256
DEPTH = 2
DEC_BATCH = 2
DEC_SEQ = 1024
PAST_LEN = 256
GRID_W = 64
ROPE_THETA = 10000.0
NORM_EPS = 1e-6

HGRN_HEADS = 8
HGRN_DK = 128
HGRN_W = 1024
DIFF_HEADS = 8
DIFF_DK = 64
GQA_HEADS = 8
GQA_KV_HEADS = 2
GQA_DH = 128
GQA_REP = 4
N_BRANCH = 3
BRANCH_W = 1024
PEER_HEADS = 8
N_KEYS = 128
N_EXPERTS = N_KEYS * N_KEYS
PEER_TOPK = 16

TP = BATCH * SEQ
TS = DEC_BATCH * DEC_SEQ
TT = TP + TS

C_HQ, C_HF0, C_HF1, C_HI, C_HG = 0, 1024, 2048, 3072, 4096
C_DQ, C_DK, C_DV = 5120, 6144, 7168
C_GQ, C_GK, C_GV = 8192, 9216, 9472
C_GATE = 9728

LANES = 128
CH = 128
NCH = TT // CH
NPC = TP // CH
CH_PER_PROMPT = SEQ // CH
HGRN_GROUP = CH_PER_PROMPT
CH_PER_SAMPLE = DEC_SEQ // CH
N_LEVELS = 7
NEG_INF = float("-inf")
MIN_NORMAL = 1.1754944e-38

_NT = (((1,), (1,)), ((), ()))
_TN = (((0,), (0,)), ((), ()))


def _cp(sem, vmem_mb):
    return pltpu.CompilerParams(dimension_semantics=sem, vmem_limit_bytes=vmem_mb * 1024 * 1024)


def _dot(a, b):
    return jnp.dot(a, b, preferred_element_type=F32)


def _dot_nt(a, b):
    return lax.dot_general(a, b, _NT, preferred_element_type=F32)


def _dot_tn(a, b):
    return lax.dot_general(a, b, _TN, preferred_element_type=F32)


def _sigmoid(x):
    return 0.5 * jnp.tanh(0.5 * x) + 0.5


def _split_bf16(x):
    hi = x.astype(BF16)
    lo = (x - hi.astype(F32)).astype(BF16)
    return hi, lo


def _seg_matrix(seg):
    r = lax.broadcasted_iota(jnp.int32, (LANES, LANES), 0) // seg
    c = lax.broadcasted_iota(jnp.int32, (LANES, LANES), 1) // seg
    return (r == c).astype(BF16)


def _rmsnorm_seg(x, gain_row, seg_mat, seg):
    hi, lo = _split_bf16(x * x)
    ss = _dot(hi, seg_mat) + _dot(lo, seg_mat)
    return x * lax.rsqrt(ss * (1.0 / seg) + NORM_EPS) * gain_row


def _mod_row(i, tm):
    n_p = TP // tm
    per = DEC_SEQ // tm
    return jnp.where(i < n_p, 0, 1 + (i - n_p) // per)


def _mod_kernel(cond_ref, w_ref, b_ref, out_ref):
    a = cond_ref[...]
    a = a * jax.nn.sigmoid(a)
    out_ref[0] = _dot(a.astype(BF16), w_ref[0].astype(BF16)) + b_ref[0]


def _modulation(cond8, mod_w, mod_b):
    tn = 1024
    n6 = 6 * D_MODEL
    return pl.pallas_call(
        _mod_kernel,
        grid=(DEPTH, n6 // tn),
        in_specs=[
            pl.BlockSpec((8, D_MODEL), lambda l, j: (0, 0)),
            pl.BlockSpec((1, D_MODEL, tn), lambda l, j: (l, 0, j)),
            pl.BlockSpec((1, 1, tn), lambda l, j: (l, 0, j)),
        ],
        out_specs=pl.BlockSpec((1, 8, tn), lambda l, j: (l, 0, j)),
        out_shape=jax.ShapeDtypeStruct((DEPTH, 8, n6), F32),
        compiler_params=_cp(("parallel", "parallel"), 40),
        name="modulation",
    )(cond8, mod_w, mod_b.reshape(DEPTH, 1, n6))


def _norm_mm_kernel(x_ref, mod_ref, gain_ref, w_ref, *rest, sh_idx, sc_idx, emit_h):
    if emit_h:
        z_ref, h_ref, h_scr = rest
    else:
        z_ref, h_scr = rest

    @pl.when(pl.program_id(1) == 0)
    def _():
        x = x_ref[...]
        ms = jnp.mean(x * x, axis=-1, keepdims=True)
        y = x * lax.rsqrt(ms + NORM_EPS) * gain_ref[...]
        h = y * (1.0 + mod_ref[0, 0, sc_idx:sc_idx + 1, :]) + mod_ref[0, 0, sh_idx:sh_idx + 1, :]
        hb = h.astype(BF16)
        h_scr[...] = hb
        if emit_h:
            h_ref[...] = hb

    z_ref[...] = _dot(h_scr[...], w_ref[0].astype(BF16)).astype(z_ref.dtype)


def _norm_matmul(x, mods4, layer, gain, w, *, sh_idx, sc_idx, tm, tn, out_dtype, emit_h, name):
    n = w.shape[2]
    out_shape = [jax.ShapeDtypeStruct((TT, n), out_dtype)]
    out_specs = [pl.BlockSpec((tm, tn), lambda i, j: (i, j))]
    if emit_h:
        out_shape.append(jax.ShapeDtypeStruct((TT, D_MODEL), BF16))
        out_specs.append(pl.BlockSpec((tm, D_MODEL), lambda i, j: (i, 0)))
    return pl.pallas_call(
        functools.partial(_norm_mm_kernel, sh_idx=sh_idx, sc_idx=sc_idx, emit_h=emit_h),
        grid=(TT // tm, n // tn),
        in_specs=[
            pl.BlockSpec((tm, D_MODEL), lambda i, j: (i, 0)),
            pl.BlockSpec((1, 1, 6, D_MODEL), lambda i, j: (layer, _mod_row(i, tm), 0, 0)),
            pl.BlockSpec((1, D_MODEL), lambda i, j: (0, 0)),
            pl.BlockSpec((1, D_MODEL, tn), lambda i, j: (layer, 0, j)),
        ],
        out_specs=out_specs,
        out_shape=out_shape,
        scratch_shapes=[pltpu.VMEM((tm, D_MODEL), BF16)],
        compiler_params=_cp(("parallel", "arbitrary"), 48),
        name=name,
    )(x, mods4, gain, w)


def _level_table(rev):
    t = np.arange(CH)[:, None]
    s = np.arange(CH)[None, :]
    x = t ^ s
    lev = np.full((CH, CH), -1, np.int32)
    nz = x > 0
    lev[nz] = np.floor(np.log2(x[nz])).astype(np.int32)
    valid = (t < s) if rev else (t > s)
    lev = np.where(valid, lev, -1)
    lev[np.arange(CH), np.arange(CH)] = N_LEVELS
    return lev.astype(np.int32)


def _bmid(b, b3, m, rev):
    off = m if rev else m - 1
    width = b.shape[1]
    if m >= 8:
        pieces = []
        for j in range(CH // (2 * m)):
            idx = j * 2 * m + off
            pieces.append(jnp.broadcast_to(b[idx:idx + 1, :], (2 * m, width)))
        return pieces[0] if len(pieces) == 1 else jnp.concatenate(pieces, axis=0)
    sub = lax.broadcasted_iota(jnp.int32, (CH // 8, 8, width), 1)
    out = None
    for j in range(8 // (2 * m)):
        idx = j * 2 * m + off
        piece = jnp.broadcast_to(b3[:, idx:idx + 1, :], (CH // 8, 8, width))
        out = piece if out is None else jnp.where(sub >= j * 2 * m, piece, out)
    return out.reshape(CH, width)


def _level_operand(q, kk, b, b3, m, rev, row):
    if m < 8:
        e = jnp.exp(-jnp.abs(b - _bmid(b, b3, m, rev)))
        q_side = ((row // m) % 2) == (0 if rev else 1)
        return jnp.where(q_side, q, kk) * e
    pieces = []
    for j in range(CH // (2 * m)):
        lo = slice(j * 2 * m, j * 2 * m + m)
        hi = slice(j * 2 * m + m, (j + 1) * 2 * m)
        mid = j * 2 * m + (m if rev else m - 1)
        bm = b[mid:mid + 1, :]
        if rev:
            pieces += [q[lo] * jnp.exp(b[lo] - bm), kk[hi] * jnp.exp(bm - b[hi])]
        else:
            pieces += [kk[lo] * jnp.exp(bm - b[lo]), q[hi] * jnp.exp(b[hi] - bm)]
    return jnp.concatenate(pieces, axis=0)


def _hgrn_kernel(*refs, rev):
    q_ref, f_ref, v_ref, lb_ref, s0_ref, lev_ref = refs[:6]
    o_ref, sfin_ref, st_scr = refs[-3:]
    i = pl.program_id(0)
    c = (NCH // HGRN_GROUP - 1 - i) if rev else i
    n_prompt_steps = NPC // HGRN_GROUP
    steps_per_sample = CH_PER_SAMPLE // HGRN_GROUP
    is_prompt = c < n_prompt_steps
    cs = jnp.maximum(c - n_prompt_steps, 0) % steps_per_sample
    start_s = cs == (steps_per_sample - 1 if rev else 0)

    @pl.when(is_prompt)
    def _():
        st_scr[...] = jnp.zeros_like(st_scr)

    @pl.when(jnp.logical_and(jnp.logical_not(is_prompt), start_s))
    def _():
        for h in range(HGRN_HEADS):
            st_scr[h] = s0_ref[0, 0, 0, h].T

    row = lax.broadcasted_iota(jnp.int32, (CH, CH), 0)
    col = lax.broadcasted_iota(jnp.int32, (CH, CH), 1)
    tri = ((col >= row) if rev else (col <= row)).astype(BF16)
    lev = lev_ref[...]
    lb = lb_ref[...]
    row_w = lax.broadcasted_iota(jnp.int32, (CH, HGRN_W), 0)
    blocks = [slice(r * 8, (r + 1) * 8) for r in range(CH // 8)]

    for sub in (reversed(range(HGRN_GROUP)) if rev else range(HGRN_GROUP)):
        rows = slice(sub * CH, (sub + 1) * CH)
        q = q_ref[rows, :].astype(F32)
        zf = f_ref[rows, :].astype(F32)
        f = lb + (1.0 - lb) * jax.nn.sigmoid(zf)
        kk = (1.0 - lb) * jax.nn.sigmoid(-zf)
        hi, lo = _split_bf16(jnp.log(jnp.maximum(f, MIN_NORMAL)))
        b = _dot(tri, hi) + _dot(tri, lo)
        b3 = b.reshape(CH // 8, 8, HGRN_W)
        xs = [_level_operand(q, kk, b, b3, 1 << lm, rev, row_w).astype(BF16) for lm in range(N_LEVELS)]
        qb = q.astype(BF16)
        kb = kk.astype(BF16)
        vb = v_ref[rows, :].astype(BF16)
        b_end = b[0:1, :] if rev else b[CH - 1:CH, :]
        qd = (q * jnp.exp(b)).astype(BF16)
        kd = (kk * jnp.exp(b_end - b)).astype(BF16)
        dec = jnp.exp(b_end)

        for h in range(HGRN_HEADS):
            sl = slice(h * HGRN_DK, (h + 1) * HGRN_DK)
            p = _dot_nt(qb[:, sl], kb[:, sl])
            a_rows = [jnp.where(lev[rs] == N_LEVELS, p[rs], 0.0) for rs in blocks]
            for lm in range(N_LEVELS):
                p = _dot_nt(xs[lm][:, sl], xs[lm][:, sl])
                for r, rs in enumerate(blocks):
                    if lm >= 3 and ((r >> (lm - 3)) & 1) == (1 if rev else 0):
                        continue
                    a_rows[r] = jnp.where(lev[rs] == lm, p[rs], a_rows[r])
            a = jnp.concatenate(a_rows, axis=0)

            st = st_scr[h]
            o_ref[rows, sl] = _dot(a.astype(BF16), vb[:, sl]) + _dot_nt(qd[:, sl], st.astype(BF16))
            st_scr[h] = st * dec[:, sl] + _dot_tn(vb[:, sl], kd[:, sl])

    @pl.when(is_prompt)
    def _():
        for h in range(HGRN_HEADS):
            sfin_ref[0, 0, 0, h] = st_scr[h].T


def _hgrn(z, lb_l, state_hgrn, layer, rev, prev=None):
    d = 1 if rev else 0
    lev = jnp.asarray(_level_table(rev))
    n_steps = NCH // HGRN_GROUP
    n_prompt_steps = NPC // HGRN_GROUP
    rows = HGRN_GROUP * CH

    def cidx(i):
        return (n_steps - 1 - i) if rev else i

    def s0_map(i):
        b = jnp.clip((cidx(i) - n_prompt_steps) // (CH_PER_SAMPLE // HGRN_GROUP), 0, DEC_BATCH - 1)
        return (b, layer, d, 0, 0, 0)

    def sfin_map(i):
        return (jnp.minimum(cidx(i), BATCH - 1), layer, d, 0, 0, 0)

    wblk = HGRN_W
    in_specs = [
        pl.BlockSpec((rows, wblk), lambda i: (cidx(i), C_HQ // wblk)),
        pl.BlockSpec((rows, wblk), lambda i: (cidx(i), (C_HF1 if rev else C_HF0) // wblk)),
        pl.BlockSpec((rows, wblk), lambda i: (cidx(i), C_HI // wblk)),
        pl.BlockSpec((1, wblk), lambda i: (0, 0)),
        pl.BlockSpec((1, 1, 1, HGRN_HEADS, HGRN_DK, HGRN_DK), s0_map),
        pl.BlockSpec((CH, CH), lambda i: (0, 0)),
    ]
    args = [z, z, z, lb_l[d:d + 1], state_hgrn, lev]
    aliases = {}
    if prev is not None:
        in_specs.append(pl.BlockSpec(memory_space=pl.ANY))
        args.append(prev)
        aliases = {6: 1}
    o, sfin = pl.pallas_call(
        functools.partial(_hgrn_kernel, rev=rev),
        grid=(n_steps,),
        in_specs=in_specs,
        out_specs=[
            pl.BlockSpec((rows, wblk), lambda i: (cidx(i), 0)),
            pl.BlockSpec((1, 1, 1, HGRN_HEADS, HGRN_DK, HGRN_DK), sfin_map),
        ],
        out_shape=[
            jax.ShapeDtypeStruct((TT, HGRN_W), F32),
            jax.ShapeDtypeStruct((BATCH, DEPTH, 2, HGRN_HEADS, HGRN_DK, HGRN_DK), F32),
        ],
        input_output_aliases=aliases,
        scratch_shapes=[pltpu.VMEM((HGRN_HEADS, HGRN_DK, HGRN_DK), F32)],
        compiler_params=_cp(("arbitrary",), 32),
        name="hgrn_bwd" if rev else "hgrn_fwd",
    )(*args)
    return o, sfin


def _rope_tables(dim, copies):
    nfreq = dim // 4
    inv_freq = ROPE_THETA ** (-np.arange(nfreq, dtype=np.float64) / nfreq)
    t = np.arange(DEC_SEQ)
    pos_row = (t // GRID_W).astype(np.float64)
    pos_col = (t % GRID_W).astype(np.float64)
    lane = np.arange(dim)
    use_col = (lane // (dim // 2)) == 1
    fidx = lane % nfreq
    first = (lane % (dim // 2)) < nfreq
    pos = np.where(use_col[None, :], pos_col[:, None], pos_row[:, None])
    ang = pos * inv_freq[fidx][None, :]
    cos = np.cos(ang)
    sin = np.where(first[None, :], -np.sin(ang), np.sin(ang))
    cos = np.tile(cos, (1, copies)).astype(np.float32)
    sin = np.tile(sin, (1, copies)).astype(np.float32)
    return jnp.asarray(cos), jnp.asarray(sin)


def _rope(x, cos, sin, dim):
    nfreq = dim // 4
    lane = lax.broadcasted_iota(jnp.int32, x.shape, 1)
    first = (lane % (dim // 2)) < nfreq
    partner = jnp.where(first, pltpu.roll(x, LANES - nfreq, 1), pltpu.roll(x, nfreq, 1))
    return x * cos + partner * sin


def _diff_kernel(*refs, latent, lam_init, tq, hpb):
    if latent:
        (q_ref, k_ref, v_ref, nq_ref, nk_ref, sub_ref, lam_ref, cos_ref, sin_ref, ck_ref, cv_ref,
         out_ref, k_scr, v_scr) = refs
    else:
        q_ref, k_ref, v_ref, nq_ref, nk_ref, sub_ref, lam_ref = refs[:7]
        out_ref, ckout_ref, cvout_ref, k_scr, v_scr = refs[-5:]
    n = q_ref.shape[0]
    seg64 = _seg_matrix(DIFF_DK)
    seg128 = _seg_matrix(LANES)
    lp = lam_ref[...]
    lam = (jnp.exp(jnp.sum(lp[0:1] * lp[1:2], axis=-1, keepdims=True))
           - jnp.exp(jnp.sum(lp[2:3] * lp[3:4], axis=-1, keepdims=True)) + lam_init)
    lane = lax.broadcasted_iota(jnp.int32, (tq, LANES), 1)
    scale = DIFF_DK ** -0.5

    for hh in range(hpb):
        hs = slice(hh * LANES, (hh + 1) * LANES)
        qn = _rmsnorm_seg(q_ref[:, hs].astype(F32), nq_ref[...], seg64, DIFF_DK)
        kn = _rmsnorm_seg(k_ref[:, hs].astype(F32), nk_ref[...], seg64, DIFF_DK)
        v = v_ref[:, hs].astype(F32)
        if latent:
            qn = _rope(qn, cos_ref[...], sin_ref[...], DIFF_DK)
            kn = _rope(kn, cos_ref[...], sin_ref[...], DIFF_DK)
            k_scr[hh, n:, :] = ck_ref[0, 0, :, hs].astype(BF16)
            v_scr[hh, n:, :] = cv_ref[0, 0, :, hs].astype(BF16)
        else:
            ckout_ref[0, 0, :, hs] = kn
            cvout_ref[0, 0, :, hs] = v
        k_scr[hh, 0:n, :] = kn.astype(BF16)
        v_scr[hh, 0:n, :] = v.astype(BF16)
        kall = k_scr[hh]
        vall = v_scr[hh]
        for blk in range(n // tq):
            qb = qn[blk * tq:(blk + 1) * tq, :] * scale
            q1 = jnp.where(lane < DIFF_DK, qb, 0.0).astype(BF16)
            q2 = jnp.where(lane >= DIFF_DK, qb, 0.0).astype(BF16)
            s1 = _dot_nt(q1, kall)
            s2 = _dot_nt(q2, kall)
            e1 = jnp.exp(s1 - jnp.max(s1, axis=-1, keepdims=True))
            e2 = jnp.exp(s2 - jnp.max(s2, axis=-1, keepdims=True))
            r1 = 1.0 / jnp.sum(e1, axis=-1, keepdims=True)
            r2 = lam / jnp.sum(e2, axis=-1, keepdims=True)
            p = e1 * r1 - e2 * r2
            o = _dot(p.astype(BF16), vall)
            on = _rmsnorm_seg(o, sub_ref[...], seg128, LANES) * (1.0 - lam_init)
            out_ref[blk * tq:(blk + 1) * tq, hs] = on.astype(out_ref.dtype)


def _diff_attention(z, nq, nk, sub, lam_p, cache_k, cache_v, layer, lam_init, latent, prev=None):
    hw = 2 * DIFF_DK
    if latent:
        n, nseq, row0, skv, hpb = DEC_SEQ, DEC_BATCH, TP // DEC_SEQ, DEC_SEQ + PAST_LEN, 1
    else:
        n, nseq, row0, skv, hpb = SEQ, BATCH, 0, SEQ, 4
    bw = hpb * hw
    tq = 256
    zspec = lambda c0: pl.BlockSpec((n, bw), lambda s, h: (row0 + s, c0 // bw + h))
    vec = pl.BlockSpec((1, hw), lambda s, h: (0, 0))
    in_specs = [zspec(C_DQ), zspec(C_DK), zspec(C_DV), vec, vec, vec,
                pl.BlockSpec((4, DIFF_DK), lambda s, h: (0, 0))]
    args = [z, z, z, nq, nk, sub, lam_p]
    out_specs = [pl.BlockSpec((n, bw), lambda s, h: (s, h))]
    out_shape = [jax.ShapeDtypeStruct((nseq * n, DIFF_HEADS * hw), BF16)]
    aliases = {}
    if latent:
        cos, sin = _rope_tables(DIFF_DK, 2)
        tab = pl.BlockSpec((DEC_SEQ, hw), lambda s, h: (0, 0))
        cspec = pl.BlockSpec((1, 1, PAST_LEN, bw), lambda s, h: (s, layer, 0, h))
        in_specs += [tab, tab, cspec, cspec]
        args += [cos, sin, cache_k, cache_v]
    else:
        cblk = pl.BlockSpec((1, 1, n, bw), lambda s, h: (s, layer, 0, h))
        out_specs += [cblk, cblk]
        out_shape += [jax.ShapeDtypeStruct((BATCH, DEPTH, SEQ, DIFF_HEADS * hw), F32)] * 2
        if prev is not None:
            in_specs += [pl.BlockSpec(memory_space=pl.ANY)] * 2
            args += list(prev)
            aliases = {7: 1, 8: 2}
    return pl.pallas_call(
        functools.partial(_diff_kernel, latent=latent, lam_init=lam_init, tq=tq, hpb=hpb),
        grid=(nseq, DIFF_HEADS // hpb),
        in_specs=in_specs,
        out_specs=out_specs,
        out_shape=out_shape,
        input_output_aliases=aliases,
        scratch_shapes=[pltpu.VMEM((hpb, skv, hw), BF16), pltpu.VMEM((hpb, skv, hw), BF16)],
        compiler_params=_cp(("parallel", "parallel"), 48),
        name="diff_latent" if latent else "diff_context",
    )(*args)


def _gqa_kernel(*refs, latent, tq):
    if latent:
        (q_ref, k_ref, v_ref, nq_ref, nk_ref, cos_ref, sin_ref, ck_ref, cv_ref,
         out_ref, k_scr, v_scr) = refs
    else:
        q_ref, k_ref, v_ref, nq_ref, nk_ref = refs[:5]
        out_ref, ckout_ref, cvout_ref, k_scr, v_scr = refs[-5:]
    n = q_ref.shape[0]
    seg128 = _seg_matrix(LANES)
    kn = _rmsnorm_seg(k_ref[...].astype(F32), nk_ref[...], seg128, GQA_DH)
    v = v_ref[...].astype(F32)
    if latent:
        kn = _rope(kn, cos_ref[...], sin_ref[...], GQA_DH)
        k_scr[n:, :] = ck_ref[0, 0].astype(BF16)
        v_scr[n:, :] = cv_ref[0, 0].astype(BF16)
    else:
        ckout_ref[0, 0] = kn
        cvout_ref[0, 0] = v
    k_scr[0:n, :] = kn.astype(BF16)
    v_scr[0:n, :] = v.astype(BF16)
    kall = k_scr[...]
    vall = v_scr[...]
    scale = GQA_DH ** -0.5
    for r in range(GQA_REP):
        sl = slice(r * GQA_DH, (r + 1) * GQA_DH)
        qn = _rmsnorm_seg(q_ref[:, sl].astype(F32), nq_ref[...], seg128, GQA_DH)
        if latent:
            qn = _rope(qn, cos_ref[...], sin_ref[...], GQA_DH)
        qb16 = qn.astype(BF16)
        for blk in range(n // tq):
            s = _dot_nt(qb16[blk * tq:(blk + 1) * tq, :], kall) * scale
            e = jnp.exp(s - jnp.max(s, axis=-1, keepdims=True))
            p = e * (1.0 / jnp.sum(e, axis=-1, keepdims=True))
            out_ref[blk * tq:(blk + 1) * tq, sl] = _dot(p.astype(BF16), vall).astype(out_ref.dtype)


def _gqa_attention(z, nq, nk, cache_k, cache_v, layer, latent, prev=None):
    qw = GQA_REP * GQA_DH
    aliases = {}
    if latent:
        n, nseq, row0, skv = DEC_SEQ, DEC_BATCH, TP // DEC_SEQ, DEC_SEQ + PAST_LEN
    else:
        n, nseq, row0, skv = SEQ, BATCH, 0, SEQ
    tq = 256
    vec = pl.BlockSpec((1, GQA_DH), lambda s, g: (0, 0))
    in_specs = [
        pl.BlockSpec((n, qw), lambda s, g: (row0 + s, C_GQ // qw + g)),
        pl.BlockSpec((n, GQA_DH), lambda s, g: (row0 + s, C_GK // GQA_DH + g)),
        pl.BlockSpec((n, GQA_DH), lambda s, g: (row0 + s, C_GV // GQA_DH + g)),
        vec, vec]
    args = [z, z, z, nq, nk]
    out_specs = [pl.BlockSpec((n, qw), lambda s, g: (s, g))]
    out_shape = [jax.ShapeDtypeStruct((nseq * n, GQA_HEADS * GQA_DH), BF16)]
    if latent:
        cos, sin = _rope_tables(GQA_DH, 1)
        tab = pl.BlockSpec((DEC_SEQ, GQA_DH), lambda s, g: (0, 0))
        cspec = pl.BlockSpec((1, 1, PAST_LEN, GQA_DH), lambda s, g: (s, layer, 0, g))
        in_specs += [tab, tab, cspec, cspec]
        args += [cos, sin, cache_k, cache_v]
    else:
        cblk = pl.BlockSpec((1, 1, n, GQA_DH), lambda s, g: (s, layer, 0, g))
        out_specs += [cblk, cblk]
        out_shape += [jax.ShapeDtypeStruct((BATCH, DEPTH, SEQ, GQA_KV_HEADS * GQA_DH), F32)] * 2
        if prev is not None:
            in_specs += [pl.BlockSpec(memory_space=pl.ANY)] * 2
            args += list(prev)
            aliases = {5: 1, 6: 2}
    return pl.pallas_call(
        functools.partial(_gqa_kernel, latent=latent, tq=tq),
        grid=(nseq, GQA_KV_HEADS),
        in_specs=in_specs,
        out_specs=out_specs,
        out_shape=out_shape,
        input_output_aliases=aliases,
        scratch_shapes=[pltpu.VMEM((skv, GQA_DH), BF16), pltpu.VMEM((skv, GQA_DH), BF16)],
        compiler_params=_cp(("parallel", "parallel"), 48),
        name="gqa_latent" if latent else "gqa_context",
    )(*args)


GATE_BLK = 512


def _merge_kernel(of_ref, ob_ref, hg_ref, hn_ref, dp_ref, ds_ref, gp_ref, gs_ref, *rest, n_prompt_blocks):
    g_refs = rest[:-3]
    wb_ref, m_ref, br_scr = rest[-3:]
    i = pl.program_id(0)
    nq = D_MODEL // GATE_BLK

    seg128 = _seg_matrix(LANES)
    for h in range(HGRN_HEADS):
        sl = slice(h * HGRN_DK, (h + 1) * HGRN_DK)
        o = of_ref[:, sl] + ob_ref[:, sl]
        g = hg_ref[:, sl].astype(F32)
        y = _rmsnorm_seg(o, hn_ref[...], seg128, HGRN_DK) * (g * _sigmoid(g))
        br_scr[0, :, sl] = y.astype(BF16)

    @pl.when(i < n_prompt_blocks)
    def _():
        br_scr[1] = dp_ref[...]
        br_scr[2] = gp_ref[...]

    @pl.when(i >= n_prompt_blocks)
    def _():
        br_scr[1] = ds_ref[...]
        br_scr[2] = gs_ref[...]

    for q in range(nq):
        cols = slice(q * GATE_BLK, (q + 1) * GATE_BLK)
        acc = _sigmoid(g_refs[q][...].astype(F32)) * _dot(br_scr[0], wb_ref[0, 0, :, cols])
        acc += _sigmoid(g_refs[nq + q][...].astype(F32)) * _dot(br_scr[1], wb_ref[0, 1, :, cols])
        acc += _sigmoid(g_refs[2 * nq + q][...].astype(F32)) * _dot(br_scr[2], wb_ref[0, 2, :, cols])
        m_ref[:, cols] = acc.astype(m_ref.dtype)


def _merge(o_f, o_b, z, hgrn_norm_row, diff_p, diff_s, gqa_p, gqa_s, w_branch, layer):
    tm = 256
    npb = TP // tm
    nsb = TS // tm
    nq = D_MODEL // GATE_BLK
    row = lambda i: (i, 0)
    prow = lambda i: (jnp.minimum(i, npb - 1), 0)
    srow = lambda i: (jnp.clip(i - npb, 0, nsb - 1), 0)
    gates = [pl.BlockSpec((tm, GATE_BLK), lambda i, b=(C_GATE + c * D_MODEL) // GATE_BLK + q: (i, b))
             for c in range(N_BRANCH) for q in range(nq)]
    return pl.pallas_call(
        functools.partial(_merge_kernel, n_prompt_blocks=npb),
        grid=(TT // tm,),
        in_specs=[
            pl.BlockSpec((tm, HGRN_W), row),
            pl.BlockSpec((tm, HGRN_W), row),
            pl.BlockSpec((tm, HGRN_W), lambda i: (i, C_HG // HGRN_W)),
            pl.BlockSpec((1, HGRN_DK), lambda i: (0, 0)),
            pl.BlockSpec((tm, BRANCH_W), prow),
            pl.BlockSpec((tm, BRANCH_W), srow),
            pl.BlockSpec((tm, BRANCH_W), prow),
            pl.BlockSpec((tm, BRANCH_W), srow),
            *gates,
            pl.BlockSpec((1, N_BRANCH, BRANCH_W, D_MODEL), lambda i: (layer, 0, 0, 0)),
        ],
        out_specs=pl.BlockSpec((tm, D_MODEL), row),
        out_shape=jax.ShapeDtypeStruct((TT, D_MODEL), BF16),
        scratch_shapes=[pltpu.VMEM((N_BRANCH, tm, BRANCH_W), BF16)],
        compiler_params=_cp(("parallel",), 56),
        name="branch_merge",
    )(o_f, o_b, z, hgrn_norm_row, diff_p, diff_s, gqa_p, gqa_s, *([z] * (N_BRANCH * nq)), w_branch)


def _out_proj_kernel(m_ref, w_ref, x_ref, mod_ref, y_ref, *, gate_idx):
    y_ref[...] = x_ref[...] + mod_ref[0, 0, gate_idx:gate_idx + 1, :] * _dot(m_ref[...], w_ref[0])


def _out_proj(m, w_out, x, mods4, layer, gate_idx):
    tm, tn = 1024, D_MODEL
    return pl.pallas_call(
        functools.partial(_out_proj_kernel, gate_idx=gate_idx),
        grid=(TT // tm, D_MODEL // tn),
        in_specs=[
            pl.BlockSpec((tm, D_MODEL), lambda i, j: (i, 0)),
            pl.BlockSpec((1, D_MODEL, tn), lambda i, j: (layer, 0, j), pipeline_mode=pl.Buffered(1)),
            pl.BlockSpec((tm, tn), lambda i, j: (i, j)),
            pl.BlockSpec((1, 1, 6, tn), lambda i, j: (layer, _mod_row(i, tm), 0, j)),
        ],
        out_specs=pl.BlockSpec((tm, tn), lambda i, j: (i, j)),
        out_shape=jax.ShapeDtypeStruct((TT, D_MODEL), F32),
        compiler_params=_cp(("parallel", "parallel"), 56),
        name="out_proj",
    )(m, w_out, x, mods4)


_STAIR = [PEER_TOPK // (r + 1) for r in range(8)]


def _extract_top(s, k, want_rank):
    vals = []
    rank = jnp.full(s.shape, float(k), F32) if want_rank else None
    for r in range(k):
        m = jnp.max(s, axis=0, keepdims=True)
        hit = s >= m
        vals.append(m)
        if want_rank:
            rank = jnp.where(hit, float(r), rank)
        s = jnp.where(hit, NEG_INF, s)
    return vals, rank


def _route_kernel(q_ref, keys_ref, r1_ref, r2_ref, *, tm):
    kb = keys_ref[0].astype(BF16)
    sub8 = lax.broadcasted_iota(jnp.int32, (8, LANES), 0)
    for g in range(tm // LANES):
        rows = slice(g * LANES, (g + 1) * LANES)
        cols = slice(g * LANES, (g + 1) * LANES)
        s1 = _dot_nt(kb[0], q_ref[rows, 0:N_KEYS])
        s2 = _dot_nt(kb[1], q_ref[rows, N_KEYS:2 * N_KEYS])
        v1, _ = _extract_top(s1, PEER_TOPK, False)
        v2, rank2 = _extract_top(s2, PEER_TOPK, True)
        sv2 = jnp.concatenate(v2, axis=0)
        sv1_hi = jnp.concatenate(v1[8:], axis=0)
        cand = [v1[0] + sv2, v1[1] + sv2[0:8]]
        for r in range(2, 8):
            cand.append(jnp.where(sub8 < _STAIR[r], v1[r] + sv2[0:8], NEG_INF))
        cand.append(sv1_hi + v2[0])
        cand = jnp.concatenate(cand, axis=0)
        tau = _extract_top(cand, PEER_TOPK, False)[0][-1]
        cmax = v1[0] + v2[0]
        zsum = jnp.sum(jnp.where(cand >= tau, jnp.exp(cand - cmax), 0.0), axis=0, keepdims=True)
        rz = 1.0 / zsum
        cnt = jnp.zeros((N_KEYS, LANES), F32)
        for r in range(PEER_TOPK):
            cnt_r = jnp.sum(jnp.where(v1[r] + sv2 >= tau, 1.0, 0.0), axis=0, keepdims=True)
            cnt = jnp.where(s1 == v1[r], cnt_r, cnt)
        r1_ref[0, 0, :, cols] = cnt
        r1_ref[0, 1, :, cols] = jnp.exp(s1 - v1[0]) * rz
        r2_ref[0, 0, :, cols] = rank2.astype(BF16)
        r2_ref[0, 1, :, cols] = jnp.exp(s2 - v2[0]).astype(BF16)


def _peer_route(qp, peer_keys, layer):
    tm = 512
    blk = pl.BlockSpec((1, 2, N_KEYS, tm), lambda i, h: (h, 0, 0, i))
    return pl.pallas_call(
        functools.partial(_route_kernel, tm=tm),
        grid=(TT // tm, PEER_HEADS),
        in_specs=[
            pl.BlockSpec((tm, 2 * N_KEYS), lambda i, h: (i, h)),
            pl.BlockSpec((1, 2, N_KEYS, N_KEYS), lambda i, h: (layer, 0, 0, 0)),
        ],
        out_specs=[blk, blk],
        out_shape=[jax.ShapeDtypeStruct((PEER_HEADS, 2, N_KEYS, TT), F32),
                   jax.ShapeDtypeStruct((PEER_HEADS, 2, N_KEYS, TT), BF16)],
        compiler_params=_cp(("parallel", "parallel"), 32),
        name="peer_route",
    )(qp, peer_keys)


def _expert_kernel(h_ref, u_ref, v_ref, r1_ref, r2_ref, x_ref, mod_ref, y_ref,
                   r2_scr, h_scr, at_scr, g_scr, acc_scr, *, tm, ce, n_parts, gate_idx):
    j = pl.program_id(1)

    @pl.when(j == 0)
    def _():
        acc_scr[...] = jnp.zeros_like(acc_scr)
        r2_scr[...] = r2_ref[...]
        h_scr[...] = h_ref[...]

    hb = h_scr[...]
    pe = ce // n_parts
    packed = (N_KEYS // 16, 16, LANES)
    contrib = None
    for p in range(n_parts):
        at_scr[p] = _dot_nt(u_ref[0, p * pe:(p + 1) * pe, :], hb)
        for ib in range(pe // N_KEYS):
            il = p * (pe // N_KEYS) + ib
            rows = slice(ib * N_KEYS, (ib + 1) * N_KEYS)
            for g in range(tm // LANES):
                cols = slice(g * LANES, (g + 1) * LANES)
                w = None
                for h in range(PEER_HEADS):
                    rk = r2_scr[h, 0, :, cols].reshape(packed)
                    e2 = r2_scr[h, 1, :, cols].reshape(packed)
                    cnt = jnp.broadcast_to(r1_ref[h, 0, il:il + 1, cols], (16, LANES)).astype(BF16)[None]
                    e1 = jnp.broadcast_to(r1_ref[h, 1, il:il + 1, cols], (16, LANES)).astype(BF16)[None]
                    wh = jnp.where(rk < cnt, e2, jnp.zeros_like(e2)) * e1
                    w = wh if w is None else w + wh
                a = at_scr[p, rows, cols]
                act = 0.5 * a * (1.0 + lax.erf(a * (2.0 ** -0.5)))
                gt = act.astype(BF16).reshape(packed) * w
                g_scr[p, rows, cols] = gt.reshape(N_KEYS, LANES)
        d = _dot_tn(g_scr[p], v_ref[0, p * pe:(p + 1) * pe, :])
        contrib = d if contrib is None else contrib + d
    acc_scr[...] += contrib

    @pl.when(j == pl.num_programs(1) - 1)
    def _():
        y_ref[...] = x_ref[...] + mod_ref[0, 0, gate_idx:gate_idx + 1, :] * acc_scr[...]


def _peer_experts(h2, u, v, r1, r2, x, mods4, layer, gate_idx):
    tm, ce, n_parts = 512, 1024, 2
    n_i1 = ce // N_KEYS
    return pl.pallas_call(
        functools.partial(_expert_kernel, tm=tm, ce=ce, n_parts=n_parts, gate_idx=gate_idx),
        grid=(TT // tm, N_EXPERTS // ce),
        in_specs=[
            pl.BlockSpec((tm, D_MODEL), lambda i, j: (i, 0)),
            pl.BlockSpec((1, ce, D_MODEL), lambda i, j: (layer, j, 0)),
            pl.BlockSpec((1, ce, D_MODEL), lambda i, j: (layer, j, 0)),
            pl.BlockSpec((PEER_HEADS, 2, n_i1, tm), lambda i, j: (0, 0, j, i)),
            pl.BlockSpec((PEER_HEADS, 2, N_KEYS, tm), lambda i, j: (0, 0, 0, i)),
            pl.BlockSpec((tm, D_MODEL), lambda i, j: (i, 0)),
            pl.BlockSpec((1, 1, 6, D_MODEL), lambda i, j: (layer, _mod_row(i, tm), 0, 0)),
        ],
        out_specs=pl.BlockSpec((tm, D_MODEL), lambda i, j: (i, 0)),
        out_shape=jax.ShapeDtypeStruct((TT, D_MODEL), F32),
        scratch_shapes=[
            pltpu.VMEM((PEER_HEADS, 2, N_KEYS, tm), BF16),
            pltpu.VMEM((tm, D_MODEL), BF16),
            pltpu.VMEM((n_parts, ce // n_parts, tm), F32),
            pltpu.VMEM((n_parts, ce // n_parts, tm), BF16),
            pltpu.VMEM((tm, D_MODEL), F32),
        ],
        compiler_params=_cp(("parallel", "arbitrary"), 56),
        name="peer_experts",
    )(h2, u, v, r1, r2, x, mods4)


def kernel(x_prompt, x_sample, c, cache_diff_k, cache_diff_v, cache_gqa_k, cache_gqa_v, state_hgrn, c_ctx,
           mod_w, mod_b, norm_mix, norm_ffn, w_in, hgrn_lb, hgrn_norm, diff_qk_norm, diff_lambda, diff_subln,
           gqa_qk_norm, w_branch, w_out, peer_wq, peer_keys, peer_u, peer_v):
    x = jnp.concatenate([x_prompt.reshape(TP, D_MODEL), x_sample.reshape(TS, D_MODEL)], axis=0)
    cond8 = jnp.concatenate([c_ctx[None, :], c, jnp.zeros((8 - 1 - DEC_BATCH, D_MODEL), F32)], axis=0)
    mods4 = _modulation(cond8, mod_w, mod_b).reshape(DEPTH, 8, 6, D_MODEL)

    lb_all = jnp.cumsum(jax.nn.softmax(hgrn_lb.astype(F32), axis=1), axis=1)
    lb_all = lb_all - lb_all[:, :1]

    w_in_b = w_in
    peer_wq_b = peer_wq.astype(BF16)
    w_branch_b = w_branch.astype(BF16)
    w_out_b = w_out.astype(BF16)
    peer_u_b = peer_u.astype(BF16)
    peer_v_b = peer_v.astype(BF16)

    cdk = cache_diff_k.reshape(DEC_BATCH, DEPTH, PAST_LEN, DIFF_HEADS * 2 * DIFF_DK)
    cdv = cache_diff_v.reshape(DEC_BATCH, DEPTH, PAST_LEN, DIFF_HEADS * 2 * DIFF_DK)
    cgk = cache_gqa_k.reshape(DEC_BATCH, DEPTH, PAST_LEN, GQA_KV_HEADS * GQA_DH)
    cgv = cache_gqa_v.reshape(DEC_BATCH, DEPTH, PAST_LEN, GQA_KV_HEADS * GQA_DH)

    new_state = None
    diff_kv = None
    gqa_kv = None
    for l in range(DEPTH):
        lam_init = 0.8 - 0.6 * math.exp(-0.3 * l)
        z = _norm_matmul(x, mods4, l, norm_mix[l][None, :], w_in_b, sh_idx=0, sc_idx=1,
                         tm=1024, tn=512, out_dtype=BF16, emit_h=False, name="in_proj")[0]

        o_f, new_state = _hgrn(z, lb_all[:, l], state_hgrn, l, rev=False, prev=new_state)
        o_b, new_state = _hgrn(z, lb_all[:, l], state_hgrn, l, rev=True, prev=new_state)

        nq_d = jnp.tile(diff_qk_norm[l, 0], 2)[None, :]
        nk_d = jnp.tile(diff_qk_norm[l, 1], 2)[None, :]
        sub = diff_subln[l][None, :]
        diff_p, *diff_kv = _diff_attention(z, nq_d, nk_d, sub, diff_lambda[l], None, None, l,
                                           lam_init, latent=False, prev=diff_kv)
        diff_s = _diff_attention(z, nq_d, nk_d, sub, diff_lambda[l], cdk, cdv, l, lam_init, latent=True)[0]

        nq_g = gqa_qk_norm[l, 0][None, :]
        nk_g = gqa_qk_norm[l, 1][None, :]
        gqa_p, *gqa_kv = _gqa_attention(z, nq_g, nk_g, None, None, l, latent=False, prev=gqa_kv)
        gqa_s = _gqa_attention(z, nq_g, nk_g, cgk, cgv, l, latent=True)[0]

        m = _merge(o_f, o_b, z, hgrn_norm[l][None, :], diff_p, diff_s, gqa_p, gqa_s, w_branch_b, l)
        x = _out_proj(m, w_out_b, x, mods4, l, gate_idx=2)

        qp, h2 = _norm_matmul(x, mods4, l, norm_ffn[l][None, :], peer_wq_b, sh_idx=3, sc_idx=4,
                              tm=512, tn=D_MODEL, out_dtype=BF16, emit_h=True, name="peer_query")
        r1, r2 = _peer_route(qp, peer_keys, l)
        x = _peer_experts(h2, peer_u_b, peer_v_b, r1, r2, x, mods4, l, gate_idx=5)

    return (x[:TP].reshape(BATCH, SEQ, D_MODEL), x[TP:].reshape(DEC_BATCH, DEC_SEQ, D_MODEL),
            diff_kv[0].reshape(BATCH, DEPTH, SEQ, DIFF_HEADS, 2 * DIFF_DK),
            diff_kv[1].reshape(BATCH, DEPTH, SEQ, DIFF_HEADS, 2 * DIFF_DK),
            gqa_kv[0].reshape(BATCH, DEPTH, SEQ, GQA_KV_HEADS, GQA_DH),
            gqa_kv[1].reshape(BATCH, DEPTH, SEQ, GQA_KV_HEADS, GQA_DH),
            new_state)
```
